```python
import math
import jax, jax.numpy as jnp
from jax import lax
import numpy as np

D_MODEL = 1024
BATCH = 16
SEQ = 4096
DEPTH = 4

N_A_LAYERS = DEPTH // 2
N_B_LAYERS = DEPTH - N_A_LAYERS
LRU_WIDTH = D_MODEL
LRU_BLOCKS = 8
LRU_BLOCK_W = LRU_WIDTH // LRU_BLOCKS
CONV_W = 4
LRU_C = 8.0
N_HEADS = 8
HEAD_DIM = D_MODEL // N_HEADS // 2
V_DIM = 2 * HEAD_DIM
QK_WIDTH = N_HEADS * 2 * HEAD_DIM
V_WIDTH = N_HEADS * V_DIM
D_FF = 4 * D_MODEL
Q_BLOCK = 128
NORM_EPS = 1e-6
SUBLN_EPS = 1e-5

kernel_name = "hawk_yoco_diff_attention_trunk"


def rmsnorm(x, g, eps=NORM_EPS):
    x32 = x.astype(jnp.float32)
    y = x32 * lax.rsqrt(jnp.mean(x32 * x32, axis=-1, keepdims=True) + eps)
    return (y * g.astype(jnp.float32)).astype(x.dtype)


def alibi_slopes(n_heads):
    return jnp.asarray(np.array([2.0 ** (-8.0 * (h + 1) / n_heads) for h in range(n_heads)], dtype=np.float32))


def sq_relu_mlp(h, w1, w2):
    u = jax.nn.relu(h @ w1)
    return (u * u) @ w2


def _lru_combine(c1, c2):
    a1, b1 = c1
    a2, b2 = c2
    return a1 * a2, a2 * b1 + b2


def recurrent_block(h, w_in, b_in, conv_w, conv_b, gate_w, gate_b, lam, w_out):
    B, S, _ = h.shape
    xy = h @ w_in + b_in
    xb, yb = jnp.split(xy, 2, axis=-1)
    gate = jax.nn.gelu(yb)
    xc = lax.conv_general_dilated(
        xb, conv_w[:, None, :], window_strides=(1,), padding=((CONV_W - 1, 0),),
        dimension_numbers=('NWC', 'WIO', 'NWC'), feature_group_count=LRU_WIDTH) + conv_b
    x32 = xc.astype(jnp.float32)
    xg = x32.reshape(B, S, LRU_BLOCKS, LRU_BLOCK_W)
    g = jnp.einsum('bsnc,gncd->gbsnd', xg, gate_w.astype(jnp.float32)).reshape(2, B, S, LRU_WIDTH)
    g = g + gate_b.astype(jnp.float32)[:, None, None, :]
    r = jax.nn.sigmoid(g[0])
    i = jax.nn.sigmoid(g[1])
    log_a = -LRU_C * r * jax.nn.softplus(-lam.astype(jnp.float32))
    a = jnp.exp(log_a)
    bx = jnp.sqrt(-jnp.expm1(2.0 * log_a)) * (i * x32)
    _, hs = lax.associative_scan(_lru_combine, (a, bx), axis=1)
    return (hs.astype(h.dtype) * gate) @ w_out


def diff_attention(h, w_q, lam_vecs, subln_g, w_o, k, v, slopes, lam_init):
    B, S, _ = h.shape
    nb = S // Q_BLOCK
    q = (h @ w_q).reshape(B, nb, Q_BLOCK, N_HEADS, 2, HEAD_DIM).transpose(1, 0, 2, 3, 4, 5)
    lv = lam_vecs.astype(jnp.float32)
    lam = jnp.exp(jnp.sum(lv[0] * lv[1])) - jnp.exp(jnp.sum(lv[2] * lv[3])) + lam_init
    scale = HEAD_DIM ** -0.5
    key_pos = jnp.arange(S)

    def block(args):
        qb, bi = args
        q_pos = bi * Q_BLOCK + jnp.arange(Q_BLOCK)
        rel = q_pos[:, None] - key_pos[None, :]
        bias = jnp.where(rel[None] >= 0, -slopes[:, None, None] * rel[None].astype(jnp.float32), -jnp.inf)
        s = jnp.einsum('bqhcd,bkhcd->bhcqk', qb, k).astype(jnp.float32) * scale
        p = jax.nn.softmax(s + bias[None, :, None], axis=-1)
        w = p[:, :, 0] - lam * p[:, :, 1]
        return jnp.einsum('bhqk,bkhe->bqhe', w.astype(v.dtype), v)

    o = lax.map(block, (q, jnp.arange(nb)))
    o = o.transpose(1, 0, 2, 3, 4).reshape(B, S, N_HEADS, V_DIM)
    o = rmsnorm(o, subln_g, SUBLN_EPS) * (1.0 - lam_init)
    return o.reshape(B, S, V_WIDTH) @ w_o


def setup_inputs(seed: int = 0) -> dict:
    key = jax.random.key(seed)
    ks = jax.random.split(key, 24)
    f32 = jnp.float32
    nA, nB = N_A_LAYERS, N_B_LAYERS
    res_scale = (2.0 * DEPTH) ** -0.5

    def nrm(k, shape, fan_in, extra=1.0):
        return jax.random.normal(k, shape, f32) * (fan_in ** -0.5) * extra

    def gain(k, shape):
        return 1.0 + 0.02 * jax.random.normal(k, shape, f32)

    a0 = jax.random.uniform(ks[0], (nA, LRU_WIDTH), f32, 0.9, 0.999)
    s0 = a0 ** (1.0 / LRU_C)
    a_lambda = jnp.log(s0) - jnp.log1p(-s0)
    return {
        "x": jax.random.normal(ks[1], (BATCH, SEQ, D_MODEL), f32),
        "a_norm": gain(ks[2], (nA, D_MODEL)),
        "a_w_in": nrm(ks[3], (nA, D_MODEL, 2 * LRU_WIDTH), D_MODEL),
        "a_b_in": 0.01 * jax.random.normal(ks[4], (nA, 2 * LRU_WIDTH), f32),
        "a_conv_w": nrm(ks[5], (nA, CONV_W, LRU_WIDTH), CONV_W),
        "a_conv_b": 0.01 * jax.random.normal(ks[6], (nA, LRU_WIDTH), f32),
        "a_gate_w": nrm(ks[7], (nA, 2, LRU_BLOCKS, LRU_BLOCK_W, LRU_BLOCK_W), LRU_BLOCK_W),
        "a_gate_b": 0.01 * jax.random.normal(ks[8], (nA, 2, LRU_WIDTH), f32),
        "a_lambda": a_lambda,
        "a_w_out": nrm(ks[9], (nA, LRU_WIDTH, D_MODEL), LRU_WIDTH, res_scale),
        "kv_norm": gain(ks[10], (D_MODEL,)),
        "w_kv": nrm(ks[11], (D_MODEL, QK_WIDTH + V_WIDTH), D_MODEL),
        "b_norm": gain(ks[12], (nB, D_MODEL)),
        "b_w_q": nrm(ks[13], (nB, D_MODEL, QK_WIDTH), D_MODEL),
        "b_lam": 0.1 * jax.random.normal(ks[14], (nB, 4, HEAD_DIM), f32),
        "b_subln": gain(ks[15], (nB, V_DIM)),
        "b_w_o": nrm(ks[16], (nB, V_WIDTH, D_MODEL), V_WIDTH, res_scale),
        "mlp_norm": gain(ks[17], (DEPTH, D_MODEL)),
        "mlp_w1": nrm(ks[18], (DEPTH, D_MODEL, D_FF), D_MODEL),
        "mlp_w2": nrm(ks[19], (DEPTH, D_FF, D_MODEL), D_FF, res_scale),
        "final_norm": gain(ks[20], (D_MODEL,)),
    }


def reference(x, a_norm, a_w_in, a_b_in, a_conv_w, a_conv_b, a_gate_w, a_gate_b, a_lambda, a_w_out,
              kv_norm, w_kv, b_norm, b_w_q, b_lam, b_subln, b_w_o,
              mlp_norm, mlp_w1, mlp_w2, final_norm):
    B, S, _ = x.shape
    slopes = alibi_slopes(N_HEADS)
    k = None
    v = None
    for l in range(DEPTH):
        if l < N_A_LAYERS:
            x = x + recurrent_block(rmsnorm(x, a_norm[l]), a_w_in[l], a_b_in[l], a_conv_w[l], a_conv_b[l],
                                    a_gate_w[l], a_gate_b[l], a_lambda[l], a_w_out[l])
        else:
            j = l - N_A_LAYERS
            lam_init = 0.8 - 0.6 * math.exp(-0.3 * l)
            x = x + diff_attention(rmsnorm(x, b_norm[j]), b_w_q[j], b_lam[j], b_subln[j], b_w_o[j],
                                   k, v, slopes, lam_init)
        x = x + sq_relu_mlp(rmsnorm(x, mlp_norm[l]), mlp_w1[l], mlp_w2[l])
        if l == N_A_LAYERS - 1:
            kv = rmsnorm(x, kv_norm) @ w_kv
            k = kv[..., :QK_WIDTH].reshape(B, S, N_HEADS, 2, HEAD_DIM)
            v = kv[..., QK_WIDTH:].reshape(B, S, N_HEADS, V_DIM)
    return rmsnorm(x, final_norm)
```

```python
import functools
import math

import jax
import jax.numpy as jnp
from jax import lax
from jax.experimental import pallas as pl
from jax.experimental.pallas import tpu as pltpu

D_MODEL = 1024
N_HEADS = 8
HEAD_DIM = 64
V_DIM = 128
LRU_BLOCKS = 8
LRU_BLOCK_W = 128
CONV_W = 4
LRU_C = 8.0
NORM_EPS = 1e-6
SUBLN_EPS = 1e-5

SUBLANES = 8
VMEM_LIMIT = 56 * 1024 * 1024

BF16 = jnp.bfloat16
F32 = jnp.float32


def _rmsnorm(x, g, eps):
    return x * lax.rsqrt(jnp.mean(x * x, axis=-1, keepdims=True) + eps) * g


def _const_spec(shape):
    nd = len(shape)
    return pl.BlockSpec(shape, lambda *_: (0,) * nd)


def _rec_kernel(x_ref, g_ref, wx_ref, wy_ref, bx_ref, by_ref, cw_ref, cb_ref, gw_ref, gb_ref,
                lam_ref, wo_ref, o_ref, xpad_ref, a_ref, b_ref, h_ref, *, ts):
    s = pl.program_id(1)
    W = D_MODEL

    @pl.when(s == 0)
    def _():
        xpad_ref[0:SUBLANES, :] = jnp.zeros((SUBLANES, W), F32)
        h_ref[...] = jnp.zeros_like(h_ref)

    x = x_ref[0]
    hn = _rmsnorm(x, g_ref[...], NORM_EPS).astype(BF16)
    xb = jnp.dot(hn, wx_ref[...], preferred_element_type=F32) + bx_ref[...]
    yb = jnp.dot(hn, wy_ref[...], preferred_element_type=F32) + by_ref[...]
    gate = jax.nn.gelu(yb, approximate=True)

    xpad_ref[SUBLANES:SUBLANES + ts, :] = xb
    xc = xb * cw_ref[3:4, :] + cb_ref[...]
    for j in range(CONV_W - 1):
        shift = CONV_W - 1 - j
        xc = xc + xpad_ref[SUBLANES - shift:SUBLANES - shift + ts, :] * cw_ref[j:j + 1, :]
    xpad_ref[0:SUBLANES, :] = xpad_ref[ts:ts + SUBLANES, :]

    xcb = xc.astype(BF16)
    r_parts, i_parts = [], []
    for n in range(LRU_BLOCKS):
        gn = jnp.dot(xcb[:, n * LRU_BLOCK_W:(n + 1) * LRU_BLOCK_W], gw_ref[n],
                     preferred_element_type=F32)
        r_parts.append(gn[:, :LRU_BLOCK_W])
        i_parts.append(gn[:, LRU_BLOCK_W:])
    r = jax.nn.sigmoid(jnp.concatenate(r_parts, axis=1) + gb_ref[0:1, :])
    ig = jax.nn.sigmoid(jnp.concatenate(i_parts, axis=1) + gb_ref[1:2, :])

    z = -lam_ref[...]
    softplus = jnp.maximum(z, 0.0) + jnp.log1p(jnp.exp(-jnp.abs(z)))
    log_a = (-LRU_C * softplus) * r
    a = jnp.exp(log_a)
    bx = jnp.sqrt(-jnp.tanh(log_a) * (a * a + 1.0)) * (ig * xc)

    G = ts // SUBLANES
    a3 = a.reshape(G, SUBLANES, W)
    b3 = bx.reshape(G, SUBLANES, W)
    row = lax.broadcasted_iota(jnp.int32, (G, SUBLANES, W), 1)
    for d in (1, 2, 4):
        a_sh = pltpu.roll(a3, d, axis=1)
        b_sh = pltpu.roll(b3, d, axis=1)
        keep = row >= d
        b3 = jnp.where(keep, a3 * b_sh + b3, b3)
        a3 = jnp.where(keep, a3 * a_sh, a3)
    a_ref[...] = a3.reshape(ts, W)
    b_ref[...] = b3.reshape(ts, W)

    def chain(gidx, h):
        off = pl.multiple_of(gidx * SUBLANES, SUBLANES)
        hs = a_ref[pl.ds(off, SUBLANES), :] * h + b_ref[pl.ds(off, SUBLANES), :]
        b_ref[pl.ds(off, SUBLANES), :] = hs
        return hs[SUBLANES - 1:SUBLANES, :]

    h_last = lax.fori_loop(0, G, chain, h_ref[0:1, :], unroll=8)
    h_ref[0:1, :] = h_last

    y = (b_ref[...] * gate).astype(BF16)
    o_ref[0] = x + jnp.dot(y, wo_ref[...], preferred_element_type=F32)


def _recurrent_layer(x, g, w_in, b_in, conv_w, conv_b, gate_w, gate_b, lam, w_out, *, ts=256):
    B, S, D = x.shape
    W = D_MODEL
    wx = w_in[:, :W].astype(BF16)
    wy = w_in[:, W:].astype(BF16)
    bxv = b_in[:W].reshape(1, W)
    byv = b_in[W:].reshape(1, W)
    gw = jnp.concatenate([gate_w[0], gate_w[1]], axis=-1).astype(BF16)
    kern = functools.partial(_rec_kernel, ts=ts)
    return pl.pallas_call(
        kern,
        out_shape=jax.ShapeDtypeStruct((B, S, D), F32),
        grid=(B, S // ts),
        in_specs=[
            pl.BlockSpec((1, ts, D), lambda b, s: (b, s, 0)),
            _const_spec((1, D)),
            _const_spec((D, W)), _const_spec((D, W)),
            _const_spec((1, W)), _const_spec((1, W)),
            _const_spec((CONV_W, W)), _const_spec((1, W)),
            _const_spec((LRU_BLOCKS, LRU_BLOCK_W, 2 * LRU_BLOCK_W)),
            _const_spec((2, W)),
            _const_spec((1, W)),
            _const_spec((W, D)),
        ],
        out_specs=pl.BlockSpec((1, ts, D), lambda b, s: (b, s, 0)),
        scratch_shapes=[
            pltpu.VMEM((ts + SUBLANES, W), F32),
            pltpu.VMEM((ts, W), F32),
            pltpu.VMEM((ts, W), F32),
            pltpu.VMEM((SUBLANES, W), F32),
        ],
        compiler_params=pltpu.CompilerParams(
            dimension_semantics=("parallel", "arbitrary"), vmem_limit_bytes=VMEM_LIMIT),
        name="rglru_layer",
    )(x, g.reshape(1, D), wx, wy, bxv, byv, conv_w, conv_b.reshape(1, W), gw, gate_b,
      lam.reshape(1, W), w_out.astype(BF16))


def _mlp_kernel(x_ref, g_ref, w1_ref, w2_ref, fg_ref, o_ref, *, final_norm):
    x = x_ref[...]
    hn = _rmsnorm(x, g_ref[...], NORM_EPS).astype(BF16)
    u = jnp.maximum(jnp.dot(hn, w1_ref[...], preferred_element_type=F32), 0.0)
    u = (u * u).astype(BF16)
    y = x + jnp.dot(u, w2_ref[...], preferred_element_type=F32)
    if final_norm:
        y = _rmsnorm(y, fg_ref[...], NORM_EPS)
    o_ref[...] = y


def _mlp_layer(x2, g, w1, w2, final_g, *, final_norm, tm=256):
    M, D = x2.shape
    F = w1.shape[1]
    kern = functools.partial(_mlp_kernel, final_norm=final_norm)
    return pl.pallas_call(
        kern,
        out_shape=jax.ShapeDtypeStruct((M, D), F32),
        grid=(M // tm,),
        in_specs=[
            pl.BlockSpec((tm, D), lambda i: (i, 0)),
            _const_spec((1, D)),
            _const_spec((D, F)),
            _const_spec((F, D)),
            _const_spec((1, D)),
        ],
        out_specs=pl.BlockSpec((tm, D), lambda i: (i, 0)),
        compiler_params=pltpu.CompilerParams(
            dimension_semantics=("parallel",), vmem_limit_bytes=VMEM_LIMIT),
        name="mlp_layer",
    )(x2, g.reshape(1, D), w1.astype(BF16), w2.astype(BF16), final_g.reshape(1, D))


def _kv_kernel(x_ref, g_ref, wk_ref, wv_ref, k_ref, vt_ref):
    hn = _rmsnorm(x_ref[0], g_ref[...], NORM_EPS).astype(BF16)
    k_ref[0] = jnp.dot(hn, wk_ref[...], preferred_element_type=F32).astype(BF16)
    v = jnp.dot(hn, wv_ref[...], preferred_element_type=F32)
    vt_ref[0] = v.T.astype(BF16)


def _kv_proj(x, g, w_kv, *, tm=512):
    B, S, D = x.shape
    return pl.pallas_call(
        _kv_kernel,
        out_shape=(jax.ShapeDtypeStruct((B, S, D), BF16), jax.ShapeDtypeStruct((B, D, S), BF16)),
        grid=(B, S // tm),
        in_specs=[
            pl.BlockSpec((1, tm, D), lambda b, s: (b, s, 0)),
            _const_spec((1, D)), _const_spec((D, D)), _const_spec((D, D)),
        ],
        out_specs=(pl.BlockSpec((1, tm, D), lambda b, s: (b, s, 0)),
                   pl.BlockSpec((1, D, tm), lambda b, s: (b, 0, s))),
        compiler_params=pltpu.CompilerParams(
            dimension_semantics=("parallel", "parallel"), vmem_limit_bytes=VMEM_LIMIT),
        name="kv_proj",
    )(x, g.reshape(1, D), w_kv[:, :D].astype(BF16), w_kv[:, D:].astype(BF16))


def _q_kernel(x_ref, g_ref, wq_ref, qt_ref, *, scale):
    hn = _rmsnorm(x_ref[0], g_ref[...], NORM_EPS).astype(BF16)
    q = jnp.dot(hn, wq_ref[...], preferred_element_type=F32) * scale
    qt_ref[0] = q.T.astype(BF16)


def _q_proj(x, g, w_q, *, tm=512):
    B, S, D = x.shape
    kern = functools.partial(_q_kernel, scale=HEAD_DIM ** -0.5)
    return pl.pallas_call(
        kern,
        out_shape=jax.ShapeDtypeStruct((B, D, S), BF16),
        grid=(B, S // tm),
        in_specs=[
            pl.BlockSpec((1, tm, D), lambda b, s: (b, s, 0)),
            _const_spec((1, D)), _const_spec((D, D)),
        ],
        out_specs=pl.BlockSpec((1, D, tm), lambda b, s: (b, 0, s)),
        compiler_params=pltpu.CompilerParams(
            dimension_semantics=("parallel", "parallel"), vmem_limit_bytes=VMEM_LIMIT),
        name="q_proj",
    )(x, g.reshape(1, D), w_q.astype(BF16))


def _attn_kernel(slopes_ref, qt_ref, k_ref, vt_ref, lam_ref, sg_ref, ot_ref,
                 m_ref, l_ref, acc_ref, *, tq, lam_init):
    h = pl.program_id(1)
    qi = pl.program_id(2)
    tk = tq
    slope = slopes_ref[h]

    qt = qt_ref[0]
    comp_row = lax.broadcasted_iota(jnp.int32, (V_DIM, tq), 0)
    zero = jnp.zeros_like(qt)
    q_comp = (jnp.where(comp_row < HEAD_DIM, qt, zero), jnp.where(comp_row >= HEAD_DIM, qt, zero))

    dk = lax.broadcasted_iota(jnp.int32, (tk, tq), 0)
    dq = lax.broadcasted_iota(jnp.int32, (tk, tq), 1)
    rel_bias = (dk - dq).astype(F32) * slope

    m_ref[...] = jnp.full(m_ref.shape, -jnp.inf, F32)
    l_ref[...] = jnp.zeros(l_ref.shape, F32)
    acc_ref[...] = jnp.zeros(acc_ref.shape, F32)

    def tile(j, masked):
        off = pl.multiple_of(j * tk, tk)
        kt = k_ref[0, pl.ds(off, tk), :]
        vt = vt_ref[0, :, pl.ds(off, tk)]
        tile_bias = -slope * ((qi - j) * tk).astype(F32)
        for c in range(2):
            sc = jnp.dot(kt, q_comp[c], preferred_element_type=F32) + rel_bias
            if masked:
                sc = jnp.where(dk > dq, -jnp.inf, sc)
            m_old = m_ref[c]
            m_new = jnp.maximum(m_old, jnp.max(sc, axis=0, keepdims=True) + tile_bias)
            alpha = jnp.exp(m_old - m_new)
            p = jnp.exp(sc - (m_new - tile_bias))
            l_ref[c] = alpha * l_ref[c] + jnp.sum(p, axis=0, keepdims=True)
            acc_ref[c] = alpha * acc_ref[c] + jnp.dot(vt, p.astype(BF16), preferred_element_type=F32)
            m_ref[c] = m_new

    def body(j, carry):
        tile(j, False)
        return carry

    lax.fori_loop(0, qi, body, 0)
    tile(qi, True)

    lv = lam_ref[0]
    lam = (jnp.exp(jnp.sum(lv[0:1] * lv[1:2], keepdims=True))
           - jnp.exp(jnp.sum(lv[2:3] * lv[3:4], keepdims=True)) + lam_init)
    o = acc_ref[0] / l_ref[0] - lam * (acc_ref[1] / l_ref[1])
    o = o * lax.rsqrt(jnp.mean(o * o, axis=0, keepdims=True) + SUBLN_EPS) * sg_ref[...]
    ot_ref[0] = (o * (1.0 - lam_init)).astype(BF16)


def _diff_attention(qt, k, vt, lam_vecs, subln_g, slopes, lam_init, *, tq=256):
    B, D, S = qt.shape
    kern = functools.partial(_attn_kernel, tq=tq, lam_init=lam_init)
    return pl.pallas_call(
        kern,
        out_shape=jax.ShapeDtypeStruct((B, D, S), BF16),
        grid=(B, N_HEADS, S // tq),
        in_specs=[
            pl.BlockSpec(memory_space=pltpu.SMEM),
            pl.BlockSpec((1, V_DIM, tq), lambda b, h, i: (b, h, i)),
            pl.BlockSpec((1, S, V_DIM), lambda b, h, i: (b, 0, h)),
            pl.BlockSpec((1, V_DIM, S), lambda b, h, i: (b, h, 0)),
            _const_spec((1, 4, HEAD_DIM)),
            _const_spec((V_DIM, 1)),
        ],
        out_specs=pl.BlockSpec((1, V_DIM, tq), lambda b, h, i: (b, h, i)),
        scratch_shapes=[
            pltpu.VMEM((2, 1, tq), F32),
            pltpu.VMEM((2, 1, tq), F32),
            pltpu.VMEM((2, V_DIM, tq), F32),
        ],
        compiler_params=pltpu.CompilerParams(
            dimension_semantics=("parallel", "parallel", "parallel"), vmem_limit_bytes=VMEM_LIMIT),
        name="diff_attention",
    )(slopes, qt, k, vt, lam_vecs.reshape(1, 4, HEAD_DIM), subln_g.reshape(V_DIM, 1))


def _out_kernel(at_ref, w_ref, x_ref, o_ref):
    y = lax.dot_general(at_ref[0], w_ref[...], (((0,), (0,)), ((), ())), preferred_element_type=F32)
    o_ref[0] = x_ref[0] + y


def _out_proj(at, w_o, x, *, tm=512):
    B, D, S = at.shape
    return pl.pallas_call(
        _out_kernel,
        out_shape=jax.ShapeDtypeStruct((B, S, D), F32),
        grid=(B, S // tm),
        in_specs=[
            pl.BlockSpec((1, D, tm), lambda b, s: (b, 0, s)),
            _const_spec((D, D)),
            pl.BlockSpec((1, tm, D), lambda b, s: (b, s, 0)),
        ],
        out_specs=pl.BlockSpec((1, tm, D), lambda b, s: (b, s, 0)),
        compiler_params=pltpu.CompilerParams(
            dimension_semantics=("parallel", "parallel"), vmem_limit_bytes=VMEM_LIMIT),
        name="attn_out_proj",
    )(at, w_o.astype(BF16), x)


def kernel(x, a_norm, a_w_in, a_b_in, a_conv_w, a_conv_b, a_gate_w, a_gate_b, a_lambda, a_w_out,
           kv_norm, w_kv, b_norm, b_w_q, b_lam, b_subln, b_w_o,
           mlp_norm, mlp_w1, mlp_w2, final_norm):
    B, S, D = x.shape
    depth = mlp_w1.shape[0]
    n_a = a_w_in.shape[0]
    slopes = jnp.asarray([2.0 ** (-8.0 * (h + 1) / N_HEADS) for h in range(N_HEADS)], F32)
    k = vt = None
    for l in range(depth):
        if l < n_a:
            x = _recurrent_layer(x, a_norm[l], a_w_in[l], a_b_in[l], a_conv_w[l], a_conv_b[l],
                                 a_gate_w[l], a_gate_b[l], a_lambda[l], a_w_out[l])
        else:
            j = l - n_a
            lam_init = 0.8 - 0.6 * math.exp(-0.3 * l)
            qt = _q_proj(x, b_norm[j], b_w_q[j])
            at = _diff_attention(qt, k, vt, b_lam[j], b_subln[j], slopes, lam_init)
            x = _out_proj(at, b_w_o[j], x)
        x = _mlp_layer(x.reshape(B * S, D), mlp_norm[l], mlp_w1[l], mlp_w2[l], final_norm,
                       final_norm=(l == depth - 1)).reshape(B, S, D)
        if l == n_a - 1:
            k, vt = _kv_proj(x, kv_norm, w_kv)
    return x
```

```python
import functools
import math

import jax
import jax.numpy as jnp
from jax import lax
from jax.experimental import pallas as pl
from jax.experimental.pallas import tpu as pltpu

D_MODEL = 1024
N_HEADS = 8
HEAD_DIM = 64
V_DIM = 128
LRU_BLOCKS = 8
LRU_BLOCK_W = 128
CONV_W = 4
LRU_C = 8.0
NORM_EPS = 1e-6
SUBLN_EPS = 1e-5

SUBLANES = 8
ONES_ROWS = 16
SCORE_LOOKAHEAD = 3
VMEM_LIMIT = 56 * 1024 * 1024

BF16 = jnp.bfloat16
F32 = jnp.float32


def _rmsnorm(x, g, eps):
    return x * lax.rsqrt(jnp.mean(x * x, axis=-1, keepdims=True) + eps) * g


def _const_spec(shape):
    nd = len(shape)
    return pl.BlockSpec(shape, lambda *_: (0,) * nd)


def _rec_kernel(x_ref, g_ref, wx_ref, wy_ref, bx_ref, by_ref, cw_ref, cb_ref, gw_ref, gb_ref,
                lam_ref, wo_ref, o_ref, xpad_ref, a_ref, b_ref, h_ref, *, ts):
    s = pl.program_id(1)
    W = D_MODEL

    @pl.when(s == 0)
    def _():
        xpad_ref[0:SUBLANES, :] = jnp.zeros((SUBLANES, W), F32)
        h_ref[...] = jnp.zeros_like(h_ref)

    x = x_ref[0]
    hn = _rmsnorm(x, g_ref[...], NORM_EPS).astype(BF16)
    xb = jnp.dot(hn, wx_ref[...], preferred_element_type=F32) + bx_ref[...]
    yb = jnp.dot(hn, wy_ref[...], preferred_element_type=F32) + by_ref[...]
    gate = jax.nn.gelu(yb, approximate=True)

    xpad_ref[SUBLANES:SUBLANES + ts, :] = xb
    xc = xb * cw_ref[3:4, :] + cb_ref[...]
    for j in range(CONV_W - 1):
        shift = CONV_W - 1 - j
        xc = xc + xpad_ref[SUBLANES - shift:SUBLANES - shift + ts, :] * cw_ref[j:j + 1, :]
    xpad_ref[0:SUBLANES, :] = xpad_ref[ts:ts + SUBLANES, :]

    xcb = xc.astype(BF16)
    r_parts, i_parts = [], []
    for n in range(LRU_BLOCKS):
        gn = jnp.dot(xcb[:, n * LRU_BLOCK_W:(n + 1) * LRU_BLOCK_W], gw_ref[n],
                     preferred_element_type=F32)
        r_parts.append(gn[:, :LRU_BLOCK_W])
        i_parts.append(gn[:, LRU_BLOCK_W:])
    r = jax.nn.sigmoid(jnp.concatenate(r_parts, axis=1) + gb_ref[0:1, :])
    ig = jax.nn.sigmoid(jnp.concatenate(i_parts, axis=1) + gb_ref[1:2, :])

    z = -lam_ref[...]
    softplus = jnp.maximum(z, 0.0) + jnp.log1p(jnp.exp(-jnp.abs(z)))
    log_a = (-LRU_C * softplus) * r
    a = jnp.exp(log_a)
    bx = jnp.sqrt(-jnp.tanh(log_a) * (a * a + 1.0)) * (ig * xc)

    G = ts // SUBLANES
    a3 = a.reshape(G, SUBLANES, W)
    b3 = bx.reshape(G, SUBLANES, W)
    row = lax.broadcasted_iota(jnp.int32, (G, SUBLANES, W), 1)
    for d in (1, 2, 4):
        a_sh = pltpu.roll(a3, d, axis=1)
        b_sh = pltpu.roll(b3, d, axis=1)
        keep = row >= d
        b3 = jnp.where(keep, a3 * b_sh + b3, b3)
        a3 = jnp.where(keep, a3 * a_sh, a3)
    a_ref[...] = a3.reshape(ts, W)
    b_ref[...] = b3.reshape(ts, W)

    def chain(gidx, h):
        off = pl.multiple_of(gidx * SUBLANES, SUBLANES)
        hs = a_ref[pl.ds(off, SUBLANES), :] * h + b_ref[pl.ds(off, SUBLANES), :]
        b_ref[pl.ds(off, SUBLANES), :] = hs
        return hs[SUBLANES - 1:SUBLANES, :]

    h_last = lax.fori_loop(0, G, chain, h_ref[0:1, :], unroll=8)
    h_ref[0:1, :] = h_last

    y = (b_ref[...] * gate).astype(BF16)
    o_ref[0] = x + jnp.dot(y, wo_ref[...], preferred_element_type=F32)


def _recurrent_layer(x, g, w_in, b_in, conv_w, conv_b, gate_w, gate_b, lam, w_out, *, ts=256):
    B, S, D = x.shape
    W = D_MODEL
    wx = w_in[:, :W].astype(BF16)
    wy = w_in[:, W:].astype(BF16)
    bxv = b_in[:W].reshape(1, W)
    byv = b_in[W:].reshape(1, W)
    gw = jnp.concatenate([gate_w[0], gate_w[1]], axis=-1).astype(BF16)
    kern = functools.partial(_rec_kernel, ts=ts)
    return pl.pallas_call(
        kern,
        out_shape=jax.ShapeDtypeStruct((B, S, D), F32),
        grid=(B, S // ts),
        in_specs=[
            pl.BlockSpec((1, ts, D), lambda b, s: (b, s, 0)),
            _const_spec((1, D)),
            _const_spec((D, W)), _const_spec((D, W)),
            _const_spec((1, W)), _const_spec((1, W)),
            _const_spec((CONV_W, W)), _const_spec((1, W)),
            _const_spec((LRU_BLOCKS, LRU_BLOCK_W, 2 * LRU_BLOCK_W)),
            _const_spec((2, W)),
            _const_spec((1, W)),
            _const_spec((W, D)),
        ],
        out_specs=pl.BlockSpec((1, ts, D), lambda b, s: (b, s, 0)),
        scratch_shapes=[
            pltpu.VMEM((ts + SUBLANES, W), F32),
            pltpu.VMEM((ts, W), F32),
            pltpu.VMEM((ts, W), F32),
            pltpu.VMEM((SUBLANES, W), F32),
        ],
        compiler_params=pltpu.CompilerParams(
            dimension_semantics=("parallel", "arbitrary"), vmem_limit_bytes=VMEM_LIMIT),
        name="rglru_layer",
    )(x, g.reshape(1, D), wx, wy, bxv, byv, conv_w, conv_b.reshape(1, W), gw, gate_b,
      lam.reshape(1, W), w_out.astype(BF16))


def _mlp_kernel(x_ref, g_ref, w1_ref, w2_ref, fg_ref, o_ref, *, final_norm):
    x = x_ref[...]
    hn = _rmsnorm(x, g_ref[...], NORM_EPS).astype(BF16)
    u = jnp.maximum(jnp.dot(hn, w1_ref[...], preferred_element_type=F32), 0.0)
    u = (u * u).astype(BF16)
    y = x + jnp.dot(u, w2_ref[...], preferred_element_type=F32)
    if final_norm:
        y = _rmsnorm(y, fg_ref[...], NORM_EPS)
    o_ref[...] = y


def _mlp_layer(x2, g, w1, w2, final_g, *, final_norm, tm=256):
    M, D = x2.shape
    F = w1.shape[1]
    kern = functools.partial(_mlp_kernel, final_norm=final_norm)
    return pl.pallas_call(
        kern,
        out_shape=jax.ShapeDtypeStruct((M, D), F32),
        grid=(M // tm,),
        in_specs=[
            pl.BlockSpec((tm, D), lambda i: (i, 0)),
            _const_spec((1, D)),
            _const_spec((D, F)),
            _const_spec((F, D)),
            _const_spec((1, D)),
        ],
        out_specs=pl.BlockSpec((tm, D), lambda i: (i, 0)),
        compiler_params=pltpu.CompilerParams(
            dimension_semantics=("parallel",), vmem_limit_bytes=VMEM_LIMIT),
        name="mlp_layer",
    )(x2, g.reshape(1, D), w1.astype(BF16), w2.astype(BF16), final_g.reshape(1, D))


def _kv_kernel(x_ref, g_ref, wk_ref, wv_ref, k_ref, vt_ref):
    hn = _rmsnorm(x_ref[0], g_ref[...], NORM_EPS).astype(BF16)
    k_ref[0] = jnp.dot(hn, wk_ref[...], preferred_element_type=F32).astype(BF16)
    v = jnp.dot(hn, wv_ref[...], preferred_element_type=F32)
    vt_ref[0] = v.T.astype(BF16)


def _kv_proj(x, g, w_kv, *, tm=512):
    B, S, D = x.shape
    return pl.pallas_call(
        _kv_kernel,
        out_shape=(jax.ShapeDtypeStruct((B, S, D), BF16), jax.ShapeDtypeStruct((B, D, S), BF16)),
        grid=(B, S // tm),
        in_specs=[
            pl.BlockSpec((1, tm, D), lambda b, s: (b, s, 0)),
            _const_spec((1, D)), _const_spec((D, D)), _const_spec((D, D)),
        ],
        out_specs=(pl.BlockSpec((1, tm, D), lambda b, s: (b, s, 0)),
                   pl.BlockSpec((1, D, tm), lambda b, s: (b, 0, s))),
        compiler_params=pltpu.CompilerParams(
            dimension_semantics=("parallel", "parallel"), vmem_limit_bytes=VMEM_LIMIT),
        name="kv_proj",
    )(x, g.reshape(1, D), w_kv[:, :D].astype(BF16), w_kv[:, D:].astype(BF16))


def _q_kernel(x_ref, g_ref, wq_ref, qt_ref, *, scale):
    hn = _rmsnorm(x_ref[0], g_ref[...], NORM_EPS).astype(BF16)
    q = jnp.dot(hn, wq_ref[...], preferred_element_type=F32) * scale
    qt_ref[0] = q.T.astype(BF16)


def _q_proj(x, g, w_q, *, tm=512):
    B, S, D = x.shape
    kern = functools.partial(_q_kernel, scale=HEAD_DIM ** -0.5)
    return pl.pallas_call(
        kern,
        out_shape=jax.ShapeDtypeStruct((B, D, S), BF16),
        grid=(B, S // tm),
        in_specs=[
            pl.BlockSpec((1, tm, D), lambda b, s: (b, s, 0)),
            _const_spec((1, D)), _const_spec((D, D)),
        ],
        out_specs=pl.BlockSpec((1, D, tm), lambda b, s: (b, 0, s)),
        compiler_params=pltpu.CompilerParams(
            dimension_semantics=("parallel", "parallel"), vmem_limit_bytes=VMEM_LIMIT),
        name="q_proj",
    )(x, g.reshape(1, D), w_q.astype(BF16))


def _attn_kernel(slopes_ref, qt_ref, k_ref, vt_ref, lam_ref, sg_ref, ot_ref,
                 qaug_ref, kaug_ref, m_ref, acc_ref, *, tq, lam_init):
    qi = pl.program_id(1)
    tk = tq

    row = lax.broadcasted_iota(jnp.int32, (V_DIM, tq), 0)
    dq_row = lax.broadcasted_iota(jnp.int32, (V_DIM, tq), 1).astype(F32)
    q_aug = jnp.where(row == 0, 1.0, jnp.where(row == 1, dq_row, 0.0)).astype(BF16)
    col = lax.broadcasted_iota(jnp.int32, (tk, V_DIM), 1)
    dk_col = lax.broadcasted_iota(jnp.int32, (tk, V_DIM), 0).astype(F32)
    for h in range(N_HEADS):
        qt = qt_ref[0, h * V_DIM:(h + 1) * V_DIM, :]
        zero = jnp.zeros_like(qt)
        qaug_ref[2 * h, 0:V_DIM, :] = jnp.where(row < HEAD_DIM, qt, zero)
        qaug_ref[2 * h + 1, 0:V_DIM, :] = jnp.where(row >= HEAD_DIM, qt, zero)
        qaug_ref[2 * h, V_DIM:, :] = q_aug
        qaug_ref[2 * h + 1, V_DIM:, :] = q_aug
        slope = slopes_ref[h]
        kaug_ref[h] = jnp.where(col == 0, dk_col * slope,
                                jnp.where(col == 1, -slope, 0.0)).astype(BF16)

    m_ref[...] = jnp.full(m_ref.shape, -jnp.inf, F32)
    acc_ref[...] = jnp.zeros(acc_ref.shape, F32)
    ones_rows = jnp.ones((ONES_ROWS, tk), BF16)

    def tile(j, masked):
        off = pl.multiple_of(j * tk, tk)
        tile_dist = ((qi - j) * tk).astype(F32)
        if masked:
            future = (lax.broadcasted_iota(jnp.int32, (tk, tq), 0)
                      > lax.broadcasted_iota(jnp.int32, (tk, tq), 1))
        def scores(h):
            kt = k_ref[0, pl.ds(off, tk), h * V_DIM:(h + 1) * V_DIM]
            lhs = jnp.concatenate([kt, kaug_ref[h]], axis=1)
            return [jnp.dot(lhs, qaug_ref[2 * h + c], preferred_element_type=F32) for c in range(2)]

        def softmax_pv(h, scs):
            vt = vt_ref[0, h * V_DIM:(h + 1) * V_DIM, pl.ds(off, tk)]
            vta = jnp.concatenate([vt, ones_rows], axis=0)
            tile_bias = -slopes_ref[h] * tile_dist
            for c in range(2):
                idx = 2 * h + c
                sc = scs[c]
                if masked:
                    sc = jnp.where(future, -jnp.inf, sc)
                m_old = m_ref[idx]
                m_new = jnp.maximum(m_old, jnp.max(sc, axis=0, keepdims=True) + tile_bias)
                alpha = jnp.exp(m_old - m_new)
                p = jnp.exp(sc - (m_new - tile_bias)).astype(BF16)
                acc_ref[idx] = alpha * acc_ref[idx] + jnp.dot(vta, p, preferred_element_type=F32)
                m_ref[idx] = m_new

        pending = {h: scores(h) for h in range(SCORE_LOOKAHEAD)}
        for h in range(N_HEADS):
            if h + SCORE_LOOKAHEAD < N_HEADS:
                pending[h + SCORE_LOOKAHEAD] = scores(h + SCORE_LOOKAHEAD)
            softmax_pv(h, pending.pop(h))

    def body(j, carry):
        tile(j, False)
        return carry

    lax.fori_loop(0, qi, body, 0)
    tile(qi, True)

    lv = lam_ref[0]
    lam = (jnp.exp(jnp.sum(lv[0:1] * lv[1:2], keepdims=True))
           - jnp.exp(jnp.sum(lv[2:3] * lv[3:4], keepdims=True)) + lam_init)
    for h in range(N_HEADS):
        a1 = acc_ref[2 * h]
        a2 = acc_ref[2 * h + 1]
        o = (a1[0:V_DIM] / a1[V_DIM:V_DIM + 1]) - lam * (a2[0:V_DIM] / a2[V_DIM:V_DIM + 1])
        o = o * lax.rsqrt(jnp.mean(o * o, axis=0, keepdims=True) + SUBLN_EPS) * sg_ref[...]
        ot_ref[0, h * V_DIM:(h + 1) * V_DIM, :] = (o * (1.0 - lam_init)).astype(BF16)


def _diff_attention(qt, k, vt, lam_vecs, subln_g, slopes, lam_init, *, tq=256):
    B, D, S = qt.shape
    kern = functools.partial(_attn_kernel, tq=tq, lam_init=lam_init)
    return pl.pallas_call(
        kern,
        out_shape=jax.ShapeDtypeStruct((B, D, S), BF16),
        grid=(B, S // tq),
        in_specs=[
            pl.BlockSpec(memory_space=pltpu.SMEM),
            pl.BlockSpec((1, D, tq), lambda b, i: (b, 0, i)),
            pl.BlockSpec((1, S, D), lambda b, i: (b, 0, 0)),
            pl.BlockSpec((1, D, S), lambda b, i: (b, 0, 0)),
            _const_spec((1, 4, HEAD_DIM)),
            _const_spec((V_DIM, 1)),
        ],
        out_specs=pl.BlockSpec((1, D, tq), lambda b, i: (b, 0, i)),
        scratch_shapes=[
            pltpu.VMEM((2 * N_HEADS, 2 * V_DIM, tq), BF16),
            pltpu.VMEM((N_HEADS, tq, V_DIM), BF16),
            pltpu.VMEM((2 * N_HEADS, 1, tq), F32),
            pltpu.VMEM((2 * N_HEADS, V_DIM + ONES_ROWS, tq), F32),
        ],
        compiler_params=pltpu.CompilerParams(
            dimension_semantics=("parallel", "parallel"), vmem_limit_bytes=VMEM_LIMIT),
        name="diff_attention",
    )(slopes, qt, k, vt, lam_vecs.reshape(1, 4, HEAD_DIM), subln_g.reshape(V_DIM, 1))


def _out_kernel(at_ref, w_ref, x_ref, o_ref):
    y = lax.dot_general(at_ref[0], w_ref[...], (((0,), (0,)), ((), ())), preferred_element_type=F32)
    o_ref[0] = x_ref[0] + y


def _out_proj(at, w_o, x, *, tm=512):
    B, D, S = at.shape
    return pl.pallas_call(
        _out_kernel,
        out_shape=jax.ShapeDtypeStruct((B, S, D), F32),
        grid=(B, S // tm),
        in_specs=[
            pl.BlockSpec((1, D, tm), lambda b, s: (b, 0, s)),
            _const_spec((D, D)),
            pl.BlockSpec((1, tm, D), lambda b, s: (b, s, 0)),
        ],
        out_specs=pl.BlockSpec((1, tm, D), lambda b, s: (b, s, 0)),
        compiler_params=pltpu.CompilerParams(
            dimension_semantics=("parallel", "parallel"), vmem_limit_bytes=VMEM_LIMIT),
        name="attn_out_proj",
    )(at, w_o.astype(BF16), x)


def _attention_layer(x, k_v, g, w_q, lam_vecs, subln_g, w_o, slopes, lam_init):
    k, vt = k_v
    qt = _q_proj(x, g, w_q)
    at = _diff_attention(qt, k, vt, lam_vecs, subln_g, slopes, lam_init)
    return _out_proj(at, w_o, x)


def kernel(x, a_norm, a_w_in, a_b_in, a_conv_w, a_conv_b, a_gate_w, a_gate_b, a_lambda, a_w_out,
           kv_norm, w_kv, b_norm, b_w_q, b_lam, b_subln, b_w_o,
           mlp_norm, mlp_w1, mlp_w2, final_norm):
    B, S, D = x.shape
    depth = mlp_w1.shape[0]
    n_a = a_w_in.shape[0]
    slopes = jnp.asarray([2.0 ** (-8.0 * (h + 1) / N_HEADS) for h in range(N_HEADS)], F32)
    k_v = None
    for l in range(depth):
        if l < n_a:
            x = _recurrent_layer(x, a_norm[l], a_w_in[l], a_b_in[l], a_conv_w[l], a_conv_b[l],
                                 a_gate_w[l], a_gate_b[l], a_lambda[l], a_w_out[l])
        else:
            j = l - n_a
            lam_init = 0.8 - 0.6 * math.exp(-0.3 * l)
            x = _attention_layer(x, k_v, b_norm[j], b_w_q[j], b_lam[j], b_subln[j], b_w_o[j],
                                 slopes, lam_init)
        x = _mlp_layer(x.reshape(B * S, D), mlp_norm[l], mlp_w1[l], mlp_w2[l], final_norm,
                       final_norm=(l == depth - 1)).reshape(B, S, D)
        if l == n_a - 1:
            k_v = _kv_proj(x, kv_norm, w_kv)
    return x
```

```python
import functools
import math

import jax
import jax.numpy as jnp
import numpy as np
from jax import lax
from jax.experimental import pallas as pl
from jax.experimental.pallas import tpu as pltpu

D_MODEL = 1024
N_HEADS = 8
HEAD_DIM = 64
V_DIM = 128
LRU_BLOCKS = 8
LRU_BLOCK_W = 128
CONV_W = 4
LRU_C = 8.0
NORM_EPS = 1e-6
SUBLN_EPS = 1e-5

SUBLANES = 8
ONES_ROWS = 16
SCORE_LOOKAHEAD = 4
LOG2E = math.log2(math.e)
VMEM_LIMIT = 56 * 1024 * 1024

BF16 = jnp.bfloat16
F32 = jnp.float32


def _rmsnorm(x, g, eps):
    return x * lax.rsqrt(jnp.mean(x * x, axis=-1, keepdims=True) + eps) * g


def _const_spec(shape):
    nd = len(shape)
    return pl.BlockSpec(shape, lambda *_: (0,) * nd)


def _gelu_tanh(x):
    c1 = math.sqrt(2.0 / math.pi)
    return x * (0.5 + 0.5 * jnp.tanh(x * (c1 + (c1 * 0.044715) * (x * x))))


def _rec_kernel(x_ref, g_ref, wx_ref, wy_ref, bx_ref, by_ref, cw_ref, cb_ref, gw_ref, gb_ref,
                lam_ref, wo_ref, perm_ref, unperm_ref, o_ref, xpad_ref, tail_ref, h_ref, *, ts):
    s_idx = pl.program_id(1)
    W = D_MODEL
    G = ts // SUBLANES
    HALO = (CONV_W - 1) * SUBLANES

    @pl.when(s_idx == 0)
    def _():
        tail_ref[...] = jnp.zeros_like(tail_ref)
        h_ref[...] = jnp.zeros_like(h_ref)

    x = x_ref[0]
    hn = _rmsnorm(x, g_ref[...], NORM_EPS).astype(BF16)
    hn = jnp.dot(perm_ref[...], hn, preferred_element_type=F32).astype(BF16)
    xb = jnp.dot(hn, wx_ref[...], preferred_element_type=F32) + bx_ref[...]
    yb = jnp.dot(hn, wy_ref[...], preferred_element_type=F32) + by_ref[...]
    gate = _gelu_tanh(yb)

    row8 = lax.broadcasted_iota(jnp.int32, (SUBLANES, W), 0)
    for k in range(CONV_W - 1):
        cur = xb[ts - HALO + k * SUBLANES:ts - HALO + (k + 1) * SUBLANES, :]
        prev = tail_ref[k * SUBLANES:(k + 1) * SUBLANES, :]
        xpad_ref[k * SUBLANES:(k + 1) * SUBLANES, :] = jnp.where(
            row8 == 0, pltpu.roll(prev, 1, axis=0), pltpu.roll(cur, 1, axis=0))
    tail_ref[...] = xb[ts - HALO:, :]
    xpad_ref[HALO:, :] = xb
    xc = xb * cw_ref[CONV_W - 1:CONV_W, :] + cb_ref[...]
    for j in range(CONV_W - 1):
        xc = xc + xpad_ref[j * SUBLANES:j * SUBLANES + ts, :] * cw_ref[j:j + 1, :]

    xcb = xc.astype(BF16)
    r_parts, i_parts = [], []
    for n in range(LRU_BLOCKS):
        gn = jnp.dot(xcb[:, n * LRU_BLOCK_W:(n + 1) * LRU_BLOCK_W], gw_ref[n],
                     preferred_element_type=F32)
        r_parts.append(gn[:, :LRU_BLOCK_W])
        i_parts.append(gn[:, LRU_BLOCK_W:])
    r = jax.nn.sigmoid(jnp.concatenate(r_parts, axis=1) + gb_ref[0:1, :])
    ig = jax.nn.sigmoid(jnp.concatenate(i_parts, axis=1) + gb_ref[1:2, :])

    z = -lam_ref[...]
    softplus = jnp.maximum(z, 0.0) + jnp.log1p(jnp.exp(-jnp.abs(z)))
    log_a = (-LRU_C * softplus) * r
    a = jnp.exp(log_a)
    bx = jnp.sqrt(-jnp.tanh(log_a) * (a * a + 1.0)) * (ig * xc)

    h_loc = jnp.zeros((SUBLANES, W), F32)
    prod = jnp.ones((SUBLANES, W), F32)
    h_steps, p_steps = [], []
    for g in range(G):
        a_g = a[g * SUBLANES:(g + 1) * SUBLANES, :]
        h_loc = a_g * h_loc + bx[g * SUBLANES:(g + 1) * SUBLANES, :]
        prod = a_g * prod
        h_steps.append(h_loc)
        p_steps.append(prod)

    t_cum, f_cum = prod, h_loc
    for d in (1, 2, 4):
        keep = row8 >= d
        f_cum = jnp.where(keep, t_cum * pltpu.roll(f_cum, d, axis=0) + f_cum, f_cum)
        t_cum = jnp.where(keep, t_cum * pltpu.roll(t_cum, d, axis=0), t_cum)
    h_prev = h_ref[SUBLANES - 1:SUBLANES, :]
    h_end = t_cum * h_prev + f_cum
    h_in = jnp.where(row8 == 0, h_prev, pltpu.roll(h_end, 1, axis=0))
    h_ref[...] = h_end
    hs = jnp.concatenate([h_steps[g] + p_steps[g] * h_in for g in range(G)], axis=0)

    y = (hs * gate).astype(BF16)
    y = jnp.dot(unperm_ref[...], y, preferred_element_type=F32).astype(BF16)
    o_ref[0] = x + jnp.dot(y, wo_ref[...], preferred_element_type=F32)


def _chunk_permutation(ts):
    steps = ts // SUBLANES
    p = np.zeros((ts, ts), np.float32)
    for c in range(SUBLANES):
        for s in range(steps):
            p[s * SUBLANES + c, c * steps + s] = 1.0
    return p


def _recurrent_layer(x, g, w_in, b_in, conv_w, conv_b, gate_w, gate_b, lam, w_out, *, ts=256):
    B, S, D = x.shape
    W = D_MODEL
    wx = w_in[:, :W].astype(BF16)
    wy = w_in[:, W:].astype(BF16)
    bxv = b_in[:W].reshape(1, W)
    byv = b_in[W:].reshape(1, W)
    gw = jnp.concatenate([gate_w[0], gate_w[1]], axis=-1).astype(BF16)
    perm = _chunk_permutation(ts)
    kern = functools.partial(_rec_kernel, ts=ts)
    return pl.pallas_call(
        kern,
        out_shape=jax.ShapeDtypeStruct((B, S, D), F32),
        grid=(B, S // ts),
        in_specs=[
            pl.BlockSpec((1, ts, D), lambda b, s: (b, s, 0)),
            _const_spec((1, D)),
            _const_spec((D, W)), _const_spec((D, W)),
            _const_spec((1, W)), _const_spec((1, W)),
            _const_spec((CONV_W, W)), _const_spec((1, W)),
            _const_spec((LRU_BLOCKS, LRU_BLOCK_W, 2 * LRU_BLOCK_W)),
            _const_spec((2, W)),
            _const_spec((1, W)),
            _const_spec((W, D)),
            _const_spec((ts, ts)), _const_spec((ts, ts)),
        ],
        out_specs=pl.BlockSpec((1, ts, D), lambda b, s: (b, s, 0)),
        scratch_shapes=[
            pltpu.VMEM((ts + (CONV_W - 1) * SUBLANES, W), F32),
            pltpu.VMEM(((CONV_W - 1) * SUBLANES, W), F32),
            pltpu.VMEM((SUBLANES, W), F32),
        ],
        compiler_params=pltpu.CompilerParams(
            dimension_semantics=("parallel", "arbitrary"), vmem_limit_bytes=VMEM_LIMIT),
        name="rglru_layer",
    )(x, g.reshape(1, D), wx, wy, bxv, byv, conv_w, conv_b.reshape(1, W), gw, gate_b,
      lam.reshape(1, W), w_out.astype(BF16), jnp.asarray(perm, BF16), jnp.asarray(perm.T, BF16))


def _mlp_kernel(x_ref, g_ref, w1_ref, w2_ref, fg_ref, o_ref, *, final_norm):
    x = x_ref[...]
    hn = _rmsnorm(x, g_ref[...], NORM_EPS).astype(BF16)
    u = jnp.maximum(jnp.dot(hn, w1_ref[...], preferred_element_type=F32), 0.0)
    u = (u * u).astype(BF16)
    y = x + jnp.dot(u, w2_ref[...], preferred_element_type=F32)
    if final_norm:
        y = _rmsnorm(y, fg_ref[...], NORM_EPS)
    o_ref[...] = y


def _mlp_layer(x2, g, w1, w2, final_g, *, final_norm, tm=256):
    M, D = x2.shape
    F = w1.shape[1]
    kern = functools.partial(_mlp_kernel, final_norm=final_norm)
    return pl.pallas_call(
        kern,
        out_shape=jax.ShapeDtypeStruct((M, D), F32),
        grid=(M // tm,),
        in_specs=[
            pl.BlockSpec((tm, D), lambda i: (i, 0)),
            _const_spec((1, D)),
            _const_spec((D, F)),
            _const_spec((F, D)),
            _const_spec((1, D)),
        ],
        out_specs=pl.BlockSpec((tm, D), lambda i: (i, 0)),
        compiler_params=pltpu.CompilerParams(
            dimension_semantics=("parallel",), vmem_limit_bytes=VMEM_LIMIT),
        name="mlp_layer",
    )(x2, g.reshape(1, D), w1.astype(BF16), w2.astype(BF16), final_g.reshape(1, D))


def _kv_kernel(x_ref, g_ref, wk_ref, wv_ref, k_ref, vt_ref):
    hn = _rmsnorm(x_ref[0], g_ref[...], NORM_EPS).astype(BF16)
    k_ref[0] = jnp.dot(hn, wk_ref[...], preferred_element_type=F32).astype(BF16)
    v = jnp.dot(hn, wv_ref[...], preferred_element_type=F32)
    vt_ref[0] = v.T.astype(BF16)


def _kv_proj(x, g, w_kv, *, tm=512):
    B, S, D = x.shape
    return pl.pallas_call(
        _kv_kernel,
        out_shape=(jax.ShapeDtypeStruct((B, S, D), BF16), jax.ShapeDtypeStruct((B, D, S), BF16)),
        grid=(B, S // tm),
        in_specs=[
            pl.BlockSpec((1, tm, D), lambda b, s: (b, s, 0)),
            _const_spec((1, D)), _const_spec((D, D)), _const_spec((D, D)),
        ],
        out_specs=(pl.BlockSpec((1, tm, D), lambda b, s: (b, s, 0)),
                   pl.BlockSpec((1, D, tm), lambda b, s: (b, 0, s))),
        compiler_params=pltpu.CompilerParams(
            dimension_semantics=("parallel", "parallel"), vmem_limit_bytes=VMEM_LIMIT),
        name="kv_proj",
    )(x, g.reshape(1, D), w_kv[:, :D].astype(BF16), w_kv[:, D:].astype(BF16))


def _q_kernel(x_ref, g_ref, wq_ref, qt_ref, *, scale):
    hn = _rmsnorm(x_ref[0], g_ref[...], NORM_EPS).astype(BF16)
    q = jnp.dot(hn, wq_ref[...], preferred_element_type=F32) * scale
    qt_ref[0] = q.T.astype(BF16)


def _q_proj(x, g, w_q, *, tm=512):
    B, S, D = x.shape
    kern = functools.partial(_q_kernel, scale=LOG2E * HEAD_DIM ** -0.5)
    return pl.pallas_call(
        kern,
        out_shape=jax.ShapeDtypeStruct((B, D, S), BF16),
        grid=(B, S // tm),
        in_specs=[
            pl.BlockSpec((1, tm, D), lambda b, s: (b, s, 0)),
            _const_spec((1, D)), _const_spec((D, D)),
        ],
        out_specs=pl.BlockSpec((1, D, tm), lambda b, s: (b, 0, s)),
        compiler_params=pltpu.CompilerParams(
            dimension_semantics=("parallel", "parallel"), vmem_limit_bytes=VMEM_LIMIT),
        name="q_proj",
    )(x, g.reshape(1, D), w_q.astype(BF16))


def _bf16_terms(c, n=3):
    terms, rest = [], np.float32(c)
    for _ in range(n):
        t = np.float32(np.asarray(rest, dtype=BF16))
        terms.append(float(t))
        rest = np.float32(rest - t)
    return terms


def _attn_kernel(qt_ref, k_ref, vt_ref, lam_ref, sg_ref, ot_ref,
                 qaug_ref, kaug_ref, sc_ref, m_ref, acc_ref, *, tq, lam_init, slopes):
    qi = pl.program_id(1)
    tk = tq
    LA = SCORE_LOOKAHEAD
    log2_slopes = [s * LOG2E for s in slopes]

    row = lax.broadcasted_iota(jnp.int32, (V_DIM, tq), 0)
    dq_row = lax.broadcasted_iota(jnp.int32, (V_DIM, tq), 1).astype(F32)
    col = lax.broadcasted_iota(jnp.int32, (tk, V_DIM), 1)
    dk_col = lax.broadcasted_iota(jnp.int32, (tk, V_DIM), 0).astype(F32)
    for h in range(N_HEADS):
        c = _bf16_terms(log2_slopes[h])
        q_aug = jnp.where(row == 0, c[0], jnp.where(row == 1, c[1], jnp.where(
            row == 2, c[2], jnp.where(row < 6, dq_row, 0.0)))).astype(BF16)
        kaug_ref[h] = jnp.where(col < 3, dk_col, jnp.where(col == 3, -c[0], jnp.where(
            col == 4, -c[1], jnp.where(col == 5, -c[2], 0.0)))).astype(BF16)
        qt = qt_ref[0, h * V_DIM:(h + 1) * V_DIM, :]
        zero = jnp.zeros_like(qt)
        qaug_ref[2 * h, 0:V_DIM, :] = jnp.where(row < HEAD_DIM, qt, zero)
        qaug_ref[2 * h + 1, 0:V_DIM, :] = jnp.where(row >= HEAD_DIM, qt, zero)
        qaug_ref[2 * h, V_DIM:, :] = q_aug
        qaug_ref[2 * h + 1, V_DIM:, :] = q_aug

    m_ref[...] = jnp.full(m_ref.shape, -jnp.inf, F32)
    acc_ref[...] = jnp.zeros(acc_ref.shape, F32)
    ones_rows = jnp.ones((ONES_ROWS, tk), BF16)

    def issue_scores(j, h):
        off = pl.multiple_of(j * tk, tk)
        kt = k_ref[0, pl.ds(off, tk), h * V_DIM:(h + 1) * V_DIM]
        lhs = jnp.concatenate([kt, kaug_ref[h]], axis=1)
        for c in range(2):
            sc_ref[h % LA, c] = jnp.dot(lhs, qaug_ref[2 * h + c], preferred_element_type=F32)

    def softmax_pv(j, h, future):
        off = pl.multiple_of(j * tk, tk)
        vt = vt_ref[0, h * V_DIM:(h + 1) * V_DIM, pl.ds(off, tk)]
        vta = jnp.concatenate([vt, ones_rows], axis=0)
        tile_bias = -log2_slopes[h] * ((qi - j) * tk).astype(F32)
        for c in range(2):
            idx = 2 * h + c
            sc = sc_ref[h % LA, c]
            if future is not None:
                sc = jnp.where(future, -jnp.inf, sc)
            m_old = m_ref[idx]
            m_new = jnp.maximum(m_old, jnp.max(sc, axis=0, keepdims=True) + tile_bias)
            alpha = jnp.exp2(m_old - m_new)
            p = jnp.exp2(sc - (m_new - tile_bias)).astype(BF16)
            acc_ref[idx] = alpha * acc_ref[idx] + jnp.dot(vta, p, preferred_element_type=F32)
            m_ref[idx] = m_new

    for h in range(LA):
        issue_scores(0, h)

    def body(j, carry):
        for h in range(N_HEADS):
            softmax_pv(j, h, None)
            if h + LA < N_HEADS:
                issue_scores(j, h + LA)
            else:
                issue_scores(j + 1, h + LA - N_HEADS)
        return carry

    lax.fori_loop(0, qi, body, 0)

    future = (lax.broadcasted_iota(jnp.int32, (tk, tq), 0)
              > lax.broadcasted_iota(jnp.int32, (tk, tq), 1))
    for h in range(N_HEADS):
        softmax_pv(qi, h, future)
        if h + LA < N_HEADS:
            issue_scores(qi, h + LA)

    lv = lam_ref[0]
    lam = (jnp.exp(jnp.sum(lv[0:1] * lv[1:2], keepdims=True))
           - jnp.exp(jnp.sum(lv[2:3] * lv[3:4], keepdims=True)) + lam_init)
    for h in range(N_HEADS):
        a1 = acc_ref[2 * h]
        a2 = acc_ref[2 * h + 1]
        o = (a1[0:V_DIM] / a1[V_DIM:V_DIM + 1]) - lam * (a2[0:V_DIM] / a2[V_DIM:V_DIM + 1])
        o = o * lax.rsqrt(jnp.mean(o * o, axis=0, keepdims=True) + SUBLN_EPS) * sg_ref[...]
        ot_ref[0, h * V_DIM:(h + 1) * V_DIM, :] = (o * (1.0 - lam_init)).astype(BF16)


def _diff_attention(qt, k, vt, lam_vecs, subln_g, slopes, lam_init, *, tq=256):
    B, D, S = qt.shape
    assert N_HEADS % SCORE_LOOKAHEAD == 0 and tq <= 256
    kern = functools.partial(_attn_kernel, tq=tq, lam_init=lam_init, slopes=slopes)
    return pl.pallas_call(
        kern,
        out_shape=jax.ShapeDtypeStruct((B, D, S), BF16),
        grid=(B, S // tq),
        in_specs=[
            pl.BlockSpec((1, D, tq), lambda b, i: (b, 0, i)),
            pl.BlockSpec((1, S, D), lambda b, i: (b, 0, 0)),
            pl.BlockSpec((1, D, S), lambda b, i: (b, 0, 0)),
            _const_spec((1, 4, HEAD_DIM)),
            _const_spec((V_DIM, 1)),
        ],
        out_specs=pl.BlockSpec((1, D, tq), lambda b, i: (b, 0, i)),
        scratch_shapes=[
            pltpu.VMEM((2 * N_HEADS, 2 * V_DIM, tq), BF16),
            pltpu.VMEM((N_HEADS, tq, V_DIM), BF16),
            pltpu.VMEM((SCORE_LOOKAHEAD, 2, tq, tq), F32),
            pltpu.VMEM((2 * N_HEADS, 1, tq), F32),
            pltpu.VMEM((2 * N_HEADS, V_DIM + ONES_ROWS, tq), F32),
        ],
        compiler_params=pltpu.CompilerParams(
            dimension_semantics=("parallel", "parallel"), vmem_limit_bytes=VMEM_LIMIT),
        name="diff_attention",
    )(qt, k, vt, lam_vecs.reshape(1, 4, HEAD_DIM), subln_g.reshape(V_DIM, 1))


def _out_kernel(at_ref, w_ref, x_ref, o_ref):
    y = lax.dot_general(at_ref[0], w_ref[...], (((0,), (0,)), ((), ())), preferred_element_type=F32)
    o_ref[0] = x_ref[0] + y


def _out_proj(at, w_o, x, *, tm=512):
    B, D, S = at.shape
    return pl.pallas_call(
        _out_kernel,
        out_shape=jax.ShapeDtypeStruct((B, S, D), F32),
        grid=(B, S // tm),
        in_specs=[
            pl.BlockSpec((1, D, tm), lambda b, s: (b, 0, s)),
            _const_spec((D, D)),
            pl.BlockSpec((1, tm, D), lambda b, s: (b, s, 0)),
        ],
        out_specs=pl.BlockSpec((1, tm, D), lambda b, s: (b, s, 0)),
        compiler_params=pltpu.CompilerParams(
            dimension_semantics=("parallel", "parallel"), vmem_limit_bytes=VMEM_LIMIT),
        name="attn_out_proj",
    )(at, w_o.astype(BF16), x)


def _attention_layer(x, k_v, g, w_q, lam_vecs, subln_g, w_o, slopes, lam_init):
    k, vt = k_v
    qt = _q_proj(x, g, w_q)
    at = _diff_attention(qt, k, vt, lam_vecs, subln_g, slopes, lam_init)
    return _out_proj(at, w_o, x)


def kernel(x, a_norm, a_w_in, a_b_in, a_conv_w, a_conv_b, a_gate_w, a_gate_b, a_lambda, a_w_out,
           kv_norm, w_kv, b_norm, b_w_q, b_lam, b_subln, b_w_o,
           mlp_norm, mlp_w1, mlp_w2, final_norm):
    B, S, D = x.shape
    depth = mlp_w1.shape[0]
    n_a = a_w_in.shape[0]
    slopes = tuple(2.0 ** (-8.0 * (h + 1) / N_HEADS) for h in range(N_HEADS))
    k_v = None
    for l in range(depth):
        if l < n_a:
            x = _recurrent_layer(x, a_norm[l], a_w_in[l], a_b_in[l], a_conv_w[l], a_conv_b[l],
                                 a_gate_w[l], a_gate_b[l], a_lambda[l], a_w_out[l])
        else:
            j = l - n_a
            lam_init = 0.8 - 0.6 * math.exp(-0.3 * l)
            x = _attention_layer(x, k_v, b_norm[j], b_w_q[j], b_lam[j], b_subln[j], b_w_o[j],
                                 slopes, lam_init)
        x = _mlp_layer(x.reshape(B * S, D), mlp_norm[l], mlp_w1[l], mlp_w2[l], final_norm,
                       final_norm=(l == depth - 1)).reshape(B, S, D)
        if l == n_a - 1:
            k_v = _kv_proj(x, kv_norm, w_kv)
    return x
```

```python
import functools
import math

import jax
import jax.numpy as jnp
import numpy as np
from jax import lax
from jax.experimental import pallas as pl
from jax.experimental.pallas import tpu as pltpu

D_MODEL = 1024
N_HEADS = 8
HEAD_DIM = 64
V_DIM = 128
LRU_BLOCKS = 8
LRU_BLOCK_W = 128
CONV_W = 4
LRU_C = 8.0
NORM_EPS = 1e-6
SUBLN_EPS = 1e-5

SUBLANES = 8
ONES_ROWS = 16
SCORE_LOOKAHEAD = 4
LOG2E = math.log2(math.e)
REC_COLS = 256
VMEM_LIMIT = 56 * 1024 * 1024

BF16 = jnp.bfloat16
F32 = jnp.float32


def _rmsnorm(x, g, eps):
    return x * lax.rsqrt(jnp.mean(x * x, axis=-1, keepdims=True) + eps) * g


def _const_spec(shape):
    nd = len(shape)
    return pl.BlockSpec(shape, lambda *_: (0,) * nd, pipeline_mode=pl.Buffered(1))


def _gelu_tanh(x):
    c1 = math.sqrt(2.0 / math.pi)
    return x * (0.5 + 0.5 * jnp.tanh(x * (c1 + (c1 * 0.044715) * (x * x))))


def _rec_kernel(x_ref, g_ref, wx_ref, wy_ref, bx_ref, by_ref, cw_ref, cb_ref, gw_ref, gb_ref,
                lam_ref, wo_ref, perm_ref, unperm_ref, o_ref,
                xb_s, yb_s, x_s, xpad_ref, tail_ref, h_ref, *, ts, tiles_per_seq):
    f = pl.program_id(0)
    W = D_MODEL
    G = ts // SUBLANES
    HALO = (CONV_W - 1) * SUBLANES
    NB = W // REC_COLS
    starts_sequence = lax.rem(f - 1, tiles_per_seq) == 0

    @pl.when(f == 0)
    def _():
        xb_s[...] = jnp.zeros_like(xb_s)
        yb_s[...] = jnp.zeros_like(yb_s)
        x_s[...] = jnp.zeros_like(x_s)
        tail_ref[...] = jnp.zeros_like(tail_ref)
        h_ref[...] = jnp.zeros_like(h_ref)

    row8 = lax.broadcasted_iota(jnp.int32, (SUBLANES, REC_COLS), 0)

    def normed_input():
        hn = _rmsnorm(x_ref[0], g_ref[...], NORM_EPS).astype(BF16)
        return jnp.dot(perm_ref[...], hn, preferred_element_type=F32).astype(BF16)

    def in_proj(cb, hn):
        cs = slice(cb * REC_COLS, (cb + 1) * REC_COLS)
        xb_s[:, cs] = jnp.dot(hn, wx_ref[:, cs], preferred_element_type=F32) + bx_ref[:, cs]
        yb_s[:, cs] = jnp.dot(hn, wy_ref[:, cs], preferred_element_type=F32) + by_ref[:, cs]

    def conv_gates(cb):
        cs = slice(cb * REC_COLS, (cb + 1) * REC_COLS)
        xb = xb_s[:, cs]
        yb = yb_s[:, cs]
        for k in range(CONV_W - 1):
            cur = xb[ts - HALO + k * SUBLANES:ts - HALO + (k + 1) * SUBLANES, :]
            prev = jnp.where(starts_sequence, 0.0, tail_ref[k * SUBLANES:(k + 1) * SUBLANES, cs])
            xpad_ref[k * SUBLANES:(k + 1) * SUBLANES, cs] = jnp.where(
                row8 == 0, pltpu.roll(prev, 1, axis=0), pltpu.roll(cur, 1, axis=0))
        tail_ref[:, cs] = xb[ts - HALO:, :]
        xpad_ref[HALO:, cs] = xb
        xc = xb * cw_ref[CONV_W - 1:CONV_W, cs] + cb_ref[:, cs]
        for j in range(CONV_W - 1):
            xc = xc + xpad_ref[j * SUBLANES:j * SUBLANES + ts, cs] * cw_ref[j:j + 1, cs]
        xcb = xc.astype(BF16)
        r_parts, i_parts = [], []
        for n in range(REC_COLS // LRU_BLOCK_W):
            gn = jnp.dot(xcb[:, n * LRU_BLOCK_W:(n + 1) * LRU_BLOCK_W],
                         gw_ref[cb * (REC_COLS // LRU_BLOCK_W) + n], preferred_element_type=F32)
            r_parts.append(gn[:, :LRU_BLOCK_W])
            i_parts.append(gn[:, LRU_BLOCK_W:])
        return xc, jnp.concatenate(r_parts, axis=1), jnp.concatenate(i_parts, axis=1), yb

    def recurrence(cb, xc, g_r, g_i, yb):
        cs = slice(cb * REC_COLS, (cb + 1) * REC_COLS)
        r = jax.nn.sigmoid(g_r + gb_ref[0:1, cs])
        ig = jax.nn.sigmoid(g_i + gb_ref[1:2, cs])
        z = -lam_ref[:, cs]
        softplus = jnp.maximum(z, 0.0) + jnp.log1p(jnp.exp(-jnp.abs(z)))
        log_a = (-LRU_C * softplus) * r
        a = jnp.exp(log_a)
        var = -jnp.tanh(log_a) * (a * a + 1.0)
        bx = jnp.where(var > 0.0, var * lax.rsqrt(var), 0.0) * (ig * xc)

        h_loc = jnp.zeros((SUBLANES, REC_COLS), F32)
        prod = jnp.ones((SUBLANES, REC_COLS), F32)
        h_steps, p_steps = [], []
        for g in range(G):
            a_g = a[g * SUBLANES:(g + 1) * SUBLANES, :]
            h_loc = a_g * h_loc + bx[g * SUBLANES:(g + 1) * SUBLANES, :]
            prod = a_g * prod
            h_steps.append(h_loc)
            p_steps.append(prod)

        t_cum, f_cum = prod, h_loc
        for d in (1, 2, 4):
            keep = row8 >= d
            f_cum = jnp.where(keep, t_cum * pltpu.roll(f_cum, d, axis=0) + f_cum, f_cum)
            t_cum = jnp.where(keep, t_cum * pltpu.roll(t_cum, d, axis=0), t_cum)
        h_prev = jnp.where(starts_sequence, 0.0, h_ref[SUBLANES - 1:SUBLANES, cs])
        h_end = t_cum * h_prev + f_cum
        h_in = jnp.where(row8 == 0, h_prev, pltpu.roll(h_end, 1, axis=0))
        h_ref[:, cs] = h_end
        hs = jnp.concatenate([h_steps[g] + p_steps[g] * h_in for g in range(G)], axis=0)
        return (hs * _gelu_tanh(yb)).astype(BF16)

    def out_proj(cb, y):
        y = jnp.dot(unperm_ref[...], y, preferred_element_type=F32).astype(BF16)
        return jnp.dot(y, wo_ref[cb * REC_COLS:(cb + 1) * REC_COLS, :], preferred_element_type=F32)

    hn = normed_input()
    gated = {0: conv_gates(0)}
    out = x_s[...]
    for cb in range(NB):
        in_proj(cb, hn)
        if cb + 1 < NB:
            gated[cb + 1] = conv_gates(cb + 1)
        out = out + out_proj(cb, recurrence(cb, *gated.pop(cb)))
    o_ref[0] = out
    x_s[...] = x_ref[0]


def _chunk_permutation(ts):
    steps = ts // SUBLANES
    p = np.zeros((ts, ts), np.float32)
    for c in range(SUBLANES):
        for s in range(steps):
            p[s * SUBLANES + c, c * steps + s] = 1.0
    return p


def _recurrent_layer(x, g, w_in, b_in, conv_w, conv_b, gate_w, gate_b, lam, w_out, *, ts=256):
    B, S, D = x.shape
    W = D_MODEL
    n_seq = S // ts
    n_tiles = B * n_seq
    wx = w_in[:, :W].astype(BF16)
    wy = w_in[:, W:].astype(BF16)
    bxv = b_in[:W].reshape(1, W)
    byv = b_in[W:].reshape(1, W)
    gw = jnp.concatenate([gate_w[0], gate_w[1]], axis=-1).astype(BF16)
    perm = _chunk_permutation(ts)
    kern = functools.partial(_rec_kernel, ts=ts, tiles_per_seq=n_seq)

    def in_tile(f):
        t = jnp.minimum(f, n_tiles - 1)
        return (t // n_seq, t % n_seq, 0)

    def out_tile(f):
        t = jnp.maximum(f - 1, 0)
        return (t // n_seq, t % n_seq, 0)

    return pl.pallas_call(
        kern,
        out_shape=jax.ShapeDtypeStruct((B, S, D), F32),
        grid=(n_tiles + 1,),
        in_specs=[
            pl.BlockSpec((1, ts, D), in_tile),
            _const_spec((1, D)),
            _const_spec((D, W)), _const_spec((D, W)),
            _const_spec((1, W)), _const_spec((1, W)),
            _const_spec((CONV_W, W)), _const_spec((1, W)),
            _const_spec((LRU_BLOCKS, LRU_BLOCK_W, 2 * LRU_BLOCK_W)),
            _const_spec((2, W)),
            _const_spec((1, W)),
            _const_spec((W, D)),
            _const_spec((ts, ts)), _const_spec((ts, ts)),
        ],
        out_specs=pl.BlockSpec((1, ts, D), out_tile),
        scratch_shapes=[
            pltpu.VMEM((ts, W), F32),
            pltpu.VMEM((ts, W), F32),
            pltpu.VMEM((ts, D), F32),
            pltpu.VMEM((ts + (CONV_W - 1) * SUBLANES, W), F32),
            pltpu.VMEM(((CONV_W - 1) * SUBLANES, W), F32),
            pltpu.VMEM((SUBLANES, W), F32),
        ],
        compiler_params=pltpu.CompilerParams(
            dimension_semantics=("arbitrary",), vmem_limit_bytes=VMEM_LIMIT),
        name="rglru_layer",
    )(x, g.reshape(1, D), wx, wy, bxv, byv, conv_w, conv_b.reshape(1, W), gw, gate_b,
      lam.reshape(1, W), w_out.astype(BF16), jnp.asarray(perm, BF16), jnp.asarray(perm.T, BF16))


def _mlp_kernel(x_ref, g_ref, w1_ref, w2_ref, fg_ref, o_ref, *, final_norm):
    x = x_ref[...]
    hn = _rmsnorm(x, g_ref[...], NORM_EPS).astype(BF16)
    u = jnp.maximum(jnp.dot(hn, w1_ref[...], preferred_element_type=F32), 0.0)
    u = (u * u).astype(BF16)
    y = x + jnp.dot(u, w2_ref[...], preferred_element_type=F32)
    if final_norm:
        y = _rmsnorm(y, fg_ref[...], NORM_EPS)
    o_ref[...] = y


def _mlp_layer(x2, g, w1, w2, final_g, *, final_norm, tm=256):
    M, D = x2.shape
    F = w1.shape[1]
    kern = functools.partial(_mlp_kernel, final_norm=final_norm)
    return pl.pallas_call(
        kern,
        out_shape=jax.ShapeDtypeStruct((M, D), F32),
        grid=(M // tm,),
        in_specs=[
            pl.BlockSpec((tm, D), lambda i: (i, 0)),
            _const_spec((1, D)),
            _const_spec((D, F)),
            _const_spec((F, D)),
            _const_spec((1, D)),
        ],
        out_specs=pl.BlockSpec((tm, D), lambda i: (i, 0)),
        compiler_params=pltpu.CompilerParams(
            dimension_semantics=("parallel",), vmem_limit_bytes=VMEM_LIMIT),
        name="mlp_layer",
    )(x2, g.reshape(1, D), w1.astype(BF16), w2.astype(BF16), final_g.reshape(1, D))


def _kv_kernel(x_ref, g_ref, wk_ref, wv_ref, k_ref, vt_ref):
    hn = _rmsnorm(x_ref[0], g_ref[...], NORM_EPS).astype(BF16)
    k_ref[0] = jnp.dot(hn, wk_ref[...], preferred_element_type=F32).astype(BF16)
    v = jnp.dot(hn, wv_ref[...], preferred_element_type=F32)
    vt_ref[0] = v.T.astype(BF16)


def _kv_proj(x, g, w_kv, *, tm=512):
    B, S, D = x.shape
    return pl.pallas_call(
        _kv_kernel,
        out_shape=(jax.ShapeDtypeStruct((B, S, D), BF16), jax.ShapeDtypeStruct((B, D, S), BF16)),
        grid=(B, S // tm),
        in_specs=[
            pl.BlockSpec((1, tm, D), lambda b, s: (b, s, 0)),
            _const_spec((1, D)), _const_spec((D, D)), _const_spec((D, D)),
        ],
        out_specs=(pl.BlockSpec((1, tm, D), lambda b, s: (b, s, 0)),
                   pl.BlockSpec((1, D, tm), lambda b, s: (b, 0, s))),
        compiler_params=pltpu.CompilerParams(
            dimension_semantics=("parallel", "parallel"), vmem_limit_bytes=VMEM_LIMIT),
        name="kv_proj",
    )(x, g.reshape(1, D), w_kv[:, :D].astype(BF16), w_kv[:, D:].astype(BF16))


def _q_kernel(x_ref, g_ref, wq_ref, qt_ref, *, scale):
    hn = _rmsnorm(x_ref[0], g_ref[...], NORM_EPS).astype(BF16)
    q = jnp.dot(hn, wq_ref[...], preferred_element_type=F32) * scale
    qt_ref[0] = q.T.astype(BF16)


def _q_proj(x, g, w_q, *, tm=512):
    B, S, D = x.shape
    kern = functools.partial(_q_kernel, scale=LOG2E * HEAD_DIM ** -0.5)
    return pl.pallas_call(
        kern,
        out_shape=jax.ShapeDtypeStruct((B, D, S), BF16),
        grid=(B, S // tm),
        in_specs=[
            pl.BlockSpec((1, tm, D), lambda b, s: (b, s, 0)),
            _const_spec((1, D)), _const_spec((D, D)),
        ],
        out_specs=pl.BlockSpec((1, D, tm), lambda b, s: (b, 0, s)),
        compiler_params=pltpu.CompilerParams(
            dimension_semantics=("parallel", "parallel"), vmem_limit_bytes=VMEM_LIMIT),
        name="q_proj",
    )(x, g.reshape(1, D), w_q.astype(BF16))


def _bf16_terms(c, n=3):
    terms, rest = [], np.float32(c)
    for _ in range(n):
        t = np.float32(np.asarray(rest, dtype=BF16))
        terms.append(float(t))
        rest = np.float32(rest - t)
    return terms


def _attn_kernel(qt_ref, k_ref, vt_ref, lam_ref, sg_ref, ot_ref,
                 qaug_ref, kaug_ref, sc_ref, m_ref, acc_ref, *, tq, lam_init, slopes):
    qi = pl.program_id(1)
    tk = tq
    LA = SCORE_LOOKAHEAD
    log2_slopes = [s * LOG2E for s in slopes]

    @pl.when((pl.program_id(0) == 0) & (qi == 0))
    def _():
        arow = lax.broadcasted_iota(jnp.int32, (V_DIM, tq), 0)
        dq_row = lax.broadcasted_iota(jnp.int32, (V_DIM, tq), 1).astype(F32)
        col = lax.broadcasted_iota(jnp.int32, (tk, V_DIM), 1)
        dk_col = lax.broadcasted_iota(jnp.int32, (tk, V_DIM), 0).astype(F32)
        for h in range(N_HEADS):
            c = _bf16_terms(log2_slopes[h])
            q_aug = jnp.where(arow == 0, c[0], jnp.where(arow == 1, c[1], jnp.where(
                arow == 2, c[2], jnp.where(arow < 6, dq_row, 0.0)))).astype(BF16)
            kaug_ref[h] = jnp.where(col < 3, dk_col, jnp.where(col == 3, -c[0], jnp.where(
                col == 4, -c[1], jnp.where(col == 5, -c[2], 0.0)))).astype(BF16)
            qaug_ref[2 * h, V_DIM:, :] = q_aug
            qaug_ref[2 * h + 1, V_DIM:, :] = q_aug

    row = lax.broadcasted_iota(jnp.int32, (V_DIM, tq), 0)
    for h in range(N_HEADS):
        qt = qt_ref[0, h * V_DIM:(h + 1) * V_DIM, :]
        zero = jnp.zeros_like(qt)
        qaug_ref[2 * h, 0:V_DIM, :] = jnp.where(row < HEAD_DIM, qt, zero)
        qaug_ref[2 * h + 1, 0:V_DIM, :] = jnp.where(row >= HEAD_DIM, qt, zero)

    m_ref[...] = jnp.full(m_ref.shape, -jnp.inf, F32)
    acc_ref[...] = jnp.zeros(acc_ref.shape, F32)
    ones_rows = jnp.ones((ONES_ROWS, tk), BF16)

    def issue_scores(j, h):
        off = pl.multiple_of(j * tk, tk)
        kt = k_ref[0, pl.ds(off, tk), h * V_DIM:(h + 1) * V_DIM]
        lhs = jnp.concatenate([kt, kaug_ref[h]], axis=1)
        for c in range(2):
            sc_ref[h % LA, c] = jnp.dot(lhs, qaug_ref[2 * h + c], preferred_element_type=F32)

    def softmax_pv(j, h, future):
        off = pl.multiple_of(j * tk, tk)
        vt = vt_ref[0, h * V_DIM:(h + 1) * V_DIM, pl.ds(off, tk)]
        vta = jnp.concatenate([vt, ones_rows], axis=0)
        tile_bias = -log2_slopes[h] * ((qi - j) * tk).astype(F32)
        for c in range(2):
            idx = 2 * h + c
            sc = sc_ref[h % LA, c]
            if future is not None:
                sc = jnp.where(future, -jnp.inf, sc)
            m_old = m_ref[idx]
            m_new = jnp.maximum(m_old, jnp.max(sc, axis=0, keepdims=True) + tile_bias)
            alpha = jnp.exp2(m_old - m_new)
            p = jnp.exp2(sc - (m_new - tile_bias)).astype(BF16)
            acc_ref[idx] = alpha * acc_ref[idx] + jnp.dot(vta, p, preferred_element_type=F32)
            m_ref[idx] = m_new

    for h in range(LA):
        issue_scores(0, h)

    def unmasked_tile(j):
        for h in range(N_HEADS):
            softmax_pv(j, h, None)
            if h + LA < N_HEADS:
                issue_scores(j, h + LA)
            else:
                issue_scores(j + 1, h + LA - N_HEADS)

    def tile_pair(i, carry):
        unmasked_tile(2 * i)
        unmasked_tile(2 * i + 1)
        return carry

    def odd_tile(i, carry):
        unmasked_tile(qi - 1)
        return carry

    lax.fori_loop(0, qi // 2, tile_pair, 0)
    lax.fori_loop(0, qi % 2, odd_tile, 0)

    future = (lax.broadcasted_iota(jnp.int32, (tk, tq), 0)
              > lax.broadcasted_iota(jnp.int32, (tk, tq), 1))
    for h in range(N_HEADS):
        softmax_pv(qi, h, future)
        if h + LA < N_HEADS:
            issue_scores(qi, h + LA)

    lv = lam_ref[0]
    lam = (jnp.exp(jnp.sum(lv[0:1] * lv[1:2], keepdims=True))
           - jnp.exp(jnp.sum(lv[2:3] * lv[3:4], keepdims=True)) + lam_init)
    for h in range(N_HEADS):
        a1 = acc_ref[2 * h]
        a2 = acc_ref[2 * h + 1]
        o = (a1[0:V_DIM] / a1[V_DIM:V_DIM + 1]) - lam * (a2[0:V_DIM] / a2[V_DIM:V_DIM + 1])
        o = o * lax.rsqrt(jnp.mean(o * o, axis=0, keepdims=True) + SUBLN_EPS) * sg_ref[...]
        ot_ref[0, h * V_DIM:(h + 1) * V_DIM, :] = (o * (1.0 - lam_init)).astype(BF16)


def _diff_attention(qt, k, vt, lam_vecs, subln_g, slopes, lam_init, *, tq=256):
    B, D, S = qt.shape
    assert N_HEADS % SCORE_LOOKAHEAD == 0 and tq <= 256
    kern = functools.partial(_attn_kernel, tq=tq, lam_init=lam_init, slopes=slopes)
    return pl.pallas_call(
        kern,
        out_shape=jax.ShapeDtypeStruct((B, D, S), BF16),
        grid=(B, S // tq),
        in_specs=[
            pl.BlockSpec((1, D, tq), lambda b, i: (b, 0, i)),
            pl.BlockSpec((1, S, D), lambda b, i: (b, 0, 0)),
            pl.BlockSpec((1, D, S), lambda b, i: (b, 0, 0)),
            _const_spec((1, 4, HEAD_DIM)),
            _const_spec((V_DIM, 1)),
        ],
        out_specs=pl.BlockSpec((1, D, tq), lambda b, i: (b, 0, i)),
        scratch_shapes=[
            pltpu.VMEM((2 * N_HEADS, 2 * V_DIM, tq), BF16),
            pltpu.VMEM((N_HEADS, tq, V_DIM), BF16),
            pltpu.VMEM((SCORE_LOOKAHEAD, 2, tq, tq), F32),
            pltpu.VMEM((2 * N_HEADS, 1, tq), F32),
            pltpu.VMEM((2 * N_HEADS, V_DIM + ONES_ROWS, tq), F32),
        ],
        compiler_params=pltpu.CompilerParams(
            dimension_semantics=("arbitrary", "arbitrary"), vmem_limit_bytes=VMEM_LIMIT),
        name="diff_attention",
    )(qt, k, vt, lam_vecs.reshape(1, 4, HEAD_DIM), subln_g.reshape(V_DIM, 1))


def _out_kernel(at_ref, w_ref, x_ref, o_ref):
    y = lax.dot_general(at_ref[0], w_ref[...], (((0,), (0,)), ((), ())), preferred_element_type=F32)
    o_ref[0] = x_ref[0] + y


def _out_proj(at, w_o, x, *, tm=512):
    B, D, S = at.shape
    return pl.pallas_call(
        _out_kernel,
        out_shape=jax.ShapeDtypeStruct((B, S, D), F32),
        grid=(B, S // tm),
        in_specs=[
            pl.BlockSpec((1, D, tm), lambda b, s: (b, 0, s)),
            _const_spec((D, D)),
            pl.BlockSpec((1, tm, D), lambda b, s: (b, s, 0)),
        ],
        out_specs=pl.BlockSpec((1, tm, D), lambda b, s: (b, s, 0)),
        compiler_params=pltpu.CompilerParams(
            dimension_semantics=("parallel", "parallel"), vmem_limit_bytes=VMEM_LIMIT),
        name="attn_out_proj",
    )(at, w_o.astype(BF16), x)


def _attention_layer(x, k_v, g, w_q, lam_vecs, subln_g, w_o, slopes, lam_init):
    k, vt = k_v
    qt = _q_proj(x, g, w_q)
    at = _diff_attention(qt, k, vt, lam_vecs, subln_g, slopes, lam_init)
    return _out_proj(at, w_o, x)


def kernel(x, a_norm, a_w_in, a_b_in, a_conv_w, a_conv_b, a_gate_w, a_gate_b, a_lambda, a_w_out,
           kv_norm, w_kv, b_norm, b_w_q, b_lam, b_subln, b_w_o,
           mlp_norm, mlp_w1, mlp_w2, final_norm):
    B, S, D = x.shape
    depth = mlp_w1.shape[0]
    n_a = a_w_in.shape[0]
    slopes = tuple(2.0 ** (-8.0 * (h + 1) / N_HEADS) for h in range(N_HEADS))
    k_v = None
    for l in range(depth):
        if l < n_a:
            x = _recurrent_layer(x, a_norm[l], a_w_in[l], a_b_in[l], a_conv_w[l], a_conv_b[l],
                                 a_gate_w[l], a_gate_b[l], a_lambda[l], a_w_out[l])
        else:
            j = l - n_a
            lam_init = 0.8 - 0.6 * math.exp(-0.3 * l)
            x = _attention_layer(x, k_v, b_norm[j], b_w_q[j], b_lam[j], b_subln[j], b_w_o[j],
                                 slopes, lam_init)
        x = _mlp_layer(x.reshape(B * S, D), mlp_norm[l], mlp_w1[l], mlp_w2[l], final_norm,
                       final_norm=(l == depth - 1)).reshape(B, S, D)
        if l == n_a - 1:
            k_v = _kv_proj(x, kv_norm, w_kv)
    return x
```

```python
import functools
import math

import jax
import jax.numpy as jnp
import numpy as np
from jax import lax
from jax.experimental import pallas as pl
from jax.experimental.pallas import tpu as pltpu

D_MODEL = 1024
N_HEADS = 8
HEAD_DIM = 64
V_DIM = 128
LRU_BLOCKS = 8
LRU_BLOCK_W = 128
CONV_W = 4
LRU_C = 8.0
NORM_EPS = 1e-6
SUBLN_EPS = 1e-5

SUBLANES = 8
ONES_ROWS = 16
SCORE_LOOKAHEAD = 4
LOG2E = math.log2(math.e)
REC_COLS = 256
REC_GATE_LOOKAHEAD = 2
PROJ_ROWS = 256
VMEM_LIMIT = 56 * 1024 * 1024

BF16 = jnp.bfloat16
F32 = jnp.float32


def _rmsnorm(x, g, eps):
    return x * lax.rsqrt(jnp.mean(x * x, axis=-1, keepdims=True) + eps) * g


def _pack_rows(w):
    *lead, K, N = w.shape
    pairs = jnp.swapaxes(w.astype(BF16).reshape(*lead, K // 2, 2, N), -1, -2)
    return lax.bitcast_convert_type(pairs, jnp.uint32)


def _bf16(packed):
    return pltpu.bitcast(packed, BF16)


def _const_spec(shape):
    nd = len(shape)
    return pl.BlockSpec(shape, lambda *_: (0,) * nd, pipeline_mode=pl.Buffered(1))


def _gelu_tanh(x):
    c1 = math.sqrt(2.0 / math.pi)
    return x * (0.5 + 0.5 * jnp.tanh(x * (c1 + (c1 * 0.044715) * (x * x))))


def _rec_kernel(x_ref, g_ref, wx_ref, wy_ref, bx_ref, by_ref, cw_ref, cb_ref, gw_ref, gb_ref,
                lam_ref, wo_ref, perm_ref, unperm_ref, xres_ref, o_ref,
                hn_s, xb_s, yb_s, xpad_ref, tail_ref, h_ref, *, ts, tiles_per_seq):
    f = pl.program_id(0)
    W = D_MODEL
    G = ts // SUBLANES
    HALO = (CONV_W - 1) * SUBLANES
    NB = W // REC_COLS
    starts_sequence = lax.rem(f - 2, tiles_per_seq) == 0

    @pl.when(f == 0)
    def _():
        hn_s[...] = jnp.zeros_like(hn_s)
        xb_s[...] = jnp.zeros_like(xb_s)
        yb_s[...] = jnp.zeros_like(yb_s)
        tail_ref[...] = jnp.zeros_like(tail_ref)
        h_ref[...] = jnp.zeros_like(h_ref)

    row8 = lax.broadcasted_iota(jnp.int32, (SUBLANES, REC_COLS), 0)

    def norm_permute():
        hn = _rmsnorm(x_ref[0], g_ref[...], NORM_EPS).astype(BF16)
        hn_s[...] = jnp.dot(_bf16(perm_ref[...]), hn, preferred_element_type=F32).astype(BF16)

    def in_proj(cb):
        cs = slice(cb * REC_COLS, (cb + 1) * REC_COLS)
        hn = hn_s[...]
        xb_s[:, cs] = jnp.dot(hn, _bf16(wx_ref[:, cs]), preferred_element_type=F32) + bx_ref[:, cs]
        yb_s[:, cs] = jnp.dot(hn, _bf16(wy_ref[:, cs]), preferred_element_type=F32) + by_ref[:, cs]

    def load_proj(cb):
        cs = slice(cb * REC_COLS, (cb + 1) * REC_COLS)
        return xb_s[:, cs], yb_s[:, cs]

    def conv_gates(cb, xb, yb):
        cs = slice(cb * REC_COLS, (cb + 1) * REC_COLS)
        for k in range(CONV_W - 1):
            cur = xb[ts - HALO + k * SUBLANES:ts - HALO + (k + 1) * SUBLANES, :]
            prev = jnp.where(starts_sequence, 0.0, tail_ref[k * SUBLANES:(k + 1) * SUBLANES, cs])
            xpad_ref[k * SUBLANES:(k + 1) * SUBLANES, cs] = jnp.where(
                row8 == 0, pltpu.roll(prev, 1, axis=0), pltpu.roll(cur, 1, axis=0))
        tail_ref[:, cs] = xb[ts - HALO:, :]
        xpad_ref[HALO:, cs] = xb
        xc = xb * cw_ref[CONV_W - 1:CONV_W, cs] + cb_ref[:, cs]
        for j in range(CONV_W - 1):
            xc = xc + xpad_ref[j * SUBLANES:j * SUBLANES + ts, cs] * cw_ref[j:j + 1, cs]
        xcb = xc.astype(BF16)
        r_parts, i_parts = [], []
        for n in range(REC_COLS // LRU_BLOCK_W):
            gn = jnp.dot(xcb[:, n * LRU_BLOCK_W:(n + 1) * LRU_BLOCK_W],
                         _bf16(gw_ref[cb * (REC_COLS // LRU_BLOCK_W) + n]), preferred_element_type=F32)
            r_parts.append(gn[:, :LRU_BLOCK_W])
            i_parts.append(gn[:, LRU_BLOCK_W:])
        return xc, jnp.concatenate(r_parts, axis=1), jnp.concatenate(i_parts, axis=1), yb

    def recurrence(cb, xc, g_r, g_i, yb):
        cs = slice(cb * REC_COLS, (cb + 1) * REC_COLS)
        r = jax.nn.sigmoid(g_r + gb_ref[0:1, cs])
        ig = jax.nn.sigmoid(g_i + gb_ref[1:2, cs])
        z = -lam_ref[:, cs]
        softplus = jnp.maximum(z, 0.0) + jnp.log1p(jnp.exp(-jnp.abs(z)))
        log_a = (-LRU_C * softplus) * r
        a = jnp.exp(log_a)
        var = -jnp.tanh(log_a) * (a * a + 1.0)
        bx = jnp.where(var > 0.0, var * lax.rsqrt(var), 0.0) * (ig * xc)

        h_loc = jnp.zeros((SUBLANES, REC_COLS), F32)
        prod = jnp.ones((SUBLANES, REC_COLS), F32)
        h_steps, p_steps = [], []
        for g in range(G):
            a_g = a[g * SUBLANES:(g + 1) * SUBLANES, :]
            h_loc = a_g * h_loc + bx[g * SUBLANES:(g + 1) * SUBLANES, :]
            prod = a_g * prod
            h_steps.append(h_loc)
            p_steps.append(prod)

        t_cum, f_cum = prod, h_loc
        for d in (1, 2, 4):
            keep = row8 >= d
            f_cum = jnp.where(keep, t_cum * pltpu.roll(f_cum, d, axis=0) + f_cum, f_cum)
            t_cum = jnp.where(keep, t_cum * pltpu.roll(t_cum, d, axis=0), t_cum)
        h_prev = jnp.where(starts_sequence, 0.0, h_ref[SUBLANES - 1:SUBLANES, cs])
        h_end = t_cum * h_prev + f_cum
        h_in = jnp.where(row8 == 0, h_prev, pltpu.roll(h_end, 1, axis=0))
        h_ref[:, cs] = h_end
        hs = jnp.concatenate([h_steps[g] + p_steps[g] * h_in for g in range(G)], axis=0)
        return (hs * _gelu_tanh(yb)).astype(BF16)

    def out_proj(cb, y):
        y = jnp.dot(_bf16(unperm_ref[...]), y, preferred_element_type=F32).astype(BF16)
        w_rows = wo_ref[cb * REC_COLS // 2:(cb + 1) * REC_COLS // 2, :]
        return jnp.dot(y, _bf16(w_rows), preferred_element_type=F32)

    gated = {cb: conv_gates(cb, *load_proj(cb)) for cb in range(REC_GATE_LOOKAHEAD)}
    for cb in range(REC_GATE_LOOKAHEAD):
        in_proj(cb)
    out = xres_ref[0]
    for cb in range(NB):
        nxt = cb + REC_GATE_LOOKAHEAD
        if nxt < NB:
            loaded = load_proj(nxt)
            gated[nxt] = conv_gates(nxt, *loaded)
        y = recurrence(cb, *gated.pop(cb))
        out = out + out_proj(cb, y)
        if nxt < NB:
            in_proj(nxt)
        if nxt == NB:
            norm_permute()
    o_ref[0] = out


def _chunk_permutation(ts):
    steps = ts // SUBLANES
    p = np.zeros((ts, ts), np.float32)
    for c in range(SUBLANES):
        for s in range(steps):
            p[s * SUBLANES + c, c * steps + s] = 1.0
    return p


def _recurrent_layer(x, g, w_in, b_in, conv_w, conv_b, gate_w, gate_b, lam, w_out, *, ts=256):
    B, S, D = x.shape
    W = D_MODEL
    n_seq = S // ts
    n_tiles = B * n_seq
    wx = _pack_rows(w_in[:, :W])
    wy = _pack_rows(w_in[:, W:])
    bxv = b_in[:W].reshape(1, W)
    byv = b_in[W:].reshape(1, W)
    gw = _pack_rows(jnp.concatenate([gate_w[0], gate_w[1]], axis=-1))
    perm = _chunk_permutation(ts)
    kern = functools.partial(_rec_kernel, ts=ts, tiles_per_seq=n_seq)

    def in_tile(f):
        t = jnp.minimum(f, n_tiles - 1)
        return (t // n_seq, t % n_seq, 0)

    def out_tile(f):
        t = jnp.maximum(f - 2, 0)
        return (t // n_seq, t % n_seq, 0)

    return pl.pallas_call(
        kern,
        out_shape=jax.ShapeDtypeStruct((B, S, D), F32),
        grid=(n_tiles + 2,),
        in_specs=[
            pl.BlockSpec((1, ts, D), in_tile),
            _const_spec((1, D)),
            _const_spec((D // 2, W)), _const_spec((D // 2, W)),
            _const_spec((1, W)), _const_spec((1, W)),
            _const_spec((CONV_W, W)), _const_spec((1, W)),
            _const_spec((LRU_BLOCKS, LRU_BLOCK_W // 2, 2 * LRU_BLOCK_W)),
            _const_spec((2, W)),
            _const_spec((1, W)),
            _const_spec((W // 2, D)),
            _const_spec((ts // 2, ts)), _const_spec((ts // 2, ts)),
            pl.BlockSpec((1, ts, D), out_tile),
        ],
        out_specs=pl.BlockSpec((1, ts, D), out_tile),
        scratch_shapes=[
            pltpu.VMEM((ts, D), BF16),
            pltpu.VMEM((ts, W), F32),
            pltpu.VMEM((ts, W), F32),
            pltpu.VMEM((ts + (CONV_W - 1) * SUBLANES, W), F32),
            pltpu.VMEM(((CONV_W - 1) * SUBLANES, W), F32),
            pltpu.VMEM((SUBLANES, W), F32),
        ],
        compiler_params=pltpu.CompilerParams(
            dimension_semantics=("arbitrary",), vmem_limit_bytes=VMEM_LIMIT),
        name="rglru_layer",
    )(x, g.reshape(1, D), wx, wy, bxv, byv, conv_w, conv_b.reshape(1, W), gw, gate_b,
      lam.reshape(1, W), _pack_rows(w_out), _pack_rows(jnp.asarray(perm)), _pack_rows(jnp.asarray(perm.T)), x)


def _mlp_kernel(x_ref, g_ref, w1_ref, w2_ref, fg_ref, o_ref, *, final_norm):
    x = x_ref[...]
    hn = _rmsnorm(x, g_ref[...], NORM_EPS).astype(BF16)
    u = jnp.maximum(jnp.dot(hn, _bf16(w1_ref[...]), preferred_element_type=F32), 0.0)
    u = (u * u).astype(BF16)
    y = x + jnp.dot(u, _bf16(w2_ref[...]), preferred_element_type=F32)
    if final_norm:
        y = _rmsnorm(y, fg_ref[...], NORM_EPS)
    o_ref[...] = y


def _mlp_layer(x2, g, w1, w2, final_g, *, final_norm, tm=256):
    M, D = x2.shape
    F = w1.shape[1]
    kern = functools.partial(_mlp_kernel, final_norm=final_norm)
    return pl.pallas_call(
        kern,
        out_shape=jax.ShapeDtypeStruct((M, D), F32),
        grid=(M // tm,),
        in_specs=[
            pl.BlockSpec((tm, D), lambda i: (i, 0)),
            _const_spec((1, D)),
            _const_spec((D // 2, F)),
            _const_spec((F // 2, D)),
            _const_spec((1, D)),
        ],
        out_specs=pl.BlockSpec((tm, D), lambda i: (i, 0)),
        compiler_params=pltpu.CompilerParams(
            dimension_semantics=("parallel",), vmem_limit_bytes=VMEM_LIMIT),
        name="mlp_layer",
    )(x2, g.reshape(1, D), _pack_rows(w1), _pack_rows(w2), final_g.reshape(1, D))


def _kv_kernel(x_ref, g_ref, wk_ref, wv_ref, k_ref, vt_ref):
    for r0 in range(0, x_ref.shape[1], PROJ_ROWS):
        rows = slice(r0, r0 + PROJ_ROWS)
        hn = _rmsnorm(x_ref[0, rows, :], g_ref[...], NORM_EPS).astype(BF16)
        k = jnp.dot(hn, _bf16(wk_ref[...]), preferred_element_type=F32).astype(BF16)
        k_ref[0, r0 // 2:(r0 + PROJ_ROWS) // 2, :] = pltpu.bitcast(k, jnp.uint32)
        v = jnp.dot(hn, _bf16(wv_ref[...]), preferred_element_type=F32)
        vt_ref[0, :, rows] = pltpu.bitcast(v.T.astype(BF16), jnp.uint32)


def _kv_proj(x, g, w_kv, *, tm=1024):
    B, S, D = x.shape
    return pl.pallas_call(
        _kv_kernel,
        out_shape=(jax.ShapeDtypeStruct((B, S // 2, D), jnp.uint32),
                   jax.ShapeDtypeStruct((B, D // 2, S), jnp.uint32)),
        grid=(B, S // tm),
        in_specs=[
            pl.BlockSpec((1, tm, D), lambda b, s: (b, s, 0)),
            _const_spec((1, D)), _const_spec((D // 2, D)), _const_spec((D // 2, D)),
        ],
        out_specs=(pl.BlockSpec((1, tm // 2, D), lambda b, s: (b, s, 0)),
                   pl.BlockSpec((1, D // 2, tm), lambda b, s: (b, 0, s))),
        compiler_params=pltpu.CompilerParams(
            dimension_semantics=("parallel", "parallel"), vmem_limit_bytes=VMEM_LIMIT),
        name="kv_proj",
    )(x, g.reshape(1, D), _pack_rows(w_kv[:, :D]), _pack_rows(w_kv[:, D:]))


def _q_kernel(x_ref, g_ref, wq_ref, qt_ref, *, scale):
    for r0 in range(0, x_ref.shape[1], PROJ_ROWS):
        rows = slice(r0, r0 + PROJ_ROWS)
        hn = _rmsnorm(x_ref[0, rows, :], g_ref[...], NORM_EPS).astype(BF16)
        q = jnp.dot(hn, _bf16(wq_ref[...]), preferred_element_type=F32) * scale
        qt_ref[0, :, rows] = q.T.astype(BF16)


def _q_proj(x, g, w_q, *, tm=1024):
    B, S, D = x.shape
    kern = functools.partial(_q_kernel, scale=LOG2E * HEAD_DIM ** -0.5)
    return pl.pallas_call(
        kern,
        out_shape=jax.ShapeDtypeStruct((B, D, S), BF16),
        grid=(B, S // tm),
        in_specs=[
            pl.BlockSpec((1, tm, D), lambda b, s: (b, s, 0)),
            _const_spec((1, D)), _const_spec((D // 2, D)),
        ],
        out_specs=pl.BlockSpec((1, D, tm), lambda b, s: (b, 0, s)),
        compiler_params=pltpu.CompilerParams(
            dimension_semantics=("parallel", "parallel"), vmem_limit_bytes=VMEM_LIMIT),
        name="q_proj",
    )(x, g.reshape(1, D), _pack_rows(w_q))


def _bf16_terms(c, n=3):
    terms, rest = [], np.float32(c)
    for _ in range(n):
        t = np.float32(np.asarray(rest, dtype=BF16))
        terms.append(float(t))
        rest = np.float32(rest - t)
    return terms


def _attn_kernel(qt_ref, k_ref, vt_ref, lam_ref, sg_ref, ot_ref,
                 qaug_ref, kaug_ref, sc_ref, m_ref, acc_ref, *, tq, lam_init, slopes):
    qi = pl.program_id(1)
    tk = tq
    LA = SCORE_LOOKAHEAD
    log2_slopes = [s * LOG2E for s in slopes]

    @pl.when((pl.program_id(0) == 0) & (qi == 0))
    def _():
        arow = lax.broadcasted_iota(jnp.int32, (V_DIM, tq), 0)
        dq_row = lax.broadcasted_iota(jnp.int32, (V_DIM, tq), 1).astype(F32)
        col = lax.broadcasted_iota(jnp.int32, (tk, V_DIM), 1)
        dk_col = lax.broadcasted_iota(jnp.int32, (tk, V_DIM), 0).astype(F32)
        for h in range(N_HEADS):
            c = _bf16_terms(log2_slopes[h])
            q_aug = jnp.where(arow == 0, c[0], jnp.where(arow == 1, c[1], jnp.where(
                arow == 2, c[2], jnp.where(arow < 6, dq_row, 0.0)))).astype(BF16)
            kaug_ref[h] = jnp.where(col < 3, dk_col, jnp.where(col == 3, -c[0], jnp.where(
                col == 4, -c[1], jnp.where(col == 5, -c[2], 0.0)))).astype(BF16)
            qaug_ref[2 * h, V_DIM:, :] = q_aug
            qaug_ref[2 * h + 1, V_DIM:, :] = q_aug

    row = lax.broadcasted_iota(jnp.int32, (V_DIM, tq), 0)
    for h in range(N_HEADS):
        qt = qt_ref[0, h * V_DIM:(h + 1) * V_DIM, :]
        zero = jnp.zeros_like(qt)
        qaug_ref[2 * h, 0:V_DIM, :] = jnp.where(row < HEAD_DIM, qt, zero)
        qaug_ref[2 * h + 1, 0:V_DIM, :] = jnp.where(row >= HEAD_DIM, qt, zero)

    m_ref[...] = jnp.full(m_ref.shape, -jnp.inf, F32)
    acc_ref[...] = jnp.zeros(acc_ref.shape, F32)
    ones_rows = jnp.ones((ONES_ROWS, tk), BF16)

    def issue_scores(j, h):
        off2 = pl.multiple_of(j * (tk // 2), tk // 2)
        kt = _bf16(k_ref[0, pl.ds(off2, tk // 2), h * V_DIM:(h + 1) * V_DIM])
        lhs = jnp.concatenate([kt, kaug_ref[h]], axis=1)
        for c in range(2):
            sc_ref[h % LA, c] = jnp.dot(lhs, qaug_ref[2 * h + c], preferred_element_type=F32)

    def softmax_pv(j, h, future):
        off = pl.multiple_of(j * tk, tk)
        vt = _bf16(vt_ref[0, h * V_DIM // 2:(h + 1) * V_DIM // 2, pl.ds(off, tk)])
        vta = jnp.concatenate([vt, ones_rows], axis=0)
        tile_bias = -log2_slopes[h] * ((qi - j) * tk).astype(F32)
        for c in range(2):
            idx = 2 * h + c
            sc = sc_ref[h % LA, c]
            if future is not None:
                sc = jnp.where(future, -jnp.inf, sc)
            m_old = m_ref[idx]
            m_new = jnp.maximum(m_old, jnp.max(sc, axis=0, keepdims=True) + tile_bias)
            alpha = jnp.exp2(m_old - m_new)
            p = jnp.exp2(sc - (m_new - tile_bias)).astype(BF16)
            acc_ref[idx] = alpha * acc_ref[idx] + jnp.dot(vta, p, preferred_element_type=F32)
            m_ref[idx] = m_new

    for h in range(LA):
        issue_scores(0, h)

    def unmasked_tile(j):
        for h in range(N_HEADS):
            softmax_pv(j, h, None)
            if h + LA < N_HEADS:
                issue_scores(j, h + LA)
            else:
                issue_scores(j + 1, h + LA - N_HEADS)

    def tile_pair(i, carry):
        unmasked_tile(2 * i)
        unmasked_tile(2 * i + 1)
        return carry

    def odd_tile(i, carry):
        unmasked_tile(qi - 1)
        return carry

    lax.fori_loop(0, qi // 2, tile_pair, 0)
    lax.fori_loop(0, qi % 2, odd_tile, 0)

    future = (lax.broadcasted_iota(jnp.int32, (tk, tq), 0)
              > lax.broadcasted_iota(jnp.int32, (tk, tq), 1))
    for h in range(N_HEADS):
        softmax_pv(qi, h, future)
        if h + LA < N_HEADS:
            issue_scores(qi, h + LA)

    lv = lam_ref[0]
    lam = (jnp.exp(jnp.sum(lv[0:1] * lv[1:2], keepdims=True))
           - jnp.exp(jnp.sum(lv[2:3] * lv[3:4], keepdims=True)) + lam_init)
    for h in range(N_HEADS):
        a1 = acc_ref[2 * h]
        a2 = acc_ref[2 * h + 1]
        o = (a1[0:V_DIM] / a1[V_DIM:V_DIM + 1]) - lam * (a2[0:V_DIM] / a2[V_DIM:V_DIM + 1])
        o = o * lax.rsqrt(jnp.mean(o * o, axis=0, keepdims=True) + SUBLN_EPS) * sg_ref[...]
        ot_ref[0, h * V_DIM:(h + 1) * V_DIM, :] = (o * (1.0 - lam_init)).astype(BF16)


def _diff_attention(qt, k, vt, lam_vecs, subln_g, slopes, lam_init, *, tq=256):
    B, D, S = qt.shape
    assert N_HEADS % SCORE_LOOKAHEAD == 0 and tq <= 256
    kern = functools.partial(_attn_kernel, tq=tq, lam_init=lam_init, slopes=slopes)
    return pl.pallas_call(
        kern,
        out_shape=jax.ShapeDtypeStruct((B, D, S), BF16),
        grid=(B, S // tq),
        in_specs=[
            pl.BlockSpec((1, D, tq), lambda b, i: (b, 0, i)),
            pl.BlockSpec((1, S // 2, D), lambda b, i: (b, 0, 0)),
            pl.BlockSpec((1, D // 2, S), lambda b, i: (b, 0, 0)),
            _const_spec((1, 4, HEAD_DIM)),
            _const_spec((V_DIM, 1)),
        ],
        out_specs=pl.BlockSpec((1, D, tq), lambda b, i: (b, 0, i)),
        scratch_shapes=[
            pltpu.VMEM((2 * N_HEADS, 2 * V_DIM, tq), BF16),
            pltpu.VMEM((N_HEADS, tq, V_DIM), BF16),
            pltpu.VMEM((SCORE_LOOKAHEAD, 2, tq, tq), F32),
            pltpu.VMEM((2 * N_HEADS, 1, tq), F32),
            pltpu.VMEM((2 * N_HEADS, V_DIM + ONES_ROWS, tq), F32),
        ],
        compiler_params=pltpu.CompilerParams(
            dimension_semantics=("arbitrary", "arbitrary"), vmem_limit_bytes=VMEM_LIMIT),
        name="diff_attention",
    )(qt, k, vt, lam_vecs.reshape(1, 4, HEAD_DIM), subln_g.reshape(V_DIM, 1))


def _out_kernel(at_ref, w_ref, x_ref, o_ref):
    for r0 in range(0, x_ref.shape[1], PROJ_ROWS):
        rows = slice(r0, r0 + PROJ_ROWS)
        y = lax.dot_general(at_ref[0, :, rows], _bf16(w_ref[...]), (((0,), (0,)), ((), ())),
                            preferred_element_type=F32)
        o_ref[0, rows, :] = x_ref[0, rows, :] + y


def _out_proj(at, w_o, x, *, tm=1024):
    B, D, S = at.shape
    return pl.pallas_call(
        _out_kernel,
        out_shape=jax.ShapeDtypeStruct((B, S, D), F32),
        grid=(B, S // tm),
        in_specs=[
            pl.BlockSpec((1, D, tm), lambda b, s: (b, 0, s)),
            _const_spec((D // 2, D)),
            pl.BlockSpec((1, tm, D), lambda b, s: (b, s, 0)),
        ],
        out_specs=pl.BlockSpec((1, tm, D), lambda b, s: (b, s, 0)),
        compiler_params=pltpu.CompilerParams(
            dimension_semantics=("parallel", "parallel"), vmem_limit_bytes=VMEM_LIMIT),
        name="attn_out_proj",
    )(at, _pack_rows(w_o), x)


def _attention_layer(x, k_v, g, w_q, lam_vecs, subln_g, w_o, slopes, lam_init):
    k, vt = k_v
    qt = _q_proj(x, g, w_q)
    at = _diff_attention(qt, k, vt, lam_vecs, subln_g, slopes, lam_init)
    return _out_proj(at, w_o, x)


def kernel(x, a_norm, a_w_in, a_b_in, a_conv_w, a_conv_b, a_gate_w, a_gate_b, a_lambda, a_w_out,
           kv_norm, w_kv, b_norm, b_w_q, b_lam, b_subln, b_w_o,
           mlp_norm, mlp_w1, mlp_w2, final_norm):
    B, S, D = x.shape
    depth = mlp_w1.shape[0]
    n_a = a_w_in.shape[0]
    slopes = tuple(2.0 ** (-8.0 * (h + 1) / N_HEADS) for h in range(N_HEADS))
    k_v = None
    for l in range(depth):
        if l < n_a:
            x = _recurrent_layer(x, a_norm[l], a_w_in[l], a_b_in[l], a_conv_w[l], a_conv_b[l],
                                 a_gate_w[l], a_gate_b[l], a_lambda[l], a_w_out[l])
        else:
            j = l - n_a
            lam_init = 0.8 - 0.6 * math.exp(-0.3 * l)
            x = _attention_layer(x, k_v, b_norm[j], b_w_q[j], b_lam[j], b_subln[j], b_w_o[j],
                                 slopes, lam_init)
        x = _mlp_layer(x.reshape(B * S, D), mlp_norm[l], mlp_w1[l], mlp_w2[l], final_norm,
                       final_norm=(l == depth - 1)).reshape(B, S, D)
        if l == n_a - 1:
            k_v = _kv_proj(x, kv_norm, w_kv)
    return x
```

```python
import functools
import math

import jax
import jax.numpy as jnp
import numpy as np
from jax import lax
from jax.experimental import pallas as pl
from jax.experimental.pallas import tpu as pltpu

D_MODEL = 1024
N_HEADS = 8
HEAD_DIM = 64
V_DIM = 128
LRU_BLOCKS = 8
LRU_BLOCK_W = 128
CONV_W = 4
LRU_C = 8.0
NORM_EPS = 1e-6
SUBLN_EPS = 1e-5

SUBLANES = 8
ONES_ROWS = 16
SCORE_LOOKAHEAD = 4
LOG2E = math.log2(math.e)
REC_COLS = 256
REC_GATE_LOOKAHEAD = 2
PROJ_ROWS = 256
VMEM_LIMIT = 56 * 1024 * 1024

BF16 = jnp.bfloat16
F32 = jnp.float32


def _rmsnorm(x, g, eps):
    return x * lax.rsqrt(jnp.mean(x * x, axis=-1, keepdims=True) + eps) * g


def _bf16(packed):
    return pltpu.bitcast(packed, BF16)


def _const_spec(shape):
    nd = len(shape)
    return pl.BlockSpec(shape, lambda *_: (0,) * nd, pipeline_mode=pl.Buffered(1))


def _gelu_tanh(x):
    c1 = math.sqrt(2.0 / math.pi)
    return x * (0.5 + 0.5 * jnp.tanh(x * (c1 + (c1 * 0.044715) * (x * x))))


def _rec_kernel(x_ref, g_ref, wx_ref, wy_ref, bx_ref, by_ref, cw_ref, cb_ref, gw_ref, gb_ref,
                lam_ref, wo_ref, perm_ref, unperm_ref, xres_ref, o_ref,
                hn_s, xb_s, yb_s, xpad_ref, tail_ref, h_ref, *, ts, tiles_per_seq):
    f = pl.program_id(0)
    W = D_MODEL
    G = ts // SUBLANES
    HALO = (CONV_W - 1) * SUBLANES
    NB = W // REC_COLS
    starts_sequence = lax.rem(f - 2, tiles_per_seq) == 0

    @pl.when(f == 0)
    def _():
        hn_s[...] = jnp.zeros_like(hn_s)
        xb_s[...] = jnp.zeros_like(xb_s)
        yb_s[...] = jnp.zeros_like(yb_s)
        tail_ref[...] = jnp.zeros_like(tail_ref)
        h_ref[...] = jnp.zeros_like(h_ref)

    row8 = lax.broadcasted_iota(jnp.int32, (SUBLANES, REC_COLS), 0)

    def norm_permute():
        hn = _rmsnorm(x_ref[0], g_ref[...], NORM_EPS).astype(BF16)
        hn_s[...] = jnp.dot(perm_ref[...], hn, preferred_element_type=F32).astype(BF16)

    def in_proj(cb):
        cs = slice(cb * REC_COLS, (cb + 1) * REC_COLS)
        hn = hn_s[...]
        xb_s[:, cs] = jnp.dot(hn, wx_ref[:, cs], preferred_element_type=F32) + bx_ref[:, cs]
        yb_s[:, cs] = jnp.dot(hn, wy_ref[:, cs], preferred_element_type=F32) + by_ref[:, cs]

    def load_proj(cb):
        cs = slice(cb * REC_COLS, (cb + 1) * REC_COLS)
        return xb_s[:, cs], yb_s[:, cs]

    def conv_gates(cb, xb, yb):
        cs = slice(cb * REC_COLS, (cb + 1) * REC_COLS)
        for k in range(CONV_W - 1):
            cur = xb[ts - HALO + k * SUBLANES:ts - HALO + (k + 1) * SUBLANES, :]
            prev = jnp.where(starts_sequence, 0.0, tail_ref[k * SUBLANES:(k + 1) * SUBLANES, cs])
            xpad_ref[k * SUBLANES:(k + 1) * SUBLANES, cs] = jnp.where(
                row8 == 0, pltpu.roll(prev, 1, axis=0), pltpu.roll(cur, 1, axis=0))
        tail_ref[:, cs] = xb[ts - HALO:, :]
        xpad_ref[HALO:, cs] = xb
        xc = xb * cw_ref[CONV_W - 1:CONV_W, cs] + cb_ref[:, cs]
        for j in range(CONV_W - 1):
            xc = xc + xpad_ref[j * SUBLANES:j * SUBLANES + ts, cs] * cw_ref[j:j + 1, cs]
        xcb = xc.astype(BF16)
        r_parts, i_parts = [], []
        for n in range(REC_COLS // LRU_BLOCK_W):
            gn = jnp.dot(xcb[:, n * LRU_BLOCK_W:(n + 1) * LRU_BLOCK_W],
                         gw_ref[cb * (REC_COLS // LRU_BLOCK_W) + n], preferred_element_type=F32)
            r_parts.append(gn[:, :LRU_BLOCK_W])
            i_parts.append(gn[:, LRU_BLOCK_W:])
        return xc, jnp.concatenate(r_parts, axis=1), jnp.concatenate(i_parts, axis=1), yb

    def recurrence(cb, xc, g_r, g_i, yb):
        cs = slice(cb * REC_COLS, (cb + 1) * REC_COLS)
        r = jax.nn.sigmoid(g_r + gb_ref[0:1, cs])
        ig = jax.nn.sigmoid(g_i + gb_ref[1:2, cs])
        z = -lam_ref[:, cs]
        softplus = jnp.maximum(z, 0.0) + jnp.log1p(jnp.exp(-jnp.abs(z)))
        log_a = (-LRU_C * softplus) * r
        a = jnp.exp(log_a)
        var = -jnp.tanh(log_a) * (a * a + 1.0)
        bx = jnp.where(var > 0.0, var * lax.rsqrt(var), 0.0) * (ig * xc)

        h_loc = jnp.zeros((SUBLANES, REC_COLS), F32)
        prod = jnp.ones((SUBLANES, REC_COLS), F32)
        h_steps, p_steps = [], []
        for g in range(G):
            a_g = a[g * SUBLANES:(g + 1) * SUBLANES, :]
            h_loc = a_g * h_loc + bx[g * SUBLANES:(g + 1) * SUBLANES, :]
            prod = a_g * prod
            h_steps.append(h_loc)
            p_steps.append(prod)

        t_cum, f_cum = prod, h_loc
        for d in (1, 2, 4):
            keep = row8 >= d
            f_cum = jnp.where(keep, t_cum * pltpu.roll(f_cum, d, axis=0) + f_cum, f_cum)
            t_cum = jnp.where(keep, t_cum * pltpu.roll(t_cum, d, axis=0), t_cum)
        h_prev = jnp.where(starts_sequence, 0.0, h_ref[SUBLANES - 1:SUBLANES, cs])
        h_end = t_cum * h_prev + f_cum
        h_in = jnp.where(row8 == 0, h_prev, pltpu.roll(h_end, 1, axis=0))
        h_ref[:, cs] = h_end
        hs = jnp.concatenate([h_steps[g] + p_steps[g] * h_in for g in range(G)], axis=0)
        return (hs * _gelu_tanh(yb)).astype(BF16)

    def out_proj(cb, y):
        y = jnp.dot(unperm_ref[...], y, preferred_element_type=F32).astype(BF16)
        return jnp.dot(y, wo_ref[cb * REC_COLS:(cb + 1) * REC_COLS, :], preferred_element_type=F32)

    gated = {cb: conv_gates(cb, *load_proj(cb)) for cb in range(REC_GATE_LOOKAHEAD)}
    for cb in range(REC_GATE_LOOKAHEAD):
        in_proj(cb)
    out = xres_ref[0]
    for cb in range(NB):
        nxt = cb + REC_GATE_LOOKAHEAD
        if nxt < NB:
            loaded = load_proj(nxt)
            gated[nxt] = conv_gates(nxt, *loaded)
        y = recurrence(cb, *gated.pop(cb))
        out = out + out_proj(cb, y)
        if nxt < NB:
            in_proj(nxt)
        if nxt == NB:
            norm_permute()
    o_ref[0] = out


def _chunk_permutation(ts):
    steps = ts // SUBLANES
    p = np.zeros((ts, ts), np.float32)
    for c in range(SUBLANES):
        for s in range(steps):
            p[s * SUBLANES + c, c * steps + s] = 1.0
    return p


def _recurrent_layer(x, g, w_in, b_in, conv_w, conv_b, gate_w, gate_b, lam, w_out, *, ts=256):
    B, S, D = x.shape
    W = D_MODEL
    n_seq = S // ts
    n_tiles = B * n_seq
    wx = w_in[:, :W].astype(BF16)
    wy = w_in[:, W:].astype(BF16)
    bxv = b_in[:W].reshape(1, W)
    byv = b_in[W:].reshape(1, W)
    gw = jnp.concatenate([gate_w[0], gate_w[1]], axis=-1).astype(BF16)
    perm = _chunk_permutation(ts)
    kern = functools.partial(_rec_kernel, ts=ts, tiles_per_seq=n_seq)

    def in_tile(f):
        t = jnp.minimum(f, n_tiles - 1)
        return (t // n_seq, t % n_seq, 0)

    def out_tile(f):
        t = jnp.maximum(f - 2, 0)
        return (t // n_seq, t % n_seq, 0)

    return pl.pallas_call(
        kern,
        out_shape=jax.ShapeDtypeStruct((B, S, D), F32),
        grid=(n_tiles + 2,),
        in_specs=[
            pl.BlockSpec((1, ts, D), in_tile),
            _const_spec((1, D)),
            _const_spec((D, W)), _const_spec((D, W)),
            _const_spec((1, W)), _const_spec((1, W)),
            _const_spec((CONV_W, W)), _const_spec((1, W)),
            _const_spec((LRU_BLOCKS, LRU_BLOCK_W, 2 * LRU_BLOCK_W)),
            _const_spec((2, W)),
            _const_spec((1, W)),
            _const_spec((W, D)),
            _const_spec((ts, ts)), _const_spec((ts, ts)),
            pl.BlockSpec((1, ts, D), out_tile),
        ],
        out_specs=pl.BlockSpec((1, ts, D), out_tile),
        scratch_shapes=[
            pltpu.VMEM((ts, D), BF16),
            pltpu.VMEM((ts, W), F32),
            pltpu.VMEM((ts, W), F32),
            pltpu.VMEM((ts + (CONV_W - 1) * SUBLANES, W), F32),
            pltpu.VMEM(((CONV_W - 1) * SUBLANES, W), F32),
            pltpu.VMEM((SUBLANES, W), F32),
        ],
        compiler_params=pltpu.CompilerParams(
            dimension_semantics=("arbitrary",), vmem_limit_bytes=VMEM_LIMIT),
        name="rglru_layer",
    )(x, g.reshape(1, D), wx, wy, bxv, byv, conv_w, conv_b.reshape(1, W), gw, gate_b,
      lam.reshape(1, W), w_out.astype(BF16), jnp.asarray(perm, BF16), jnp.asarray(perm.T, BF16), x)


def _mlp_kernel(x_ref, g_ref, w1_ref, w2_ref, fg_ref, o_ref, *, final_norm):
    x = x_ref[...]
    hn = _rmsnorm(x, g_ref[...], NORM_EPS).astype(BF16)
    u = jnp.maximum(jnp.dot(hn, w1_ref[...], preferred_element_type=F32), 0.0)
    u = (u * u).astype(BF16)
    y = x + jnp.dot(u, w2_ref[...], preferred_element_type=F32)
    if final_norm:
        y = _rmsnorm(y, fg_ref[...], NORM_EPS)
    o_ref[...] = y


def _mlp_layer(x2, g, w1, w2, final_g, *, final_norm, tm=512):
    M, D = x2.shape
    F = w1.shape[1]
    kern = functools.partial(_mlp_kernel, final_norm=final_norm)
    return pl.pallas_call(
        kern,
        out_shape=jax.ShapeDtypeStruct((M, D), F32),
        grid=(M // tm,),
        in_specs=[
            pl.BlockSpec((tm, D), lambda i: (i, 0)),
            _const_spec((1, D)),
            _const_spec((D, F)),
            _const_spec((F, D)),
            _const_spec((1, D)),
        ],
        out_specs=pl.BlockSpec((tm, D), lambda i: (i, 0)),
        compiler_params=pltpu.CompilerParams(
            dimension_semantics=("parallel",), vmem_limit_bytes=VMEM_LIMIT),
        name="mlp_layer",
    )(x2, g.reshape(1, D), w1.astype(BF16), w2.astype(BF16), final_g.reshape(1, D))


def _kv_kernel(x_ref, g_ref, wk_ref, wv_ref, k_ref, vt_ref):
    for r0 in range(0, x_ref.shape[1], PROJ_ROWS):
        rows = slice(r0, r0 + PROJ_ROWS)
        hn = _rmsnorm(x_ref[0, rows, :], g_ref[...], NORM_EPS).astype(BF16)
        k = jnp.dot(hn, wk_ref[...], preferred_element_type=F32).astype(BF16)
        k_ref[0, r0 // 2:(r0 + PROJ_ROWS) // 2, :] = pltpu.bitcast(k, jnp.uint32)
        v = jnp.dot(hn, wv_ref[...], preferred_element_type=F32)
        vt_ref[0, :, rows] = pltpu.bitcast(v.T.astype(BF16), jnp.uint32)


def _kv_proj(x, g, w_kv, *, tm=1024):
    B, S, D = x.shape
    return pl.pallas_call(
        _kv_kernel,
        out_shape=(jax.ShapeDtypeStruct((B, S // 2, D), jnp.uint32),
                   jax.ShapeDtypeStruct((B, D // 2, S), jnp.uint32)),
        grid=(B, S // tm),
        in_specs=[
            pl.BlockSpec((1, tm, D), lambda b, s: (b, s, 0)),
            _const_spec((1, D)), _const_spec((D, D)), _const_spec((D, D)),
        ],
        out_specs=(pl.BlockSpec((1, tm // 2, D), lambda b, s: (b, s, 0)),
                   pl.BlockSpec((1, D // 2, tm), lambda b, s: (b, 0, s))),
        compiler_params=pltpu.CompilerParams(
            dimension_semantics=("parallel", "parallel"), vmem_limit_bytes=VMEM_LIMIT),
        name="kv_proj",
    )(x, g.reshape(1, D), w_kv[:, :D].astype(BF16), w_kv[:, D:].astype(BF16))


def _q_kernel(x_ref, g_ref, wq_ref, qt_ref, *, scale):
    for r0 in range(0, x_ref.shape[1], PROJ_ROWS):
        rows = slice(r0, r0 + PROJ_ROWS)
        hn = _rmsnorm(x_ref[0, rows, :], g_ref[...], NORM_EPS).astype(BF16)
        q = jnp.dot(hn, wq_ref[...], preferred_element_type=F32) * scale
        qt_ref[0, :, rows] = q.T.astype(BF16)


def _q_proj(x, g, w_q, *, tm=1024):
    B, S, D = x.shape
    kern = functools.partial(_q_kernel, scale=LOG2E * HEAD_DIM ** -0.5)
    return pl.pallas_call(
        kern,
        out_shape=jax.ShapeDtypeStruct((B, D, S), BF16),
        grid=(B, S // tm),
        in_specs=[
            pl.BlockSpec((1, tm, D), lambda b, s: (b, s, 0)),
            _const_spec((1, D)), _const_spec((D, D)),
        ],
        out_specs=pl.BlockSpec((1, D, tm), lambda b, s: (b, 0, s)),
        compiler_params=pltpu.CompilerParams(
            dimension_semantics=("parallel", "parallel"), vmem_limit_bytes=VMEM_LIMIT),
        name="q_proj",
    )(x, g.reshape(1, D), w_q.astype(BF16))


def _bf16_terms(c, n=3):
    terms, rest = [], np.float32(c)
    for _ in range(n):
        t = np.float32(np.asarray(rest, dtype=BF16))
        terms.append(float(t))
        rest = np.float32(rest - t)
    return terms


def _attn_kernel(qt_ref, k_ref, vt_ref, lam_ref, sg_ref, ot_ref,
                 qaug_ref, kaug_ref, sc_ref, m_ref, acc_ref, *, tq, lam_init, slopes):
    qi = pl.program_id(1)
    tk = tq
    LA = SCORE_LOOKAHEAD
    log2_slopes = [s * LOG2E for s in slopes]

    @pl.when((pl.program_id(0) == 0) & (qi == 0))
    def _():
        arow = lax.broadcasted_iota(jnp.int32, (V_DIM, tq), 0)
        dq_row = lax.broadcasted_iota(jnp.int32, (V_DIM, tq), 1).astype(F32)
        col = lax.broadcasted_iota(jnp.int32, (tk, V_DIM), 1)
        dk_col = lax.broadcasted_iota(jnp.int32, (tk, V_DIM), 0).astype(F32)
        for h in range(N_HEADS):
            c = _bf16_terms(log2_slopes[h])
            q_aug = jnp.where(arow == 0, c[0], jnp.where(arow == 1, c[1], jnp.where(
                arow == 2, c[2], jnp.where(arow < 6, dq_row, 0.0)))).astype(BF16)
            kaug_ref[h] = jnp.where(col < 3, dk_col, jnp.where(col == 3, -c[0], jnp.where(
                col == 4, -c[1], jnp.where(col == 5, -c[2], 0.0)))).astype(BF16)
            qaug_ref[2 * h, V_DIM:, :] = q_aug
            qaug_ref[2 * h + 1, V_DIM:, :] = q_aug

    row = lax.broadcasted_iota(jnp.int32, (V_DIM, tq), 0)
    for h in range(N_HEADS):
        qt = qt_ref[0, h * V_DIM:(h + 1) * V_DIM, :]
        zero = jnp.zeros_like(qt)
        qaug_ref[2 * h, 0:V_DIM, :] = jnp.where(row < HEAD_DIM, qt, zero)
        qaug_ref[2 * h + 1, 0:V_DIM, :] = jnp.where(row >= HEAD_DIM, qt, zero)

    ones_rows = jnp.ones((ONES_ROWS, tk), BF16)

    def issue_scores(j, h):
        off2 = pl.multiple_of(j * (tk // 2), tk // 2)
        kt = _bf16(k_ref[0, pl.ds(off2, tk // 2), h * V_DIM:(h + 1) * V_DIM])
        lhs = jnp.concatenate([kt, kaug_ref[h]], axis=1)
        for c in range(2):
            sc_ref[h % LA, c] = jnp.dot(lhs, qaug_ref[2 * h + c], preferred_element_type=F32)

    def softmax_pv(j, h, future):
        off = pl.multiple_of(j * tk, tk)
        vt = _bf16(vt_ref[0, h * V_DIM // 2:(h + 1) * V_DIM // 2, pl.ds(off, tk)])
        vta = jnp.concatenate([vt, ones_rows], axis=0)
        tile_bias = -log2_slopes[h] * ((qi - j) * tk).astype(F32)
        for c in range(2):
            idx = 2 * h + c
            sc = sc_ref[h % LA, c]
            if future is not None:
                sc = jnp.where(future, -jnp.inf, sc)
            m_old = m_ref[idx]
            m_new = jnp.maximum(m_old, jnp.max(sc, axis=0, keepdims=True) + tile_bias)
            alpha = jnp.exp2(m_old - m_new)
            p = jnp.exp2(sc - (m_new - tile_bias)).astype(BF16)
            acc_ref[idx] = alpha * acc_ref[idx] + jnp.dot(vta, p, preferred_element_type=F32)
            m_ref[idx] = m_new

    for h in range(LA):
        issue_scores(0, h)
    m_ref[...] = jnp.full(m_ref.shape, -jnp.inf, F32)
    acc_ref[...] = jnp.zeros(acc_ref.shape, F32)

    def unmasked_tile(j):
        for h in range(N_HEADS):
            softmax_pv(j, h, None)
            if h + LA < N_HEADS:
                issue_scores(j, h + LA)
            else:
                issue_scores(j + 1, h + LA - N_HEADS)

    def tile_pair(i, carry):
        unmasked_tile(2 * i)
        unmasked_tile(2 * i + 1)
        return carry

    def odd_tile(i, carry):
        unmasked_tile(qi - 1)
        return carry

    lax.fori_loop(0, qi // 2, tile_pair, 0)
    lax.fori_loop(0, qi % 2, odd_tile, 0)

    future = (lax.broadcasted_iota(jnp.int32, (tk, tq), 0)
              > lax.broadcasted_iota(jnp.int32, (tk, tq), 1))
    for h in range(N_HEADS):
        softmax_pv(qi, h, future)
        if h + LA < N_HEADS:
            issue_scores(qi, h + LA)

    lv = lam_ref[0]
    lam = (jnp.exp(jnp.sum(lv[0:1] * lv[1:2], keepdims=True))
           - jnp.exp(jnp.sum(lv[2:3] * lv[3:4], keepdims=True)) + lam_init)
    for h in range(N_HEADS):
        a1 = acc_ref[2 * h]
        a2 = acc_ref[2 * h + 1]
        o = (a1[0:V_DIM] / a1[V_DIM:V_DIM + 1]) - lam * (a2[0:V_DIM] / a2[V_DIM:V_DIM + 1])
        o = o * lax.rsqrt(jnp.mean(o * o, axis=0, keepdims=True) + SUBLN_EPS) * sg_ref[...]
        ot_ref[0, h * V_DIM:(h + 1) * V_DIM, :] = (o * (1.0 - lam_init)).astype(BF16)


def _diff_attention(qt, k, vt, lam_vecs, subln_g, slopes, lam_init, *, tq=256):
    B, D, S = qt.shape
    assert N_HEADS % SCORE_LOOKAHEAD == 0 and tq <= 256
    kern = functools.partial(_attn_kernel, tq=tq, lam_init=lam_init, slopes=slopes)
    return pl.pallas_call(
        kern,
        out_shape=jax.ShapeDtypeStruct((B, D, S), BF16),
        grid=(B, S // tq),
        in_specs=[
            pl.BlockSpec((1, D, tq), lambda b, i: (b, 0, i)),
            pl.BlockSpec((1, S // 2, D), lambda b, i: (b, 0, 0)),
            pl.BlockSpec((1, D // 2, S), lambda b, i: (b, 0, 0)),
            _const_spec((1, 4, HEAD_DIM)),
            _const_spec((V_DIM, 1)),
        ],
        out_specs=pl.BlockSpec((1, D, tq), lambda b, i: (b, 0, i)),
        scratch_shapes=[
            pltpu.VMEM((2 * N_HEADS, 2 * V_DIM, tq), BF16),
            pltpu.VMEM((N_HEADS, tq, V_DIM), BF16),
            pltpu.VMEM((SCORE_LOOKAHEAD, 2, tq, tq), F32),
            pltpu.VMEM((2 * N_HEADS, 1, tq), F32),
            pltpu.VMEM((2 * N_HEADS, V_DIM + ONES_ROWS, tq), F32),
        ],
        compiler_params=pltpu.CompilerParams(
            dimension_semantics=("arbitrary", "arbitrary"), vmem_limit_bytes=VMEM_LIMIT),
        name="diff_attention",
    )(qt, k, vt, lam_vecs.reshape(1, 4, HEAD_DIM), subln_g.reshape(V_DIM, 1))


def _out_kernel(at_ref, w_ref, x_ref, o_ref):
    for r0 in range(0, x_ref.shape[1], PROJ_ROWS):
        rows = slice(r0, r0 + PROJ_ROWS)
        y = lax.dot_general(at_ref[0, :, rows], w_ref[...], (((0,), (0,)), ((), ())),
                            preferred_element_type=F32)
        o_ref[0, rows, :] = x_ref[0, rows, :] + y


def _out_proj(at, w_o, x, *, tm=1024):
    B, D, S = at.shape
    return pl.pallas_call(
        _out_kernel,
        out_shape=jax.ShapeDtypeStruct((B, S, D), F32),
        grid=(B, S // tm),
        in_specs=[
            pl.BlockSpec((1, D, tm), lambda b, s: (b, 0, s)),
            _const_spec((D, D)),
            pl.BlockSpec((1, tm, D), lambda b, s: (b, s, 0)),
        ],
        out_specs=pl.BlockSpec((1, tm, D), lambda b, s: (b, s, 0)),
        compiler_params=pltpu.CompilerParams(
            dimension_semantics=("parallel", "parallel"), vmem_limit_bytes=VMEM_LIMIT),
        name="attn_out_proj",
    )(at, w_o.astype(BF16), x)


def _attention_layer(x, k_v, g, w_q, lam_vecs, subln_g, w_o, slopes, lam_init):
    k, vt = k_v
    qt = _q_proj(x, g, w_q)
    at = _diff_attention(qt, k, vt, lam_vecs, subln_g, slopes, lam_init)
    return _out_proj(at, w_o, x)


def kernel(x, a_norm, a_w_in, a_b_in, a_conv_w, a_conv_b, a_gate_w, a_gate_b, a_lambda, a_w_out,
           kv_norm, w_kv, b_norm, b_w_q, b_lam, b_subln, b_w_o,
           mlp_norm, mlp_w1, mlp_w2, final_norm):
    B, S, D = x.shape
    depth = mlp_w1.shape[0]
    n_a = a_w_in.shape[0]
    slopes = tuple(2.0 ** (-8.0 * (h + 1) / N_HEADS) for h in range(N_HEADS))
    k_v = None
    for l in range(depth):
        if l < n_a:
            x = _recurrent_layer(x, a_norm[l], a_w_in[l], a_b_in[l], a_conv_w[l], a_conv_b[l],
                                 a_gate_w[l], a_gate_b[l], a_lambda[l], a_w_out[l])
        else:
            j = l - n_a
            lam_init = 0.8 - 0.6 * math.exp(-0.3 * l)
            x = _attention_layer(x, k_v, b_norm[j], b_w_q[j], b_lam[j], b_subln[j], b_w_o[j],
                                 slopes, lam_init)
        x = _mlp_layer(x.reshape(B * S, D), mlp_norm[l], mlp_w1[l], mlp_w2[l], final_norm,
                       final_norm=(l == depth - 1)).reshape(B, S, D)
        if l == n_a - 1:
            k_v = _kv_proj(x, kv_norm, w_kv)
    return x
```

```python
import functools
import math

import jax
import jax.numpy as jnp
import numpy as np
from jax import lax
from jax.experimental import pallas as pl
from jax.experimental.pallas import tpu as pltpu

D_MODEL = 1024
N_HEADS = 8
HEAD_DIM = 64
V_DIM = 128
LRU_BLOCKS = 8
LRU_BLOCK_W = 128
CONV_W = 4
LRU_C = 8.0
NORM_EPS = 1e-6
SUBLN_EPS = 1e-5

SUBLANES = 8
ONES_ROWS = 16
SCORE_LOOKAHEAD = 4
LOG2E = math.log2(math.e)
REC_COLS = 256
REC_GATE_LOOKAHEAD = 2
PROJ_ROWS = 256
VMEM_LIMIT = 56 * 1024 * 1024

BF16 = jnp.bfloat16
F32 = jnp.float32


def _rmsnorm(x, g, eps):
    return x * lax.rsqrt(jnp.mean(x * x, axis=-1, keepdims=True) + eps) * g


def _bf16(packed):
    return pltpu.bitcast(packed, BF16)


def _const_spec(shape):
    nd = len(shape)
    return pl.BlockSpec(shape, lambda *_: (0,) * nd, pipeline_mode=pl.Buffered(1))


def _gelu_tanh(x):
    c1 = math.sqrt(2.0 / math.pi)
    return x * (0.5 + 0.5 * jnp.tanh(x * (c1 + (c1 * 0.044715) * (x * x))))


def _rec_kernel(x_ref, g_ref, wx_ref, wy_ref, bx_ref, by_ref, cw_ref, cb_ref, gw_ref, gb_ref,
                lam_ref, wo_ref, perm_ref, unperm_ref, xres_ref, o_ref,
                hn_s, xb_s, yb_s, xpad_ref, tail_ref, h_ref, *, ts, tiles_per_seq):
    f = pl.program_id(0)
    W = D_MODEL
    G = ts // SUBLANES
    HALO = (CONV_W - 1) * SUBLANES
    NB = W // REC_COLS
    starts_sequence = lax.rem(f - 2, tiles_per_seq) == 0

    @pl.when(f == 0)
    def _():
        hn_s[...] = jnp.zeros_like(hn_s)
        xb_s[...] = jnp.zeros_like(xb_s)
        yb_s[...] = jnp.zeros_like(yb_s)
        tail_ref[...] = jnp.zeros_like(tail_ref)
        h_ref[...] = jnp.zeros_like(h_ref)

    row8 = lax.broadcasted_iota(jnp.int32, (SUBLANES, REC_COLS), 0)

    def norm_permute():
        hn = _rmsnorm(x_ref[0], g_ref[...], NORM_EPS).astype(BF16)
        hn_s[...] = jnp.dot(perm_ref[...], hn, preferred_element_type=F32).astype(BF16)

    def in_proj(cb):
        cs = slice(cb * REC_COLS, (cb + 1) * REC_COLS)
        hn = hn_s[...]
        xb_s[:, cs] = jnp.dot(hn, wx_ref[:, cs], preferred_element_type=F32) + bx_ref[:, cs]
        yb_s[:, cs] = jnp.dot(hn, wy_ref[:, cs], preferred_element_type=F32) + by_ref[:, cs]

    def load_proj(cb):
        cs = slice(cb * REC_COLS, (cb + 1) * REC_COLS)
        return xb_s[:, cs], yb_s[:, cs]

    def conv_gates(cb, xb, yb):
        cs = slice(cb * REC_COLS, (cb + 1) * REC_COLS)
        for k in range(CONV_W - 1):
            cur = xb[ts - HALO + k * SUBLANES:ts - HALO + (k + 1) * SUBLANES, :]
            prev = jnp.where(starts_sequence, 0.0, tail_ref[k * SUBLANES:(k + 1) * SUBLANES, cs])
            xpad_ref[k * SUBLANES:(k + 1) * SUBLANES, cs] = jnp.where(
                row8 == 0, pltpu.roll(prev, 1, axis=0), pltpu.roll(cur, 1, axis=0))
        tail_ref[:, cs] = xb[ts - HALO:, :]
        xpad_ref[HALO:, cs] = xb
        xc = xb * cw_ref[CONV_W - 1:CONV_W, cs] + cb_ref[:, cs]
        for j in range(CONV_W - 1):
            xc = xc + xpad_ref[j * SUBLANES:j * SUBLANES + ts, cs] * cw_ref[j:j + 1, cs]
        xcb = xc.astype(BF16)
        r_parts, i_parts = [], []
        for n in range(REC_COLS // LRU_BLOCK_W):
            gn = jnp.dot(xcb[:, n * LRU_BLOCK_W:(n + 1) * LRU_BLOCK_W],
                         gw_ref[cb * (REC_COLS // LRU_BLOCK_W) + n], preferred_element_type=F32)
            r_parts.append(gn[:, :LRU_BLOCK_W])
            i_parts.append(gn[:, LRU_BLOCK_W:])
        return xc, jnp.concatenate(r_parts, axis=1), jnp.concatenate(i_parts, axis=1), yb

    def recurrence(cb, xc, g_r, g_i, yb):
        cs = slice(cb * REC_COLS, (cb + 1) * REC_COLS)
        r = jax.nn.sigmoid(g_r + gb_ref[0:1, cs])
        ig = jax.nn.sigmoid(g_i + gb_ref[1:2, cs])
        z = -lam_ref[:, cs]
        softplus = jnp.maximum(z, 0.0) + jnp.log1p(jnp.exp(-jnp.abs(z)))
        log_a = (-LRU_C * softplus) * r
        a = jnp.exp(log_a)
        bx = jnp.sqrt(-jnp.tanh(log_a) * (a * a + 1.0)) * (ig * xc)

        h_loc = jnp.zeros((SUBLANES, REC_COLS), F32)
        prod = jnp.ones((SUBLANES, REC_COLS), F32)
        h_steps, p_steps = [], []
        for g in range(G):
            a_g = a[g * SUBLANES:(g + 1) * SUBLANES, :]
            h_loc = a_g * h_loc + bx[g * SUBLANES:(g + 1) * SUBLANES, :]
            prod = a_g * prod
            h_steps.append(h_loc)
            p_steps.append(prod)

        t_cum, f_cum = prod, h_loc
        for d in (1, 2, 4):
            keep = row8 >= d
            f_cum = jnp.where(keep, t_cum * pltpu.roll(f_cum, d, axis=0) + f_cum, f_cum)
            t_cum = jnp.where(keep, t_cum * pltpu.roll(t_cum, d, axis=0), t_cum)
        h_prev = jnp.where(starts_sequence, 0.0, h_ref[SUBLANES - 1:SUBLANES, cs])
        h_end = t_cum * h_prev + f_cum
        h_in = jnp.where(row8 == 0, h_prev, pltpu.roll(h_end, 1, axis=0))
        h_ref[:, cs] = h_end
        hs = jnp.concatenate([h_steps[g] + p_steps[g] * h_in for g in range(G)], axis=0)
        return (hs * _gelu_tanh(yb)).astype(BF16)

    def out_proj(cb, y):
        y = jnp.dot(unperm_ref[...], y, preferred_element_type=F32).astype(BF16)
        return jnp.dot(y, wo_ref[cb * REC_COLS:(cb + 1) * REC_COLS, :], preferred_element_type=F32)

    gated = {cb: conv_gates(cb, *load_proj(cb)) for cb in range(REC_GATE_LOOKAHEAD)}
    for cb in range(REC_GATE_LOOKAHEAD):
        in_proj(cb)
    out = xres_ref[0]
    for cb in range(NB):
        nxt = cb + REC_GATE_LOOKAHEAD
        if nxt < NB:
            loaded = load_proj(nxt)
            gated[nxt] = conv_gates(nxt, *loaded)
        y = recurrence(cb, *gated.pop(cb))
        out = out + out_proj(cb, y)
        if nxt < NB:
            in_proj(nxt)
        if nxt == NB:
            norm_permute()
    o_ref[0] = out


def _chunk_permutation(ts):
    steps = ts // SUBLANES
    p = np.zeros((ts, ts), np.float32)
    for c in range(SUBLANES):
        for s in range(steps):
            p[s * SUBLANES + c, c * steps + s] = 1.0
    return p


def _recurrent_layer(x, g, w_in, b_in, conv_w, conv_b, gate_w, gate_b, lam, w_out, *, ts=256):
    B, S, D = x.shape
    W = D_MODEL
    n_seq = S // ts
    n_tiles = B * n_seq
    wx = w_in[:, :W].astype(BF16)
    wy = w_in[:, W:].astype(BF16)
    bxv = b_in[:W].reshape(1, W)
    byv = b_in[W:].reshape(1, W)
    gw = jnp.concatenate([gate_w[0], gate_w[1]], axis=-1).astype(BF16)
    perm = _chunk_permutation(ts)
    kern = functools.partial(_rec_kernel, ts=ts, tiles_per_seq=n_seq)

    def in_tile(f):
        t = jnp.minimum(f, n_tiles - 1)
        return (t // n_seq, t % n_seq, 0)

    def out_tile(f):
        t = jnp.maximum(f - 2, 0)
        return (t // n_seq, t % n_seq, 0)

    return pl.pallas_call(
        kern,
        out_shape=jax.ShapeDtypeStruct((B, S, D), F32),
        grid=(n_tiles + 2,),
        in_specs=[
            pl.BlockSpec((1, ts, D), in_tile),
            _const_spec((1, D)),
            _const_spec((D, W)), _const_spec((D, W)),
            _const_spec((1, W)), _const_spec((1, W)),
            _const_spec((CONV_W, W)), _const_spec((1, W)),
            _const_spec((LRU_BLOCKS, LRU_BLOCK_W, 2 * LRU_BLOCK_W)),
            _const_spec((2, W)),
            _const_spec((1, W)),
            _const_spec((W, D)),
            _const_spec((ts, ts)), _const_spec((ts, ts)),
            pl.BlockSpec((1, ts, D), out_tile),
        ],
        out_specs=pl.BlockSpec((1, ts, D), out_tile),
        scratch_shapes=[
            pltpu.VMEM((ts, D), BF16),
            pltpu.VMEM((ts, W), F32),
            pltpu.VMEM((ts, W), F32),
            pltpu.VMEM((ts + (CONV_W - 1) * SUBLANES, W), F32),
            pltpu.VMEM(((CONV_W - 1) * SUBLANES, W), F32),
            pltpu.VMEM((SUBLANES, W), F32),
        ],
        compiler_params=pltpu.CompilerParams(
            dimension_semantics=("arbitrary",), vmem_limit_bytes=VMEM_LIMIT),
        name="rglru_layer",
    )(x, g.reshape(1, D), wx, wy, bxv, byv, conv_w, conv_b.reshape(1, W), gw, gate_b,
      lam.reshape(1, W), w_out.astype(BF16), jnp.asarray(perm, BF16), jnp.asarray(perm.T, BF16), x)


def _mlp_body(x, g_ref, w1_ref, w2_ref, fg_ref, o_ref, final_norm):
    hn = _rmsnorm(x, g_ref[...], NORM_EPS).astype(BF16)
    u = jnp.maximum(jnp.dot(hn, w1_ref[...], preferred_element_type=F32), 0.0)
    u = (u * u).astype(BF16)
    y = x + jnp.dot(u, w2_ref[...], preferred_element_type=F32)
    if final_norm:
        y = _rmsnorm(y, fg_ref[...], NORM_EPS)
    o_ref[0] = y


def _mlp_kernel(x_ref, g_ref, w1_ref, w2_ref, fg_ref, o_ref, *, final_norm):
    _mlp_body(x_ref[0], g_ref, w1_ref, w2_ref, fg_ref, o_ref, final_norm)


def _attn_out_mlp_kernel(at_ref, wo_ref, x_ref, g_ref, w1_ref, w2_ref, fg_ref, o_ref, *, final_norm):
    x = x_ref[0] + lax.dot_general(at_ref[0], wo_ref[...], (((0,), (0,)), ((), ())),
                                   preferred_element_type=F32)
    _mlp_body(x, g_ref, w1_ref, w2_ref, fg_ref, o_ref, final_norm)


def _mlp_layer(x, g, w1, w2, final_g, *, final_norm, attn=None, tm=512):
    B, S, D = x.shape
    F = w1.shape[1]
    row_spec = pl.BlockSpec((1, tm, D), lambda b, s: (b, s, 0))
    specs = [row_spec, _const_spec((1, D)), _const_spec((D, F)), _const_spec((F, D)),
             _const_spec((1, D))]
    args = [x, g.reshape(1, D), w1.astype(BF16), w2.astype(BF16), final_g.reshape(1, D)]
    kern = _mlp_kernel
    if attn is not None:
        at, w_o = attn
        specs = [pl.BlockSpec((1, D, tm), lambda b, s: (b, 0, s)), _const_spec((D, D))] + specs
        args = [at, w_o.astype(BF16)] + args
        kern = _attn_out_mlp_kernel
    return pl.pallas_call(
        functools.partial(kern, final_norm=final_norm),
        out_shape=jax.ShapeDtypeStruct((B, S, D), F32),
        grid=(B, S // tm),
        in_specs=specs,
        out_specs=row_spec,
        compiler_params=pltpu.CompilerParams(
            dimension_semantics=("parallel", "parallel"), vmem_limit_bytes=VMEM_LIMIT),
        name="mlp_layer",
    )(*args)


def _kv_kernel(x_ref, g_ref, wk_ref, wv_ref, k_ref, vt_ref):
    for r0 in range(0, x_ref.shape[1], PROJ_ROWS):
        rows = slice(r0, r0 + PROJ_ROWS)
        hn = _rmsnorm(x_ref[0, rows, :], g_ref[...], NORM_EPS).astype(BF16)
        k = jnp.dot(hn, wk_ref[...], preferred_element_type=F32).astype(BF16)
        k_ref[0, r0 // 2:(r0 + PROJ_ROWS) // 2, :] = pltpu.bitcast(k, jnp.uint32)
        v = jnp.dot(hn, wv_ref[...], preferred_element_type=F32)
        vt_ref[0, :, rows] = pltpu.bitcast(v.T.astype(BF16), jnp.uint32)


def _kv_proj(x, g, w_kv, *, tm=1024):
    B, S, D = x.shape
    return pl.pallas_call(
        _kv_kernel,
        out_shape=(jax.ShapeDtypeStruct((B, S // 2, D), jnp.uint32),
                   jax.ShapeDtypeStruct((B, D // 2, S), jnp.uint32)),
        grid=(B, S // tm),
        in_specs=[
            pl.BlockSpec((1, tm, D), lambda b, s: (b, s, 0)),
            _const_spec((1, D)), _const_spec((D, D)), _const_spec((D, D)),
        ],
        out_specs=(pl.BlockSpec((1, tm // 2, D), lambda b, s: (b, s, 0)),
                   pl.BlockSpec((1, D // 2, tm), lambda b, s: (b, 0, s))),
        compiler_params=pltpu.CompilerParams(
            dimension_semantics=("parallel", "parallel"), vmem_limit_bytes=VMEM_LIMIT),
        name="kv_proj",
    )(x, g.reshape(1, D), w_kv[:, :D].astype(BF16), w_kv[:, D:].astype(BF16))


def _q_kernel(x_ref, g_ref, wq_ref, qt_ref, *, scale):
    for r0 in range(0, x_ref.shape[1], PROJ_ROWS):
        rows = slice(r0, r0 + PROJ_ROWS)
        hn = _rmsnorm(x_ref[0, rows, :], g_ref[...], NORM_EPS).astype(BF16)
        q = jnp.dot(hn, wq_ref[...], preferred_element_type=F32) * scale
        qt_ref[0, :, rows] = q.T.astype(BF16)


def _q_proj(x, g, w_q, *, tm=1024):
    B, S, D = x.shape
    kern = functools.partial(_q_kernel, scale=LOG2E * HEAD_DIM ** -0.5)
    return pl.pallas_call(
        kern,
        out_shape=jax.ShapeDtypeStruct((B, D, S), BF16),
        grid=(B, S // tm),
        in_specs=[
            pl.BlockSpec((1, tm, D), lambda b, s: (b, s, 0)),
            _const_spec((1, D)), _const_spec((D, D)),
        ],
        out_specs=pl.BlockSpec((1, D, tm), lambda b, s: (b, 0, s)),
        compiler_params=pltpu.CompilerParams(
            dimension_semantics=("parallel", "parallel"), vmem_limit_bytes=VMEM_LIMIT),
        name="q_proj",
    )(x, g.reshape(1, D), w_q.astype(BF16))


def _bf16_terms(c, n=3):
    terms, rest = [], np.float32(c)
    for _ in range(n):
        t = np.float32(np.asarray(rest, dtype=BF16))
        terms.append(float(t))
        rest = np.float32(rest - t)
    return terms


def _attn_kernel(qt_ref, k_ref, vt_ref, lam_ref, sg_ref, ot_ref,
                 qaug_ref, kaug_ref, sc_ref, m_ref, acc_ref, *, tq, lam_init, slopes):
    qi = pl.program_id(1)
    tk = tq
    LA = SCORE_LOOKAHEAD
    log2_slopes = [s * LOG2E for s in slopes]

    @pl.when((pl.program_id(0) == 0) & (qi == 0))
    def _():
        arow = lax.broadcasted_iota(jnp.int32, (V_DIM, tq), 0)
        dq_row = lax.broadcasted_iota(jnp.int32, (V_DIM, tq), 1).astype(F32)
        col = lax.broadcasted_iota(jnp.int32, (tk, V_DIM), 1)
        dk_col = lax.broadcasted_iota(jnp.int32, (tk, V_DIM), 0).astype(F32)
        for h in range(N_HEADS):
            c = _bf16_terms(log2_slopes[h])
            q_aug = jnp.where(arow == 0, c[0], jnp.where(arow == 1, c[1], jnp.where(
                arow == 2, c[2], jnp.where(arow < 6, dq_row, 0.0)))).astype(BF16)
            kaug_ref[h] = jnp.where(col < 3, dk_col, jnp.where(col == 3, -c[0], jnp.where(
                col == 4, -c[1], jnp.where(col == 5, -c[2], 0.0)))).astype(BF16)
            qaug_ref[2 * h, V_DIM:, :] = q_aug
            qaug_ref[2 * h + 1, V_DIM:, :] = q_aug

    row = lax.broadcasted_iota(jnp.int32, (V_DIM, tq), 0)
    for h in range(N_HEADS):
        qt = qt_ref[0, h * V_DIM:(h + 1) * V_DIM, :]
        zero = jnp.zeros_like(qt)
        qaug_ref[2 * h, 0:V_DIM, :] = jnp.where(row < HEAD_DIM, qt, zero)
        qaug_ref[2 * h + 1, 0:V_DIM, :] = jnp.where(row >= HEAD_DIM, qt, zero)

    ones_rows = jnp.ones((ONES_ROWS, tk), BF16)

    def issue_scores(j, h):
        off2 = pl.multiple_of(j * (tk // 2), tk // 2)
        kt = _bf16(k_ref[0, pl.ds(off2, tk // 2), h * V_DIM:(h + 1) * V_DIM])
        lhs = jnp.concatenate([kt, kaug_ref[h]], axis=1)
        for c in range(2):
            sc_ref[h, c] = jnp.dot(lhs, qaug_ref[2 * h + c], preferred_element_type=F32)

    def softmax_pv(j, h, future):
        off = pl.multiple_of(j * tk, tk)
        vt = _bf16(vt_ref[0, h * V_DIM // 2:(h + 1) * V_DIM // 2, pl.ds(off, tk)])
        vta = jnp.concatenate([vt, ones_rows], axis=0)
        tile_bias = -log2_slopes[h] * ((qi - j) * tk).astype(F32)
        for c in range(2):
            idx = 2 * h + c
            sc = sc_ref[h, c]
            if future is not None:
                sc = jnp.where(future, -jnp.inf, sc)
            m_old = m_ref[idx]
            m_new = jnp.maximum(m_old, jnp.max(sc, axis=0, keepdims=True) + tile_bias)
            alpha = jnp.exp2(m_old - m_new)
            p = jnp.exp2(sc - (m_new - tile_bias)).astype(BF16)
            acc_ref[idx] = alpha * acc_ref[idx] + jnp.dot(vta, p, preferred_element_type=F32)
            m_ref[idx] = m_new

    for h in range(LA):
        issue_scores(0, h)
    m_ref[...] = jnp.full(m_ref.shape, -jnp.inf, F32)
    acc_ref[...] = jnp.zeros(acc_ref.shape, F32)

    def unmasked_tile(j):
        for h in range(N_HEADS):
            softmax_pv(j, h, None)
            if h + LA < N_HEADS:
                issue_scores(j, h + LA)
            else:
                issue_scores(j + 1, h + LA - N_HEADS)

    def tile_pair(i, carry):
        unmasked_tile(2 * i)
        unmasked_tile(2 * i + 1)
        return carry

    def odd_tile(i, carry):
        unmasked_tile(qi - 1)
        return carry

    lax.fori_loop(0, qi // 2, tile_pair, 0)
    lax.fori_loop(0, qi % 2, odd_tile, 0)

    future = (lax.broadcasted_iota(jnp.int32, (tk, tq), 0)
              > lax.broadcasted_iota(jnp.int32, (tk, tq), 1))
    for h in range(N_HEADS):
        softmax_pv(qi, h, future)
        if h + LA < N_HEADS:
            issue_scores(qi, h + LA)

    lv = lam_ref[0]
    lam = (jnp.exp(jnp.sum(lv[0:1] * lv[1:2], keepdims=True))
           - jnp.exp(jnp.sum(lv[2:3] * lv[3:4], keepdims=True)) + lam_init)
    for h in range(N_HEADS):
        a1 = acc_ref[2 * h]
        a2 = acc_ref[2 * h + 1]
        o = (a1[0:V_DIM] / a1[V_DIM:V_DIM + 1]) - lam * (a2[0:V_DIM] / a2[V_DIM:V_DIM + 1])
        o = o * lax.rsqrt(jnp.mean(o * o, axis=0, keepdims=True) + SUBLN_EPS) * sg_ref[...]
        ot_ref[0, h * V_DIM:(h + 1) * V_DIM, :] = (o * (1.0 - lam_init)).astype(BF16)


def _diff_attention(qt, k, vt, lam_vecs, subln_g, slopes, lam_init, *, tq=256):
    B, D, S = qt.shape
    assert N_HEADS % SCORE_LOOKAHEAD == 0 and tq <= 256
    kern = functools.partial(_attn_kernel, tq=tq, lam_init=lam_init, slopes=slopes)
    return pl.pallas_call(
        kern,
        out_shape=jax.ShapeDtypeStruct((B, D, S), BF16),
        grid=(B, S // tq),
        in_specs=[
            pl.BlockSpec((1, D, tq), lambda b, i: (b, 0, i)),
            pl.BlockSpec((1, S // 2, D), lambda b, i: (b, 0, 0)),
            pl.BlockSpec((1, D // 2, S), lambda b, i: (b, 0, 0)),
            _const_spec((1, 4, HEAD_DIM)),
            _const_spec((V_DIM, 1)),
        ],
        out_specs=pl.BlockSpec((1, D, tq), lambda b, i: (b, 0, i)),
        scratch_shapes=[
            pltpu.VMEM((2 * N_HEADS, 2 * V_DIM, tq), BF16),
            pltpu.VMEM((N_HEADS, tq, V_DIM), BF16),
            pltpu.VMEM((N_HEADS, 2, tq, tq), F32),
            pltpu.VMEM((2 * N_HEADS, 1, tq), F32),
            pltpu.VMEM((2 * N_HEADS, V_DIM + ONES_ROWS, tq), F32),
        ],
        compiler_params=pltpu.CompilerParams(
            dimension_semantics=("arbitrary", "arbitrary"), vmem_limit_bytes=VMEM_LIMIT),
        name="diff_attention",
    )(qt, k, vt, lam_vecs.reshape(1, 4, HEAD_DIM), subln_g.reshape(V_DIM, 1))


def _attention_heads(x, k_v, g, w_q, lam_vecs, subln_g, slopes, lam_init):
    k, vt = k_v
    qt = _q_proj(x, g, w_q)
    return _diff_attention(qt, k, vt, lam_vecs, subln_g, slopes, lam_init)


def kernel(x, a_norm, a_w_in, a_b_in, a_conv_w, a_conv_b, a_gate_w, a_gate_b, a_lambda, a_w_out,
           kv_norm, w_kv, b_norm, b_w_q, b_lam, b_subln, b_w_o,
           mlp_norm, mlp_w1, mlp_w2, final_norm):
    depth = mlp_w1.shape[0]
    n_a = a_w_in.shape[0]
    slopes = tuple(2.0 ** (-8.0 * (h + 1) / N_HEADS) for h in range(N_HEADS))
    k_v = None
    for l in range(depth):
        attn = None
        if l < n_a:
            x = _recurrent_layer(x, a_norm[l], a_w_in[l], a_b_in[l], a_conv_w[l], a_conv_b[l],
                                 a_gate_w[l], a_gate_b[l], a_lambda[l], a_w_out[l])
        else:
            j = l - n_a
            lam_init = 0.8 - 0.6 * math.exp(-0.3 * l)
            heads = _attention_heads(x, k_v, b_norm[j], b_w_q[j], b_lam[j], b_subln[j], slopes, lam_init)
            attn = (heads, b_w_o[j])
        x = _mlp_layer(x, mlp_norm[l], mlp_w1[l], mlp_w2[l], final_norm,
                       final_norm=(l == depth - 1), attn=attn)
        if l == n_a - 1:
            k_v = _kv_proj(x, kv_norm, w_kv)
    return x
```

```python
import functools
import math

import jax
import jax.numpy as jnp
import numpy as np
from jax import lax
from jax.experimental import pallas as pl
from jax.experimental.pallas import tpu as pltpu

D_MODEL = 1024
N_HEADS = 8
HEAD_DIM = 64
V_DIM = 128
LRU_BLOCKS = 8
LRU_BLOCK_W = 128
CONV_W = 4
LRU_C = 8.0
NORM_EPS = 1e-6
SUBLN_EPS = 1e-5

SUBLANES = 8
ONES_ROWS = 16
SCORE_LOOKAHEAD = 4
LOG2E = math.log2(math.e)
REC_COLS = 256
REC_GATE_LOOKAHEAD = 2
PROJ_ROWS = 256
VMEM_LIMIT = 56 * 1024 * 1024

BF16 = jnp.bfloat16
F32 = jnp.float32


def _rmsnorm(x, g, eps):
    return x * lax.rsqrt(jnp.mean(x * x, axis=-1, keepdims=True) + eps) * g


def _bf16(packed):
    return pltpu.bitcast(packed, BF16)


def _const_spec(shape):
    nd = len(shape)
    return pl.BlockSpec(shape, lambda *_: (0,) * nd, pipeline_mode=pl.Buffered(1))


def _gelu_tanh(x):
    c1 = math.sqrt(2.0 / math.pi)
    return x * (0.5 + 0.5 * jnp.tanh(x * (c1 + (c1 * 0.044715) * (x * x))))


def _rec_mlp_kernel(x_ref, g_ref, wx_ref, wy_ref, bx_ref, by_ref, cw_ref, cb_ref, gw_ref, gb_ref,
                    lam_ref, wo_ref, perm_ref, unperm_ref, xres_ref, mg_ref, w1_ref, w2_ref, o_ref,
                    hn_s, xb_s, yb_s, xmid_s, xpad_ref, tail_ref, h_ref, *, ts, tiles_per_seq):
    f = pl.program_id(0)
    W = D_MODEL
    G = ts // SUBLANES
    HALO = (CONV_W - 1) * SUBLANES
    NB = W // REC_COLS
    starts_sequence = lax.rem(f - 2, tiles_per_seq) == 0

    @pl.when(f == 0)
    def _():
        hn_s[...] = jnp.zeros_like(hn_s)
        xb_s[...] = jnp.zeros_like(xb_s)
        yb_s[...] = jnp.zeros_like(yb_s)
        xmid_s[...] = jnp.zeros_like(xmid_s)
        tail_ref[...] = jnp.zeros_like(tail_ref)
        h_ref[...] = jnp.zeros_like(h_ref)

    row8 = lax.broadcasted_iota(jnp.int32, (SUBLANES, REC_COLS), 0)

    def norm_permute():
        hn = _rmsnorm(x_ref[0], g_ref[...], NORM_EPS).astype(BF16)
        hn_s[...] = jnp.dot(perm_ref[...], hn, preferred_element_type=F32).astype(BF16)

    def in_proj(cb):
        cs = slice(cb * REC_COLS, (cb + 1) * REC_COLS)
        hn = hn_s[...]
        xb_s[:, cs] = jnp.dot(hn, wx_ref[:, cs], preferred_element_type=F32) + bx_ref[:, cs]
        yb_s[:, cs] = jnp.dot(hn, wy_ref[:, cs], preferred_element_type=F32) + by_ref[:, cs]

    def load_proj(cb):
        cs = slice(cb * REC_COLS, (cb + 1) * REC_COLS)
        return xb_s[:, cs], yb_s[:, cs]

    def conv_gates(cb, xb, yb):
        cs = slice(cb * REC_COLS, (cb + 1) * REC_COLS)
        for k in range(CONV_W - 1):
            cur = xb[ts - HALO + k * SUBLANES:ts - HALO + (k + 1) * SUBLANES, :]
            prev = jnp.where(starts_sequence, 0.0, tail_ref[k * SUBLANES:(k + 1) * SUBLANES, cs])
            xpad_ref[k * SUBLANES:(k + 1) * SUBLANES, cs] = jnp.where(
                row8 == 0, pltpu.roll(prev, 1, axis=0), pltpu.roll(cur, 1, axis=0))
        tail_ref[:, cs] = xb[ts - HALO:, :]
        xpad_ref[HALO:, cs] = xb
        xc = xb * cw_ref[CONV_W - 1:CONV_W, cs] + cb_ref[:, cs]
        for j in range(CONV_W - 1):
            xc = xc + xpad_ref[j * SUBLANES:j * SUBLANES + ts, cs] * cw_ref[j:j + 1, cs]
        xcb = xc.astype(BF16)
        r_parts, i_parts = [], []
        for n in range(REC_COLS // LRU_BLOCK_W):
            gn = jnp.dot(xcb[:, n * LRU_BLOCK_W:(n + 1) * LRU_BLOCK_W],
                         gw_ref[cb * (REC_COLS // LRU_BLOCK_W) + n], preferred_element_type=F32)
            r_parts.append(gn[:, :LRU_BLOCK_W])
            i_parts.append(gn[:, LRU_BLOCK_W:])
        return xc, jnp.concatenate(r_parts, axis=1), jnp.concatenate(i_parts, axis=1), yb

    def recurrence(cb, xc, g_r, g_i, yb):
        cs = slice(cb * REC_COLS, (cb + 1) * REC_COLS)
        r = jax.nn.sigmoid(g_r + gb_ref[0:1, cs])
        ig = jax.nn.sigmoid(g_i + gb_ref[1:2, cs])
        z = -lam_ref[:, cs]
        softplus = jnp.maximum(z, 0.0) + jnp.log1p(jnp.exp(-jnp.abs(z)))
        log_a = (-LRU_C * softplus) * r
        a = jnp.exp(log_a)
        bx = jnp.sqrt(-jnp.tanh(log_a) * (a * a + 1.0)) * (ig * xc)

        h_loc = jnp.zeros((SUBLANES, REC_COLS), F32)
        prod = jnp.ones((SUBLANES, REC_COLS), F32)
        h_steps, p_steps = [], []
        for g in range(G):
            a_g = a[g * SUBLANES:(g + 1) * SUBLANES, :]
            h_loc = a_g * h_loc + bx[g * SUBLANES:(g + 1) * SUBLANES, :]
            prod = a_g * prod
            h_steps.append(h_loc)
            p_steps.append(prod)

        t_cum, f_cum = prod, h_loc
        for d in (1, 2, 4):
            keep = row8 >= d
            f_cum = jnp.where(keep, t_cum * pltpu.roll(f_cum, d, axis=0) + f_cum, f_cum)
            t_cum = jnp.where(keep, t_cum * pltpu.roll(t_cum, d, axis=0), t_cum)
        h_prev = jnp.where(starts_sequence, 0.0, h_ref[SUBLANES - 1:SUBLANES, cs])
        h_end = t_cum * h_prev + f_cum
        h_in = jnp.where(row8 == 0, h_prev, pltpu.roll(h_end, 1, axis=0))
        h_ref[:, cs] = h_end
        hs = jnp.concatenate([h_steps[g] + p_steps[g] * h_in for g in range(G)], axis=0)
        return (hs * _gelu_tanh(yb)).astype(BF16)

    def out_proj(cb, y):
        y = jnp.dot(unperm_ref[...], y, preferred_element_type=F32).astype(BF16)
        return jnp.dot(y, wo_ref[cb * REC_COLS:(cb + 1) * REC_COLS, :], preferred_element_type=F32)

    F = w1_ref.shape[1]
    fchunk = F // NB
    xm = xmid_s[...]
    hm = _rmsnorm(xm, mg_ref[...], NORM_EPS).astype(BF16)

    def mlp_chunk(fc):
        fs = slice(fc * fchunk, (fc + 1) * fchunk)
        u = jnp.maximum(jnp.dot(hm, w1_ref[:, fs], preferred_element_type=F32), 0.0)
        return jnp.dot((u * u).astype(BF16), w2_ref[fs, :], preferred_element_type=F32)

    loaded = {cb: load_proj(cb) for cb in range(NB)}
    for cb in range(NB):
        in_proj(cb)
    norm_permute()
    gated = {cb: conv_gates(cb, *loaded.pop(cb)) for cb in range(REC_GATE_LOOKAHEAD)}
    out = xres_ref[0]
    mlp_out = xm
    for cb in range(NB):
        nxt = cb + REC_GATE_LOOKAHEAD
        if nxt < NB:
            gated[nxt] = conv_gates(nxt, *loaded.pop(nxt))
        mlp_out = mlp_out + mlp_chunk(cb)
        y = recurrence(cb, *gated.pop(cb))
        out = out + out_proj(cb, y)
    o_ref[0] = mlp_out
    xmid_s[...] = out


def _chunk_permutation(ts):
    steps = ts // SUBLANES
    p = np.zeros((ts, ts), np.float32)
    for c in range(SUBLANES):
        for s in range(steps):
            p[s * SUBLANES + c, c * steps + s] = 1.0
    return p


def _recurrent_mlp_layer(x, g, w_in, b_in, conv_w, conv_b, gate_w, gate_b, lam, w_out,
                         mlp_g, w1, w2, *, ts=256):
    B, S, D = x.shape
    F = w1.shape[1]
    W = D_MODEL
    n_seq = S // ts
    n_tiles = B * n_seq
    wx = w_in[:, :W].astype(BF16)
    wy = w_in[:, W:].astype(BF16)
    bxv = b_in[:W].reshape(1, W)
    byv = b_in[W:].reshape(1, W)
    gw = jnp.concatenate([gate_w[0], gate_w[1]], axis=-1).astype(BF16)
    perm = _chunk_permutation(ts)
    kern = functools.partial(_rec_mlp_kernel, ts=ts, tiles_per_seq=n_seq)

    def in_tile(f):
        t = jnp.minimum(f, n_tiles - 1)
        return (t // n_seq, t % n_seq, 0)

    def res_tile(f):
        t = jnp.clip(f - 2, 0, n_tiles - 1)
        return (t // n_seq, t % n_seq, 0)

    def out_tile(f):
        t = jnp.maximum(f - 3, 0)
        return (t // n_seq, t % n_seq, 0)

    return pl.pallas_call(
        kern,
        out_shape=jax.ShapeDtypeStruct((B, S, D), F32),
        grid=(n_tiles + 3,),
        in_specs=[
            pl.BlockSpec((1, ts, D), in_tile),
            _const_spec((1, D)),
            _const_spec((D, W)), _const_spec((D, W)),
            _const_spec((1, W)), _const_spec((1, W)),
            _const_spec((CONV_W, W)), _const_spec((1, W)),
            _const_spec((LRU_BLOCKS, LRU_BLOCK_W, 2 * LRU_BLOCK_W)),
            _const_spec((2, W)),
            _const_spec((1, W)),
            _const_spec((W, D)),
            _const_spec((ts, ts)), _const_spec((ts, ts)),
            pl.BlockSpec((1, ts, D), res_tile),
            _const_spec((1, D)), _const_spec((D, F)), _const_spec((F, D)),
        ],
        out_specs=pl.BlockSpec((1, ts, D), out_tile),
        scratch_shapes=[
            pltpu.VMEM((ts, D), BF16),
            pltpu.VMEM((ts, W), F32),
            pltpu.VMEM((ts, W), F32),
            pltpu.VMEM((ts, D), F32),
            pltpu.VMEM((ts + (CONV_W - 1) * SUBLANES, W), F32),
            pltpu.VMEM(((CONV_W - 1) * SUBLANES, W), F32),
            pltpu.VMEM((SUBLANES, W), F32),
        ],
        compiler_params=pltpu.CompilerParams(
            dimension_semantics=("arbitrary",), vmem_limit_bytes=VMEM_LIMIT),
        name="rglru_mlp_layer",
    )(x, g.reshape(1, D), wx, wy, bxv, byv, conv_w, conv_b.reshape(1, W), gw, gate_b,
      lam.reshape(1, W), w_out.astype(BF16), jnp.asarray(perm, BF16), jnp.asarray(perm.T, BF16), x,
      mlp_g.reshape(1, D), w1.astype(BF16), w2.astype(BF16))


def _mlp_body(x, g_ref, w1_ref, w2_ref, fg_ref, o_ref, final_norm):
    hn = _rmsnorm(x, g_ref[...], NORM_EPS).astype(BF16)
    u = jnp.maximum(jnp.dot(hn, w1_ref[...], preferred_element_type=F32), 0.0)
    u = (u * u).astype(BF16)
    y = x + jnp.dot(u, w2_ref[...], preferred_element_type=F32)
    if final_norm:
        y = _rmsnorm(y, fg_ref[...], NORM_EPS)
    o_ref[0] = y


def _mlp_kernel(x_ref, g_ref, w1_ref, w2_ref, fg_ref, o_ref, *, final_norm):
    _mlp_body(x_ref[0], g_ref, w1_ref, w2_ref, fg_ref, o_ref, final_norm)


def _attn_out_mlp_kernel(at_ref, wo_ref, x_ref, g_ref, w1_ref, w2_ref, fg_ref, o_ref, *, final_norm):
    x = x_ref[0] + lax.dot_general(at_ref[0], wo_ref[...], (((0,), (0,)), ((), ())),
                                   preferred_element_type=F32)
    _mlp_body(x, g_ref, w1_ref, w2_ref, fg_ref, o_ref, final_norm)


def _mlp_layer(x, g, w1, w2, final_g, *, final_norm, attn=None, tm=512):
    B, S, D = x.shape
    F = w1.shape[1]
    row_spec = pl.BlockSpec((1, tm, D), lambda b, s: (b, s, 0))
    specs = [row_spec, _const_spec((1, D)), _const_spec((D, F)), _const_spec((F, D)),
             _const_spec((1, D))]
    args = [x, g.reshape(1, D), w1.astype(BF16), w2.astype(BF16), final_g.reshape(1, D)]
    kern = _mlp_kernel
    if attn is not None:
        at, w_o = attn
        specs = [pl.BlockSpec((1, D, tm), lambda b, s: (b, 0, s)), _const_spec((D, D))] + specs
        args = [at, w_o.astype(BF16)] + args
        kern = _attn_out_mlp_kernel
    return pl.pallas_call(
        functools.partial(kern, final_norm=final_norm),
        out_shape=jax.ShapeDtypeStruct((B, S, D), F32),
        grid=(B, S // tm),
        in_specs=specs,
        out_specs=row_spec,
        compiler_params=pltpu.CompilerParams(
            dimension_semantics=("parallel", "parallel"), vmem_limit_bytes=VMEM_LIMIT),
        name="mlp_layer",
    )(*args)


def _kv_kernel(x_ref, g_ref, wk_ref, wv_ref, k_ref, vt_ref):
    for r0 in range(0, x_ref.shape[1], PROJ_ROWS):
        rows = slice(r0, r0 + PROJ_ROWS)
        hn = _rmsnorm(x_ref[0, rows, :], g_ref[...], NORM_EPS).astype(BF16)
        k = jnp.dot(hn, wk_ref[...], preferred_element_type=F32).astype(BF16)
        k_ref[0, r0 // 2:(r0 + PROJ_ROWS) // 2, :] = pltpu.bitcast(k, jnp.uint32)
        v = jnp.dot(hn, wv_ref[...], preferred_element_type=F32)
        vt_ref[0, :, rows] = pltpu.bitcast(v.T.astype(BF16), jnp.uint32)


def _kv_proj(x, g, w_kv, *, tm=1024):
    B, S, D = x.shape
    return pl.pallas_call(
        _kv_kernel,
        out_shape=(jax.ShapeDtypeStruct((B, S // 2, D), jnp.uint32),
                   jax.ShapeDtypeStruct((B, D // 2, S), jnp.uint32)),
        grid=(B, S // tm),
        in_specs=[
            pl.BlockSpec((1, tm, D), lambda b, s: (b, s, 0)),
            _const_spec((1, D)), _const_spec((D, D)), _const_spec((D, D)),
        ],
        out_specs=(pl.BlockSpec((1, tm // 2, D), lambda b, s: (b, s, 0)),
                   pl.BlockSpec((1, D // 2, tm), lambda b, s: (b, 0, s))),
        compiler_params=pltpu.CompilerParams(
            dimension_semantics=("parallel", "parallel"), vmem_limit_bytes=VMEM_LIMIT),
        name="kv_proj",
    )(x, g.reshape(1, D), w_kv[:, :D].astype(BF16), w_kv[:, D:].astype(BF16))


def _q_kernel(x_ref, g_ref, wq_ref, qt_ref, *, scale):
    for r0 in range(0, x_ref.shape[1], PROJ_ROWS):
        rows = slice(r0, r0 + PROJ_ROWS)
        hn = _rmsnorm(x_ref[0, rows, :], g_ref[...], NORM_EPS).astype(BF16)
        q = jnp.dot(hn, wq_ref[...], preferred_element_type=F32) * scale
        qt_ref[0, :, rows] = q.T.astype(BF16)


def _q_proj(x, g, w_q, *, tm=1024):
    B, S, D = x.shape
    kern = functools.partial(_q_kernel, scale=LOG2E * HEAD_DIM ** -0.5)
    return pl.pallas_call(
        kern,
        out_shape=jax.ShapeDtypeStruct((B, D, S), BF16),
        grid=(B, S // tm),
        in_specs=[
            pl.BlockSpec((1, tm, D), lambda b, s: (b, s, 0)),
            _const_spec((1, D)), _const_spec((D, D)),
        ],
        out_specs=pl.BlockSpec((1, D, tm), lambda b, s: (b, 0, s)),
        compiler_params=pltpu.CompilerParams(
            dimension_semantics=("parallel", "parallel"), vmem_limit_bytes=VMEM_LIMIT),
        name="q_proj",
    )(x, g.reshape(1, D), w_q.astype(BF16))


def _bf16_terms(c, n=3):
    terms, rest = [], np.float32(c)
    for _ in range(n):
        t = np.float32(np.asarray(rest, dtype=BF16))
        terms.append(float(t))
        rest = np.float32(rest - t)
    return terms


def _attn_kernel(qt_ref, k_ref, vt_ref, lam_ref, sg_ref, ot_ref,
                 qaug_ref, kaug_ref, sc_ref, m_ref, acc_ref, *, tq, lam_init, slopes):
    qi = pl.program_id(1)
    tk = tq
    LA = SCORE_LOOKAHEAD
    log2_slopes = [s * LOG2E for s in slopes]

    @pl.when((pl.program_id(0) == 0) & (qi == 0))
    def _():
        arow = lax.broadcasted_iota(jnp.int32, (V_DIM, tq), 0)
        dq_row = lax.broadcasted_iota(jnp.int32, (V_DIM, tq), 1).astype(F32)
        col = lax.broadcasted_iota(jnp.int32, (tk, V_DIM), 1)
        dk_col = lax.broadcasted_iota(jnp.int32, (tk, V_DIM), 0).astype(F32)
        for h in range(N_HEADS):
            c = _bf16_terms(log2_slopes[h])
            q_aug = jnp.where(arow == 0, c[0], jnp.where(arow == 1, c[1], jnp.where(
                arow == 2, c[2], jnp.where(arow < 6, dq_row, 0.0)))).astype(BF16)
            kaug_ref[h] = jnp.where(col < 3, dk_col, jnp.where(col == 3, -c[0], jnp.where(
                col == 4, -c[1], jnp.where(col == 5, -c[2], 0.0)))).astype(BF16)
            qaug_ref[2 * h, V_DIM:, :] = q_aug
            qaug_ref[2 * h + 1, V_DIM:, :] = q_aug

    row = lax.broadcasted_iota(jnp.int32, (V_DIM, tq), 0)
    for h in range(N_HEADS):
        qt = qt_ref[0, h * V_DIM:(h + 1) * V_DIM, :]
        zero = jnp.zeros_like(qt)
        qaug_ref[2 * h, 0:V_DIM, :] = jnp.where(row < HEAD_DIM, qt, zero)
        qaug_ref[2 * h + 1, 0:V_DIM, :] = jnp.where(row >= HEAD_DIM, qt, zero)

    ones_rows = jnp.ones((ONES_ROWS, tk), BF16)

    def issue_scores(j, h):
        off2 = pl.multiple_of(j * (tk // 2), tk // 2)
        kt = _bf16(k_ref[0, pl.ds(off2, tk // 2), h * V_DIM:(h + 1) * V_DIM])
        lhs = jnp.concatenate([kt, kaug_ref[h]], axis=1)
        for c in range(2):
            sc_ref[h % LA, c] = jnp.dot(lhs, qaug_ref[2 * h + c], preferred_element_type=F32)

    def softmax_pv(j, h, future):
        off = pl.multiple_of(j * tk, tk)
        vt = _bf16(vt_ref[0, h * V_DIM // 2:(h + 1) * V_DIM // 2, pl.ds(off, tk)])
        vta = jnp.concatenate([vt, ones_rows], axis=0)
        tile_bias = -log2_slopes[h] * ((qi - j) * tk).astype(F32)
        for c in range(2):
            idx = 2 * h + c
            sc = sc_ref[h % LA, c]
            if future is not None:
                sc = jnp.where(future, -jnp.inf, sc)
            m_old = m_ref[idx]
            m_new = jnp.maximum(m_old, jnp.max(sc, axis=0, keepdims=True) + tile_bias)
            alpha = jnp.exp2(m_old - m_new)
            p = jnp.exp2(sc - (m_new - tile_bias)).astype(BF16)
            acc_ref[idx] = alpha * acc_ref[idx] + jnp.dot(vta, p, preferred_element_type=F32)
            m_ref[idx] = m_new

    for h in range(LA):
        issue_scores(0, h)
    m_ref[...] = jnp.full(m_ref.shape, -jnp.inf, F32)
    acc_ref[...] = jnp.zeros(acc_ref.shape, F32)

    def unmasked_tile(j):
        for h in range(N_HEADS):
            softmax_pv(j, h, None)
            if h + LA < N_HEADS:
                issue_scores(j, h + LA)
            else:
                issue_scores(j + 1, h + LA - N_HEADS)

    def tile_pair(i, carry):
        unmasked_tile(2 * i)
        unmasked_tile(2 * i + 1)
        return carry

    def odd_tile(i, carry):
        unmasked_tile(qi - 1)
        return carry

    lax.fori_loop(0, qi // 2, tile_pair, 0)
    lax.fori_loop(0, qi % 2, odd_tile, 0)

    future = (lax.broadcasted_iota(jnp.int32, (tk, tq), 0)
              > lax.broadcasted_iota(jnp.int32, (tk, tq), 1))
    for h in range(N_HEADS):
        softmax_pv(qi, h, future)
        if h + LA < N_HEADS:
            issue_scores(qi, h + LA)

    lv = lam_ref[0]
    lam = (jnp.exp(jnp.sum(lv[0:1] * lv[1:2], keepdims=True))
           - jnp.exp(jnp.sum(lv[2:3] * lv[3:4], keepdims=True)) + lam_init)
    for h in range(N_HEADS):
        a1 = acc_ref[2 * h]
        a2 = acc_ref[2 * h + 1]
        o = (a1[0:V_DIM] / a1[V_DIM:V_DIM + 1]) - lam * (a2[0:V_DIM] / a2[V_DIM:V_DIM + 1])
        o = o * lax.rsqrt(jnp.mean(o * o, axis=0, keepdims=True) + SUBLN_EPS) * sg_ref[...]
        ot_ref[0, h * V_DIM:(h + 1) * V_DIM, :] = (o * (1.0 - lam_init)).astype(BF16)


def _diff_attention(qt, k, vt, lam_vecs, subln_g, slopes, lam_init, *, tq=256):
    B, D, S = qt.shape
    assert N_HEADS % SCORE_LOOKAHEAD == 0 and tq <= 256
    kern = functools.partial(_attn_kernel, tq=tq, lam_init=lam_init, slopes=slopes)
    return pl.pallas_call(
        kern,
        out_shape=jax.ShapeDtypeStruct((B, D, S), BF16),
        grid=(B, S // tq),
        in_specs=[
            pl.BlockSpec((1, D, tq), lambda b, i: (b, 0, i)),
            pl.BlockSpec((1, S // 2, D), lambda b, i: (b, 0, 0)),
            pl.BlockSpec((1, D // 2, S), lambda b, i: (b, 0, 0)),
            _const_spec((1, 4, HEAD_DIM)),
            _const_spec((V_DIM, 1)),
        ],
        out_specs=pl.BlockSpec((1, D, tq), lambda b, i: (b, 0, i)),
        scratch_shapes=[
            pltpu.VMEM((2 * N_HEADS, 2 * V_DIM, tq), BF16),
            pltpu.VMEM((N_HEADS, tq, V_DIM), BF16),
            pltpu.VMEM((SCORE_LOOKAHEAD, 2, tq, tq), F32),
            pltpu.VMEM((2 * N_HEADS, 1, tq), F32),
            pltpu.VMEM((2 * N_HEADS, V_DIM + ONES_ROWS, tq), F32),
        ],
        compiler_params=pltpu.CompilerParams(
            dimension_semantics=("arbitrary", "arbitrary"), vmem_limit_bytes=VMEM_LIMIT),
        name="diff_attention",
    )(qt, k, vt, lam_vecs.reshape(1, 4, HEAD_DIM), subln_g.reshape(V_DIM, 1))


def _attention_heads(x, k_v, g, w_q, lam_vecs, subln_g, slopes, lam_init):
    k, vt = k_v
    qt = _q_proj(x, g, w_q)
    return _diff_attention(qt, k, vt, lam_vecs, subln_g, slopes, lam_init)


def kernel(x, a_norm, a_w_in, a_b_in, a_conv_w, a_conv_b, a_gate_w, a_gate_b, a_lambda, a_w_out,
           kv_norm, w_kv, b_norm, b_w_q, b_lam, b_subln, b_w_o,
           mlp_norm, mlp_w1, mlp_w2, final_norm):
    depth = mlp_w1.shape[0]
    n_a = a_w_in.shape[0]
    slopes = tuple(2.0 ** (-8.0 * (h + 1) / N_HEADS) for h in range(N_HEADS))
    k_v = None
    for l in range(depth):
        if l < n_a:
            x = _recurrent_mlp_layer(x, a_norm[l], a_w_in[l], a_b_in[l], a_conv_w[l], a_conv_b[l],
                                     a_gate_w[l], a_gate_b[l], a_lambda[l], a_w_out[l],
                                     mlp_norm[l], mlp_w1[l], mlp_w2[l])
        else:
            j = l - n_a
            lam_init = 0.8 - 0.6 * math.exp(-0.3 * l)
            heads = _attention_heads(x, k_v, b_norm[j], b_w_q[j], b_lam[j], b_subln[j], slopes, lam_init)
            x = _mlp_layer(x, mlp_norm[l], mlp_w1[l], mlp_w2[l], final_norm,
                           final_norm=(l == depth - 1), attn=(heads, b_w_o[j]))
        if l == n_a - 1:
            k_v = _kv_proj(x, kv_norm, w_kv)
    return x
```

```python
import functools
import math

import jax
import jax.numpy as jnp
import numpy as np
from jax import lax
from jax.experimental import pallas as pl
from jax.experimental.pallas import tpu as pltpu

D_MODEL = 1024
N_HEADS = 8
HEAD_DIM = 64
V_DIM = 128
LRU_BLOCKS = 8
LRU_BLOCK_W = 128
CONV_W = 4
LRU_C = 8.0
NORM_EPS = 1e-6
SUBLN_EPS = 1e-5

SUBLANES = 8
ONES_ROWS = 16
SCORE_LOOKAHEAD = 4
LOG2E = math.log2(math.e)
REC_COLS = 256
REC_GATE_LOOKAHEAD = 2
PROJ_ROWS = 256
VMEM_LIMIT = 56 * 1024 * 1024

BF16 = jnp.bfloat16
F32 = jnp.float32


def _rmsnorm(x, g, eps):
    return x * lax.rsqrt(jnp.mean(x * x, axis=-1, keepdims=True) + eps) * g


def _bf16(packed):
    return pltpu.bitcast(packed, BF16)


def _const_spec(shape):
    nd = len(shape)
    return pl.BlockSpec(shape, lambda *_: (0,) * nd, pipeline_mode=pl.Buffered(1))


def _gelu_tanh(x):
    c1 = math.sqrt(2.0 / math.pi)
    return x * (0.5 + 0.5 * jnp.tanh(x * (c1 + (c1 * 0.044715) * (x * x))))


def _rec_mlp_kernel(x_ref, g_ref, wx_ref, wy_ref, bx_ref, by_ref, cw_ref, cb_ref, gw_ref, gb_ref,
                    lam_ref, wo_ref, perm_ref, unperm_ref, xres_ref, mg_ref, w1_ref, w2_ref, o_ref,
                    hn_s, xb_s, yb_s, xmid_s, xpad_ref, tail_ref, h_ref, *, ts, tiles_per_seq):
    f = pl.program_id(0)
    W = D_MODEL
    G = ts // SUBLANES
    HALO = (CONV_W - 1) * SUBLANES
    NB = W // REC_COLS
    starts_sequence = lax.rem(f - 2, tiles_per_seq) == 0

    @pl.when(f == 0)
    def _():
        hn_s[...] = jnp.zeros_like(hn_s)
        xb_s[...] = jnp.zeros_like(xb_s)
        yb_s[...] = jnp.zeros_like(yb_s)
        xmid_s[...] = jnp.zeros_like(xmid_s)
        tail_ref[...] = jnp.zeros_like(tail_ref)
        h_ref[...] = jnp.zeros_like(h_ref)

    row8 = lax.broadcasted_iota(jnp.int32, (SUBLANES, REC_COLS), 0)

    def norm_permute():
        hn = _rmsnorm(x_ref[0], g_ref[...], NORM_EPS).astype(BF16)
        hn_s[...] = jnp.dot(perm_ref[...], hn, preferred_element_type=F32).astype(BF16)

    def in_proj(cb):
        cs = slice(cb * REC_COLS, (cb + 1) * REC_COLS)
        hn = hn_s[...]
        xb_s[:, cs] = jnp.dot(hn, wx_ref[:, cs], preferred_element_type=F32) + bx_ref[:, cs]
        yb_s[:, cs] = jnp.dot(hn, wy_ref[:, cs], preferred_element_type=F32) + by_ref[:, cs]

    def load_proj(cb):
        cs = slice(cb * REC_COLS, (cb + 1) * REC_COLS)
        return xb_s[:, cs], yb_s[:, cs]

    def conv_gates(cb, xb, yb):
        cs = slice(cb * REC_COLS, (cb + 1) * REC_COLS)
        for k in range(CONV_W - 1):
            cur = xb[ts - HALO + k * SUBLANES:ts - HALO + (k + 1) * SUBLANES, :]
            prev = jnp.where(starts_sequence, 0.0, tail_ref[k * SUBLANES:(k + 1) * SUBLANES, cs])
            xpad_ref[k * SUBLANES:(k + 1) * SUBLANES, cs] = jnp.where(
                row8 == 0, pltpu.roll(prev, 1, axis=0), pltpu.roll(cur, 1, axis=0))
        tail_ref[:, cs] = xb[ts - HALO:, :]
        xpad_ref[HALO:, cs] = xb
        xc = xb * cw_ref[CONV_W - 1:CONV_W, cs] + cb_ref[:, cs]
        for j in range(CONV_W - 1):
            xc = xc + xpad_ref[j * SUBLANES:j * SUBLANES + ts, cs] * cw_ref[j:j + 1, cs]
        xcb = xc.astype(BF16)
        r_parts, i_parts = [], []
        for n in range(REC_COLS // LRU_BLOCK_W):
            gn = jnp.dot(xcb[:, n * LRU_BLOCK_W:(n + 1) * LRU_BLOCK_W],
                         gw_ref[cb * (REC_COLS // LRU_BLOCK_W) + n], preferred_element_type=F32)
            r_parts.append(gn[:, :LRU_BLOCK_W])
            i_parts.append(gn[:, LRU_BLOCK_W:])
        return xc, jnp.concatenate(r_parts, axis=1), jnp.concatenate(i_parts, axis=1), yb

    def recurrence(cb, xc, g_r, g_i, yb):
        cs = slice(cb * REC_COLS, (cb + 1) * REC_COLS)
        r = jax.nn.sigmoid(g_r + gb_ref[0:1, cs])
        ig = jax.nn.sigmoid(g_i + gb_ref[1:2, cs])
        z = -lam_ref[:, cs]
        softplus = jnp.maximum(z, 0.0) + jnp.log1p(jnp.exp(-jnp.abs(z)))
        log_a = (-LRU_C * softplus) * r
        a = jnp.exp(log_a)
        bx = jnp.sqrt(-jnp.tanh(log_a) * (a * a + 1.0)) * (ig * xc)

        h_loc = jnp.zeros((SUBLANES, REC_COLS), F32)
        prod = jnp.ones((SUBLANES, REC_COLS), F32)
        h_steps, p_steps = [], []
        for g in range(G):
            a_g = a[g * SUBLANES:(g + 1) * SUBLANES, :]
            h_loc = a_g * h_loc + bx[g * SUBLANES:(g + 1) * SUBLANES, :]
            prod = a_g * prod
            h_steps.append(h_loc)
            p_steps.append(prod)

        t_cum, f_cum = prod, h_loc
        for d in (1, 2, 4):
            keep = row8 >= d
            f_cum = jnp.where(keep, t_cum * pltpu.roll(f_cum, d, axis=0) + f_cum, f_cum)
            t_cum = jnp.where(keep, t_cum * pltpu.roll(t_cum, d, axis=0), t_cum)
        h_prev = jnp.where(starts_sequence, 0.0, h_ref[SUBLANES - 1:SUBLANES, cs])
        h_end = t_cum * h_prev + f_cum
        h_in = jnp.where(row8 == 0, h_prev, pltpu.roll(h_end, 1, axis=0))
        h_ref[:, cs] = h_end
        hs = jnp.concatenate([h_steps[g] + p_steps[g] * h_in for g in range(G)], axis=0)
        return (hs * _gelu_tanh(yb)).astype(BF16)

    def unpermute(y):
        return jnp.dot(unperm_ref[...], y, preferred_element_type=F32).astype(BF16)

    def out_proj(cb, y):
        return jnp.dot(y, wo_ref[cb * REC_COLS:(cb + 1) * REC_COLS, :], preferred_element_type=F32)

    F = w1_ref.shape[1]
    fchunk = F // NB
    xm = xmid_s[...]
    hm = _rmsnorm(xm, mg_ref[...], NORM_EPS).astype(BF16)

    def mlp_up(fc):
        fs = slice(fc * fchunk, (fc + 1) * fchunk)
        u = jnp.maximum(jnp.dot(hm, w1_ref[:, fs], preferred_element_type=F32), 0.0)
        return (u * u).astype(BF16)

    def mlp_down(fc, u):
        return jnp.dot(u, w2_ref[fc * fchunk:(fc + 1) * fchunk, :], preferred_element_type=F32)

    loaded = {cb: load_proj(cb) for cb in range(NB)}
    for cb in range(NB):
        in_proj(cb)
    norm_permute()
    up = {0: mlp_up(0)}
    gated = {cb: conv_gates(cb, *loaded.pop(cb)) for cb in range(REC_GATE_LOOKAHEAD)}
    out = xres_ref[0]
    mlp_out = xm
    y_nat = {}
    for cb in range(NB):
        if cb + 1 < NB:
            up[cb + 1] = mlp_up(cb + 1)
        mlp_out = mlp_out + mlp_down(cb, up.pop(cb))
        nxt = cb + REC_GATE_LOOKAHEAD
        if nxt < NB:
            gated[nxt] = conv_gates(nxt, *loaded.pop(nxt))
        y_nat[cb] = unpermute(recurrence(cb, *gated.pop(cb)))
        if cb > 0:
            out = out + out_proj(cb - 1, y_nat.pop(cb - 1))
    out = out + out_proj(NB - 1, y_nat.pop(NB - 1))
    o_ref[0] = mlp_out
    xmid_s[...] = out


def _chunk_permutation(ts):
    steps = ts // SUBLANES
    p = np.zeros((ts, ts), np.float32)
    for c in range(SUBLANES):
        for s in range(steps):
            p[s * SUBLANES + c, c * steps + s] = 1.0
    return p


def _recurrent_mlp_layer(x, g, w_in, b_in, conv_w, conv_b, gate_w, gate_b, lam, w_out,
                         mlp_g, w1, w2, *, ts=256):
    B, S, D = x.shape
    F = w1.shape[1]
    W = D_MODEL
    n_seq = S // ts
    n_tiles = B * n_seq
    wx = w_in[:, :W].astype(BF16)
    wy = w_in[:, W:].astype(BF16)
    bxv = b_in[:W].reshape(1, W)
    byv = b_in[W:].reshape(1, W)
    gw = jnp.concatenate([gate_w[0], gate_w[1]], axis=-1).astype(BF16)
    perm = _chunk_permutation(ts)
    kern = functools.partial(_rec_mlp_kernel, ts=ts, tiles_per_seq=n_seq)

    def in_tile(f):
        t = jnp.minimum(f, n_tiles - 1)
        return (t // n_seq, t % n_seq, 0)

    def res_tile(f):
        t = jnp.clip(f - 2, 0, n_tiles - 1)
        return (t // n_seq, t % n_seq, 0)

    def out_tile(f):
        t = jnp.maximum(f - 3, 0)
        return (t // n_seq, t % n_seq, 0)

    return pl.pallas_call(
        kern,
        out_shape=jax.ShapeDtypeStruct((B, S, D), F32),
        grid=(n_tiles + 3,),
        in_specs=[
            pl.BlockSpec((1, ts, D), in_tile),
            _const_spec((1, D)),
            _const_spec((D, W)), _const_spec((D, W)),
            _const_spec((1, W)), _const_spec((1, W)),
            _const_spec((CONV_W, W)), _const_spec((1, W)),
            _const_spec((LRU_BLOCKS, LRU_BLOCK_W, 2 * LRU_BLOCK_W)),
            _const_spec((2, W)),
            _const_spec((1, W)),
            _const_spec((W, D)),
            _const_spec((ts, ts)), _const_spec((ts, ts)),
            pl.BlockSpec((1, ts, D), res_tile),
            _const_spec((1, D)), _const_spec((D, F)), _const_spec((F, D)),
        ],
        out_specs=pl.BlockSpec((1, ts, D), out_tile),
        scratch_shapes=[
            pltpu.VMEM((ts, D), BF16),
            pltpu.VMEM((ts, W), F32),
            pltpu.VMEM((ts, W), F32),
            pltpu.VMEM((ts, D), F32),
            pltpu.VMEM((ts + (CONV_W - 1) * SUBLANES, W), F32),
            pltpu.VMEM(((CONV_W - 1) * SUBLANES, W), F32),
            pltpu.VMEM((SUBLANES, W), F32),
        ],
        compiler_params=pltpu.CompilerParams(
            dimension_semantics=("arbitrary",), vmem_limit_bytes=VMEM_LIMIT),
        name="rglru_mlp_layer",
    )(x, g.reshape(1, D), wx, wy, bxv, byv, conv_w, conv_b.reshape(1, W), gw, gate_b,
      lam.reshape(1, W), w_out.astype(BF16), jnp.asarray(perm, BF16), jnp.asarray(perm.T, BF16), x,
      mlp_g.reshape(1, D), w1.astype(BF16), w2.astype(BF16))


def _mlp_halves(xs, g_ref, w1_ref, w2_ref, fg_ref, o_ref, final_norm):
    hs = [_rmsnorm(x, g_ref[...], NORM_EPS).astype(BF16) for x in xs]
    us = [jnp.maximum(jnp.dot(h, w1_ref[...], preferred_element_type=F32), 0.0) for h in hs]
    us = [(u * u).astype(BF16) for u in us]
    ys = [x + jnp.dot(u, w2_ref[...], preferred_element_type=F32) for x, u in zip(xs, us)]
    half = xs[0].shape[0]
    for i, y in enumerate(ys):
        if final_norm:
            y = _rmsnorm(y, fg_ref[...], NORM_EPS)
        o_ref[0, i * half:(i + 1) * half, :] = y


def _row_halves(n):
    return [slice(0, n // 2), slice(n // 2, n)]


def _mlp_kernel(x_ref, g_ref, w1_ref, w2_ref, fg_ref, o_ref, *, final_norm):
    xs = [x_ref[0, rows, :] for rows in _row_halves(x_ref.shape[1])]
    _mlp_halves(xs, g_ref, w1_ref, w2_ref, fg_ref, o_ref, final_norm)


def _attn_out_mlp_kernel(at_ref, wo_ref, x_ref, g_ref, w1_ref, w2_ref, fg_ref, o_ref, *, final_norm):
    xs = [x_ref[0, rows, :] + lax.dot_general(at_ref[0, :, rows], wo_ref[...], (((0,), (0,)), ((), ())),
                                              preferred_element_type=F32)
          for rows in _row_halves(x_ref.shape[1])]
    _mlp_halves(xs, g_ref, w1_ref, w2_ref, fg_ref, o_ref, final_norm)


def _mlp_layer(x, g, w1, w2, final_g, *, final_norm, attn=None, tm=512):
    B, S, D = x.shape
    F = w1.shape[1]
    row_spec = pl.BlockSpec((1, tm, D), lambda b, s: (b, s, 0))
    specs = [row_spec, _const_spec((1, D)), _const_spec((D, F)), _const_spec((F, D)),
             _const_spec((1, D))]
    args = [x, g.reshape(1, D), w1.astype(BF16), w2.astype(BF16), final_g.reshape(1, D)]
    kern = _mlp_kernel
    if attn is not None:
        at, w_o = attn
        specs = [pl.BlockSpec((1, D, tm), lambda b, s: (b, 0, s)), _const_spec((D, D))] + specs
        args = [at, w_o.astype(BF16)] + args
        kern = _attn_out_mlp_kernel
    return pl.pallas_call(
        functools.partial(kern, final_norm=final_norm),
        out_shape=jax.ShapeDtypeStruct((B, S, D), F32),
        grid=(B, S // tm),
        in_specs=specs,
        out_specs=row_spec,
        compiler_params=pltpu.CompilerParams(
            dimension_semantics=("parallel", "parallel"), vmem_limit_bytes=VMEM_LIMIT),
        name="mlp_layer",
    )(*args)


def _kv_kernel(x_ref, g_ref, wk_ref, wv_ref, k_ref, vt_ref):
    for r0 in range(0, x_ref.shape[1], PROJ_ROWS):
        rows = slice(r0, r0 + PROJ_ROWS)
        hn = _rmsnorm(x_ref[0, rows, :], g_ref[...], NORM_EPS).astype(BF16)
        k = jnp.dot(hn, wk_ref[...], preferred_element_type=F32).astype(BF16)
        k_ref[0, r0 // 2:(r0 + PROJ_ROWS) // 2, :] = pltpu.bitcast(k, jnp.uint32)
        v = jnp.dot(hn, wv_ref[...], preferred_element_type=F32)
        vt_ref[0, :, rows] = pltpu.bitcast(v.T.astype(BF16), jnp.uint32)


def _kv_proj(x, g, w_kv, *, tm=1024):
    B, S, D = x.shape
    return pl.pallas_call(
        _kv_kernel,
        out_shape=(jax.ShapeDtypeStruct((B, S // 2, D), jnp.uint32),
                   jax.ShapeDtypeStruct((B, D // 2, S), jnp.uint32)),
        grid=(B, S // tm),
        in_specs=[
            pl.BlockSpec((1, tm, D), lambda b, s: (b, s, 0)),
            _const_spec((1, D)), _const_spec((D, D)), _const_spec((D, D)),
        ],
        out_specs=(pl.BlockSpec((1, tm // 2, D), lambda b, s: (b, s, 0)),
                   pl.BlockSpec((1, D // 2, tm), lambda b, s: (b, 0, s))),
        compiler_params=pltpu.CompilerParams(
            dimension_semantics=("parallel", "parallel"), vmem_limit_bytes=VMEM_LIMIT),
        name="kv_proj",
    )(x, g.reshape(1, D), w_kv[:, :D].astype(BF16), w_kv[:, D:].astype(BF16))


def _q_kernel(x_ref, g_ref, wq_ref, qt_ref, *, scale):
    for r0 in range(0, x_ref.shape[1], PROJ_ROWS):
        rows = slice(r0, r0 + PROJ_ROWS)
        hn = _rmsnorm(x_ref[0, rows, :], g_ref[...], NORM_EPS).astype(BF16)
        q = jnp.dot(hn, wq_ref[...], preferred_element_type=F32) * scale
        qt_ref[0, :, rows] = q.T.astype(BF16)


def _q_proj(x, g, w_q, *, tm=1024):
    B, S, D = x.shape
    kern = functools.partial(_q_kernel, scale=LOG2E * HEAD_DIM ** -0.5)
    return pl.pallas_call(
        kern,
        out_shape=jax.ShapeDtypeStruct((B, D, S), BF16),
        grid=(B, S // tm),
        in_specs=[
            pl.BlockSpec((1, tm, D), lambda b, s: (b, s, 0)),
            _const_spec((1, D)), _const_spec((D, D)),
        ],
        out_specs=pl.BlockSpec((1, D, tm), lambda b, s: (b, 0, s)),
        compiler_params=pltpu.CompilerParams(
            dimension_semantics=("parallel", "parallel"), vmem_limit_bytes=VMEM_LIMIT),
        name="q_proj",
    )(x, g.reshape(1, D), w_q.astype(BF16))


def _bf16_terms(c, n=3):
    terms, rest = [], np.float32(c)
    for _ in range(n):
        t = np.float32(np.asarray(rest, dtype=BF16))
        terms.append(float(t))
        rest = np.float32(rest - t)
    return terms


def _attn_kernel(qt_ref, k_ref, vt_ref, lam_ref, sg_ref, ot_ref,
                 qaug_ref, kaug_ref, sc_ref, m_ref, acc_ref, *, tq, lam_init, slopes):
    qi = pl.program_id(1)
    tk = tq
    LA = SCORE_LOOKAHEAD
    log2_slopes = [s * LOG2E for s in slopes]

    @pl.when((pl.program_id(0) == 0) & (qi == 0))
    def _():
        arow = lax.broadcasted_iota(jnp.int32, (V_DIM, tq), 0)
        dq_row = lax.broadcasted_iota(jnp.int32, (V_DIM, tq), 1).astype(F32)
        col = lax.broadcasted_iota(jnp.int32, (tk, V_DIM), 1)
        dk_col = lax.broadcasted_iota(jnp.int32, (tk, V_DIM), 0).astype(F32)
        for h in range(N_HEADS):
            c = _bf16_terms(log2_slopes[h])
            q_aug = jnp.where(arow == 0, c[0], jnp.where(arow == 1, c[1], jnp.where(
                arow == 2, c[2], jnp.where(arow < 6, dq_row, 0.0)))).astype(BF16)
            kaug_ref[h] = jnp.where(col < 3, dk_col, jnp.where(col == 3, -c[0], jnp.where(
                col == 4, -c[1], jnp.where(col == 5, -c[2], 0.0)))).astype(BF16)
            qaug_ref[2 * h, V_DIM:, :] = q_aug
            qaug_ref[2 * h + 1, V_DIM:, :] = q_aug

    row = lax.broadcasted_iota(jnp.int32, (V_DIM, tq), 0)
    for h in range(N_HEADS):
        qt = qt_ref[0, h * V_DIM:(h + 1) * V_DIM, :]
        zero = jnp.zeros_like(qt)
        qaug_ref[2 * h, 0:V_DIM, :] = jnp.where(row < HEAD_DIM, qt, zero)
        qaug_ref[2 * h + 1, 0:V_DIM, :] = jnp.where(row >= HEAD_DIM, qt, zero)

    ones_rows = jnp.ones((ONES_ROWS, tk), BF16)

    def issue_scores(j, h):
        off2 = pl.multiple_of(j * (tk // 2), tk // 2)
        kt = _bf16(k_ref[0, pl.ds(off2, tk // 2), h * V_DIM:(h + 1) * V_DIM])
        lhs = jnp.concatenate([kt, kaug_ref[h]], axis=1)
        for c in range(2):
            sc_ref[h % LA, c] = jnp.dot(lhs, qaug_ref[2 * h + c], preferred_element_type=F32)

    def softmax_pv(j, h, future):
        off = pl.multiple_of(j * tk, tk)
        vt = _bf16(vt_ref[0, h * V_DIM // 2:(h + 1) * V_DIM // 2, pl.ds(off, tk)])
        vta = jnp.concatenate([vt, ones_rows], axis=0)
        tile_bias = -log2_slopes[h] * ((qi - j) * tk).astype(F32)
        for c in range(2):
            idx = 2 * h + c
            sc = sc_ref[h % LA, c]
            if future is not None:
                sc = jnp.where(future, -jnp.inf, sc)
            m_old = m_ref[idx]
            m_new = jnp.maximum(m_old, jnp.max(sc, axis=0, keepdims=True) + tile_bias)
            alpha = jnp.exp2(m_old - m_new)
            p = jnp.exp2(sc - (m_new - tile_bias)).astype(BF16)
            acc_ref[idx] = alpha * acc_ref[idx] + jnp.dot(vta, p, preferred_element_type=F32)
            m_ref[idx] = m_new

    for h in range(LA):
        issue_scores(0, h)
    m_ref[...] = jnp.full(m_ref.shape, -jnp.inf, F32)
    acc_ref[...] = jnp.zeros(acc_ref.shape, F32)

    def unmasked_tile(j):
        for h in range(N_HEADS):
            softmax_pv(j, h, None)
            if h + LA < N_HEADS:
                issue_scores(j, h + LA)
            else:
                issue_scores(j + 1, h + LA - N_HEADS)

    def tile_pair(i, carry):
        unmasked_tile(2 * i)
        unmasked_tile(2 * i + 1)
        return carry

    def odd_tile(i, carry):
        unmasked_tile(qi - 1)
        return carry

    lax.fori_loop(0, qi // 2, tile_pair, 0)
    lax.fori_loop(0, qi % 2, odd_tile, 0)

    future = (lax.broadcasted_iota(jnp.int32, (tk, tq), 0)
              > lax.broadcasted_iota(jnp.int32, (tk, tq), 1))
    for h in range(N_HEADS):
        softmax_pv(qi, h, future)
        if h + LA < N_HEADS:
            issue_scores(qi, h + LA)

    lv = lam_ref[0]
    lam = (jnp.exp(jnp.sum(lv[0:1] * lv[1:2], keepdims=True))
           - jnp.exp(jnp.sum(lv[2:3] * lv[3:4], keepdims=True)) + lam_init)
    for h in range(N_HEADS):
        a1 = acc_ref[2 * h]
        a2 = acc_ref[2 * h + 1]
        o = (a1[0:V_DIM] / a1[V_DIM:V_DIM + 1]) - lam * (a2[0:V_DIM] / a2[V_DIM:V_DIM + 1])
        o = o * lax.rsqrt(jnp.mean(o * o, axis=0, keepdims=True) + SUBLN_EPS) * sg_ref[...]
        ot_ref[0, h * V_DIM:(h + 1) * V_DIM, :] = (o * (1.0 - lam_init)).astype(BF16)


def _diff_attention(qt, k, vt, lam_vecs, subln_g, slopes, lam_init, *, tq=256):
    B, D, S = qt.shape
    assert N_HEADS % SCORE_LOOKAHEAD == 0 and tq <= 256
    kern = functools.partial(_attn_kernel, tq=tq, lam_init=lam_init, slopes=slopes)
    return pl.pallas_call(
        kern,
        out_shape=jax.ShapeDtypeStruct((B, D, S), BF16),
        grid=(B, S // tq),
        in_specs=[
            pl.BlockSpec((1, D, tq), lambda b, i: (b, 0, i)),
            pl.BlockSpec((1, S // 2, D), lambda b, i: (b, 0, 0)),
            pl.BlockSpec((1, D // 2, S), lambda b, i: (b, 0, 0)),
            _const_spec((1, 4, HEAD_DIM)),
            _const_spec((V_DIM, 1)),
        ],
        out_specs=pl.BlockSpec((1, D, tq), lambda b, i: (b, 0, i)),
        scratch_shapes=[
            pltpu.VMEM((2 * N_HEADS, 2 * V_DIM, tq), BF16),
            pltpu.VMEM((N_HEADS, tq, V_DIM), BF16),
            pltpu.VMEM((SCORE_LOOKAHEAD, 2, tq, tq), F32),
            pltpu.VMEM((2 * N_HEADS, 1, tq), F32),
            pltpu.VMEM((2 * N_HEADS, V_DIM + ONES_ROWS, tq), F32),
        ],
        compiler_params=pltpu.CompilerParams(
            dimension_semantics=("arbitrary", "arbitrary"), vmem_limit_bytes=VMEM_LIMIT),
        name="diff_attention",
    )(qt, k, vt, lam_vecs.reshape(1, 4, HEAD_DIM), subln_g.reshape(V_DIM, 1))


def _attention_heads(x, k_v, g, w_q, lam_vecs, subln_g, slopes, lam_init):
    k, vt = k_v
    qt = _q_proj(x, g, w_q)
    return _diff_attention(qt, k, vt, lam_vecs, subln_g, slopes, lam_init)


def kernel(x, a_norm, a_w_in, a_b_in, a_conv_w, a_conv_b, a_gate_w, a_gate_b, a_lambda, a_w_out,
           kv_norm, w_kv, b_norm, b_w_q, b_lam, b_subln, b_w_o,
           mlp_norm, mlp_w1, mlp_w2, final_norm):
    depth = mlp_w1.shape[0]
    n_a = a_w_in.shape[0]
    slopes = tuple(2.0 ** (-8.0 * (h + 1) / N_HEADS) for h in range(N_HEADS))
    k_v = None
    for l in range(depth):
        if l < n_a:
            x = _recurrent_mlp_layer(x, a_norm[l], a_w_in[l], a_b_in[l], a_conv_w[l], a_conv_b[l],
                                     a_gate_w[l], a_gate_b[l], a_lambda[l], a_w_out[l],
                                     mlp_norm[l], mlp_w1[l], mlp_w2[l])
        else:
            j = l - n_a
            lam_init = 0.8 - 0.6 * math.exp(-0.3 * l)
            heads = _attention_heads(x, k_v, b_norm[j], b_w_q[j], b_lam[j], b_subln[j], slopes, lam_init)
            x = _mlp_layer(x, mlp_norm[l], mlp_w1[l], mlp_w2[l], final_norm,
                           final_norm=(l == depth - 1), attn=(heads, b_w_o[j]))
        if l == n_a - 1:
            k_v = _kv_proj(x, kv_norm, w_kv)
    return x
```

```python
import functools
import math

import jax
import jax.numpy as jnp
import numpy as np
from jax import lax
from jax.experimental import pallas as pl
from jax.experimental.pallas import tpu as pltpu

D_MODEL = 1024
N_HEADS = 8
HEAD_DIM = 64
V_DIM = 128
LRU_BLOCKS = 8
LRU_BLOCK_W = 128
CONV_W = 4
LRU_C = 8.0
NORM_EPS = 1e-6
SUBLN_EPS = 1e-5

SUBLANES = 8
ONES_ROWS = 16
SCORE_LOOKAHEAD = 4
LOG2E = math.log2(math.e)
REC_COLS = 256
REC_GATE_LOOKAHEAD = 2
PROJ_ROWS = 256
VMEM_LIMIT = 56 * 1024 * 1024

BF16 = jnp.bfloat16
F32 = jnp.float32


def _rmsnorm(x, g, eps):
    return x * lax.rsqrt(jnp.mean(x * x, axis=-1, keepdims=True) + eps) * g


def _bf16(packed):
    return pltpu.bitcast(packed, BF16)


def _const_spec(shape):
    nd = len(shape)
    return pl.BlockSpec(shape, lambda *_: (0,) * nd, pipeline_mode=pl.Buffered(1))


def _gelu_tanh(x):
    c1 = math.sqrt(2.0 / math.pi)
    return x * (0.5 + 0.5 * jnp.tanh(x * (c1 + (c1 * 0.044715) * (x * x))))


def _rec_mlp_kernel(x_ref, g_ref, wx_ref, wy_ref, bx_ref, by_ref, cw_ref, cb_ref, gw_ref, gb_ref,
                    lam_ref, wo_ref, perm_ref, unperm_ref, xres_ref, mg_ref, w1_ref, w2_ref, o_ref,
                    hn_s, xb_s, yb_s, xmid_s, xpad_ref, tail_ref, h_ref, *, ts, tiles_per_seq):
    f = pl.program_id(0)
    W = D_MODEL
    G = ts // SUBLANES
    HALO = (CONV_W - 1) * SUBLANES
    NB = W // REC_COLS
    starts_sequence = lax.rem(f - 2, tiles_per_seq) == 0

    @pl.when(f == 0)
    def _():
        hn_s[...] = jnp.zeros_like(hn_s)
        xb_s[...] = jnp.zeros_like(xb_s)
        yb_s[...] = jnp.zeros_like(yb_s)
        xmid_s[...] = jnp.zeros_like(xmid_s)
        tail_ref[...] = jnp.zeros_like(tail_ref)
        h_ref[...] = jnp.zeros_like(h_ref)

    row8 = lax.broadcasted_iota(jnp.int32, (SUBLANES, REC_COLS), 0)

    def norm_permute():
        hn = _rmsnorm(x_ref[0], g_ref[...], NORM_EPS).astype(BF16)
        hn_s[...] = jnp.dot(perm_ref[...], hn, preferred_element_type=F32).astype(BF16)

    def in_proj(cb):
        cs = slice(cb * REC_COLS, (cb + 1) * REC_COLS)
        hn = hn_s[...]
        xb_s[:, cs] = jnp.dot(hn, wx_ref[:, cs], preferred_element_type=F32) + bx_ref[:, cs]
        yb_s[:, cs] = jnp.dot(hn, wy_ref[:, cs], preferred_element_type=F32) + by_ref[:, cs]

    def load_proj(cb):
        cs = slice(cb * REC_COLS, (cb + 1) * REC_COLS)
        return xb_s[:, cs], yb_s[:, cs]

    def conv_gates(cb, xb, yb):
        cs = slice(cb * REC_COLS, (cb + 1) * REC_COLS)
        for k in range(CONV_W - 1):
            cur = xb[ts - HALO + k * SUBLANES:ts - HALO + (k + 1) * SUBLANES, :]
            prev = jnp.where(starts_sequence, 0.0, tail_ref[k * SUBLANES:(k + 1) * SUBLANES, cs])
            xpad_ref[k * SUBLANES:(k + 1) * SUBLANES, cs] = jnp.where(
                row8 == 0, pltpu.roll(prev, 1, axis=0), pltpu.roll(cur, 1, axis=0))
        tail_ref[:, cs] = xb[ts - HALO:, :]
        xpad_ref[HALO:, cs] = xb
        xc = xb * cw_ref[CONV_W - 1:CONV_W, cs] + cb_ref[:, cs]
        for j in range(CONV_W - 1):
            xc = xc + xpad_ref[j * SUBLANES:j * SUBLANES + ts, cs] * cw_ref[j:j + 1, cs]
        xcb = xc.astype(BF16)
        r_parts, i_parts = [], []
        for n in range(REC_COLS // LRU_BLOCK_W):
            gn = jnp.dot(xcb[:, n * LRU_BLOCK_W:(n + 1) * LRU_BLOCK_W],
                         gw_ref[cb * (REC_COLS // LRU_BLOCK_W) + n], preferred_element_type=F32)
            r_parts.append(gn[:, :LRU_BLOCK_W])
            i_parts.append(gn[:, LRU_BLOCK_W:])
        return xc, jnp.concatenate(r_parts, axis=1), jnp.concatenate(i_parts, axis=1), yb

    def recurrence(cb, xc, g_r, g_i, yb):
        cs = slice(cb * REC_COLS, (cb + 1) * REC_COLS)
        r = jax.nn.sigmoid(g_r + gb_ref[0:1, cs])
        ig = jax.nn.sigmoid(g_i + gb_ref[1:2, cs])
        z = -lam_ref[:, cs]
        softplus = jnp.maximum(z, 0.0) + jnp.log1p(jnp.exp(-jnp.abs(z)))
        log_a = (-LRU_C * softplus) * r
        a = jnp.exp(log_a)
        bx = jnp.sqrt(-jnp.tanh(log_a) * (a * a + 1.0)) * (ig * xc)

        h_loc = jnp.zeros((SUBLANES, REC_COLS), F32)
        prod = jnp.ones((SUBLANES, REC_COLS), F32)
        h_steps, p_steps = [], []
        for g in range(G):
            a_g = a[g * SUBLANES:(g + 1) * SUBLANES, :]
            h_loc = a_g * h_loc + bx[g * SUBLANES:(g + 1) * SUBLANES, :]
            prod = a_g * prod
            h_steps.append(h_loc)
            p_steps.append(prod)

        t_cum, f_cum = prod, h_loc
        for d in (1, 2, 4):
            keep = row8 >= d
            f_cum = jnp.where(keep, t_cum * pltpu.roll(f_cum, d, axis=0) + f_cum, f_cum)
            t_cum = jnp.where(keep, t_cum * pltpu.roll(t_cum, d, axis=0), t_cum)
        h_prev = jnp.where(starts_sequence, 0.0, h_ref[SUBLANES - 1:SUBLANES, cs])
        h_end = t_cum * h_prev + f_cum
        h_in = jnp.where(row8 == 0, h_prev, pltpu.roll(h_end, 1, axis=0))
        h_ref[:, cs] = h_end
        hs = jnp.concatenate([h_steps[g] + p_steps[g] * h_in for g in range(G)], axis=0)
        return (hs * _gelu_tanh(yb)).astype(BF16)

    def unpermute(y):
        return jnp.dot(unperm_ref[...], y, preferred_element_type=F32).astype(BF16)

    def out_proj(cb, y):
        return jnp.dot(y, wo_ref[cb * REC_COLS:(cb + 1) * REC_COLS, :], preferred_element_type=F32)

    F = w1_ref.shape[1]
    fchunk = F // NB
    xm = xmid_s[...]
    hm = _rmsnorm(xm, mg_ref[...], NORM_EPS).astype(BF16)

    def mlp_up(fc):
        fs = slice(fc * fchunk, (fc + 1) * fchunk)
        u = jnp.maximum(jnp.dot(hm, w1_ref[:, fs], preferred_element_type=F32), 0.0)
        return (u * u).astype(BF16)

    def mlp_down(fc, u):
        return jnp.dot(u, w2_ref[fc * fchunk:(fc + 1) * fchunk, :], preferred_element_type=F32)

    loaded = {cb: load_proj(cb) for cb in range(NB)}
    for cb in range(NB):
        in_proj(cb)
    norm_permute()
    up = {0: mlp_up(0)}
    gated = {cb: conv_gates(cb, *loaded.pop(cb)) for cb in range(REC_GATE_LOOKAHEAD)}
    out = xres_ref[0]
    mlp_out = xm
    y_nat = {}
    for cb in range(NB):
        if cb + 1 < NB:
            up[cb + 1] = mlp_up(cb + 1)
        mlp_out = mlp_out + mlp_down(cb, up.pop(cb))
        nxt = cb + REC_GATE_LOOKAHEAD
        if nxt < NB:
            gated[nxt] = conv_gates(nxt, *loaded.pop(nxt))
        y_nat[cb] = unpermute(recurrence(cb, *gated.pop(cb)))
        if cb > 0:
            out = out + out_proj(cb - 1, y_nat.pop(cb - 1))
    out = out + out_proj(NB - 1, y_nat.pop(NB - 1))
    o_ref[0] = mlp_out
    xmid_s[...] = out


def _chunk_permutation(ts):
    steps = ts // SUBLANES
    p = np.zeros((ts, ts), np.float32)
    for c in range(SUBLANES):
        for s in range(steps):
            p[s * SUBLANES + c, c * steps + s] = 1.0
    return p


def _recurrent_mlp_layer(x, g, w_in, b_in, conv_w, conv_b, gate_w, gate_b, lam, w_out,
                         mlp_g, w1, w2, *, ts=256):
    B, S, D = x.shape
    F = w1.shape[1]
    W = D_MODEL
    n_seq = S // ts
    n_tiles = B * n_seq
    wx = w_in[:, :W].astype(BF16)
    wy = w_in[:, W:].astype(BF16)
    bxv = b_in[:W].reshape(1, W)
    byv = b_in[W:].reshape(1, W)
    gw = jnp.concatenate([gate_w[0], gate_w[1]], axis=-1).astype(BF16)
    perm = _chunk_permutation(ts)
    kern = functools.partial(_rec_mlp_kernel, ts=ts, tiles_per_seq=n_seq)

    def in_tile(f):
        t = jnp.minimum(f, n_tiles - 1)
        return (t // n_seq, t % n_seq, 0)

    def res_tile(f):
        t = jnp.clip(f - 2, 0, n_tiles - 1)
        return (t // n_seq, t % n_seq, 0)

    def out_tile(f):
        t = jnp.maximum(f - 3, 0)
        return (t // n_seq, t % n_seq, 0)

    return pl.pallas_call(
        kern,
        out_shape=jax.ShapeDtypeStruct((B, S, D), F32),
        grid=(n_tiles + 3,),
        in_specs=[
            pl.BlockSpec((1, ts, D), in_tile),
            _const_spec((1, D)),
            _const_spec((D, W)), _const_spec((D, W)),
            _const_spec((1, W)), _const_spec((1, W)),
            _const_spec((CONV_W, W)), _const_spec((1, W)),
            _const_spec((LRU_BLOCKS, LRU_BLOCK_W, 2 * LRU_BLOCK_W)),
            _const_spec((2, W)),
            _const_spec((1, W)),
            _const_spec((W, D)),
            _const_spec((ts, ts)), _const_spec((ts, ts)),
            pl.BlockSpec((1, ts, D), res_tile),
            _const_spec((1, D)), _const_spec((D, F)), _const_spec((F, D)),
        ],
        out_specs=pl.BlockSpec((1, ts, D), out_tile),
        scratch_shapes=[
            pltpu.VMEM((ts, D), BF16),
            pltpu.VMEM((ts, W), F32),
            pltpu.VMEM((ts, W), F32),
            pltpu.VMEM((ts, D), F32),
            pltpu.VMEM((ts + (CONV_W - 1) * SUBLANES, W), F32),
            pltpu.VMEM(((CONV_W - 1) * SUBLANES, W), F32),
            pltpu.VMEM((SUBLANES, W), F32),
        ],
        compiler_params=pltpu.CompilerParams(
            dimension_semantics=("arbitrary",), vmem_limit_bytes=VMEM_LIMIT),
        name="rglru_mlp_layer",
    )(x, g.reshape(1, D), wx, wy, bxv, byv, conv_w, conv_b.reshape(1, W), gw, gate_b,
      lam.reshape(1, W), w_out.astype(BF16), jnp.asarray(perm, BF16), jnp.asarray(perm.T, BF16), x,
      mlp_g.reshape(1, D), w1.astype(BF16), w2.astype(BF16))


def _mlp_halves(xs, g_ref, w1_ref, w2_ref, fg_ref, o_ref, final_norm):
    hs = [_rmsnorm(x, g_ref[...], NORM_EPS).astype(BF16) for x in xs]
    us = [jnp.maximum(jnp.dot(h, w1_ref[...], preferred_element_type=F32), 0.0) for h in hs]
    us = [(u * u).astype(BF16) for u in us]
    ys = [x + jnp.dot(u, w2_ref[...], preferred_element_type=F32) for x, u in zip(xs, us)]
    half = xs[0].shape[0]
    for i, y in enumerate(ys):
        if final_norm:
            y = _rmsnorm(y, fg_ref[...], NORM_EPS)
        o_ref[0, i * half:(i + 1) * half, :] = y


def _row_halves(n):
    return [slice(0, n // 2), slice(n // 2, n)]


def _mlp_kernel(x_ref, g_ref, w1_ref, w2_ref, fg_ref, o_ref, *, final_norm):
    xs = [x_ref[0, rows, :] for rows in _row_halves(x_ref.shape[1])]
    _mlp_halves(xs, g_ref, w1_ref, w2_ref, fg_ref, o_ref, final_norm)


def _attn_out_mlp_kernel(at_ref, wo_ref, x_ref, g_ref, w1_ref, w2_ref, fg_ref, o_ref, *, final_norm):
    xs = [x_ref[0, rows, :] + lax.dot_general(at_ref[0, :, rows], wo_ref[...], (((0,), (0,)), ((), ())),
                                              preferred_element_type=F32)
          for rows in _row_halves(x_ref.shape[1])]
    _mlp_halves(xs, g_ref, w1_ref, w2_ref, fg_ref, o_ref, final_norm)


def _mlp_layer(x, g, w1, w2, final_g, *, final_norm, attn=None, tm=512):
    B, S, D = x.shape
    F = w1.shape[1]
    row_spec = pl.BlockSpec((1, tm, D), lambda b, s: (b, s, 0))
    specs = [row_spec, _const_spec((1, D)), _const_spec((D, F)), _const_spec((F, D)),
             _const_spec((1, D))]
    args = [x, g.reshape(1, D), w1.astype(BF16), w2.astype(BF16), final_g.reshape(1, D)]
    kern = _mlp_kernel
    if attn is not None:
        at, w_o = attn
        specs = [pl.BlockSpec((1, D, tm), lambda b, s: (b, 0, s)), _const_spec((D, D))] + specs
        args = [at, w_o.astype(BF16)] + args
        kern = _attn_out_mlp_kernel
    return pl.pallas_call(
        functools.partial(kern, final_norm=final_norm),
        out_shape=jax.ShapeDtypeStruct((B, S, D), F32),
        grid=(B, S // tm),
        in_specs=specs,
        out_specs=row_spec,
        compiler_params=pltpu.CompilerParams(
            dimension_semantics=("parallel", "parallel"), vmem_limit_bytes=VMEM_LIMIT),
        name="mlp_layer",
    )(*args)


def _kv_kernel(x_ref, g_ref, wk_ref, wv_ref, k_ref, vt_ref):
    for r0 in range(0, x_ref.shape[1], PROJ_ROWS):
        rows = slice(r0, r0 + PROJ_ROWS)
        hn = _rmsnorm(x_ref[0, rows, :], g_ref[...], NORM_EPS).astype(BF16)
        k = jnp.dot(hn, wk_ref[...], preferred_element_type=F32).astype(BF16)
        k_ref[0, r0 // 2:(r0 + PROJ_ROWS) // 2, :] = pltpu.bitcast(k, jnp.uint32)
        v = jnp.dot(hn, wv_ref[...], preferred_element_type=F32)
        vt_ref[0, :, rows] = pltpu.bitcast(v.T.astype(BF16), jnp.uint32)


def _kv_proj(x, g, w_kv, *, tm=1024):
    B, S, D = x.shape
    return pl.pallas_call(
        _kv_kernel,
        out_shape=(jax.ShapeDtypeStruct((B, S // 2, D), jnp.uint32),
                   jax.ShapeDtypeStruct((B, D // 2, S), jnp.uint32)),
        grid=(B, S // tm),
        in_specs=[
            pl.BlockSpec((1, tm, D), lambda b, s: (b, s, 0)),
            _const_spec((1, D)), _const_spec((D, D)), _const_spec((D, D)),
        ],
        out_specs=(pl.BlockSpec((1, tm // 2, D), lambda b, s: (b, s, 0)),
                   pl.BlockSpec((1, D // 2, tm), lambda b, s: (b, 0, s))),
        compiler_params=pltpu.CompilerParams(
            dimension_semantics=("parallel", "parallel"), vmem_limit_bytes=VMEM_LIMIT),
        name="kv_proj",
    )(x, g.reshape(1, D), w_kv[:, :D].astype(BF16), w_kv[:, D:].astype(BF16))


def _q_kernel(x_ref, g_ref, wq_ref, qt_ref, *, scale):
    for r0 in range(0, x_ref.shape[1], PROJ_ROWS):
        rows = slice(r0, r0 + PROJ_ROWS)
        hn = _rmsnorm(x_ref[0, rows, :], g_ref[...], NORM_EPS).astype(BF16)
        q = jnp.dot(hn, wq_ref[...], preferred_element_type=F32) * scale
        qt_ref[0, :, rows] = q.T.astype(BF16)


def _q_proj(x, g, w_q, *, tm=1024):
    B, S, D = x.shape
    kern = functools.partial(_q_kernel, scale=LOG2E * HEAD_DIM ** -0.5)
    return pl.pallas_call(
        kern,
        out_shape=jax.ShapeDtypeStruct((B, D, S), BF16),
        grid=(B, S // tm),
        in_specs=[
            pl.BlockSpec((1, tm, D), lambda b, s: (b, s, 0)),
            _const_spec((1, D)), _const_spec((D, D)),
        ],
        out_specs=pl.BlockSpec((1, D, tm), lambda b, s: (b, 0, s)),
        compiler_params=pltpu.CompilerParams(
            dimension_semantics=("parallel", "parallel"), vmem_limit_bytes=VMEM_LIMIT),
        name="q_proj",
    )(x, g.reshape(1, D), w_q.astype(BF16))


def _bf16_terms(c, n=3):
    terms, rest = [], np.float32(c)
    for _ in range(n):
        t = np.float32(np.asarray(rest, dtype=BF16))
        terms.append(float(t))
        rest = np.float32(rest - t)
    return terms


def _attn_kernel(qt_ref, k_ref, vt_ref, lam_ref, sg_ref, ot_ref,
                 qaug_ref, kaug_ref, sc_ref, m_ref, acc_ref, *, tq, lam_init, slopes):
    qi = pl.program_id(1)
    tk = tq
    LA = SCORE_LOOKAHEAD
    log2_slopes = [s * LOG2E for s in slopes]

    @pl.when((pl.program_id(0) == 0) & (qi == 0))
    def _():
        arow = lax.broadcasted_iota(jnp.int32, (V_DIM, tq), 0)
        dq_row = lax.broadcasted_iota(jnp.int32, (V_DIM, tq), 1).astype(F32)
        col = lax.broadcasted_iota(jnp.int32, (tk, V_DIM), 1)
        dk_col = lax.broadcasted_iota(jnp.int32, (tk, V_DIM), 0).astype(F32)
        for h in range(N_HEADS):
            c = _bf16_terms(log2_slopes[h])
            q_aug = jnp.where(arow == 0, c[0], jnp.where(arow == 1, c[1], jnp.where(
                arow == 2, c[2], jnp.where(arow < 6, dq_row, 0.0)))).astype(BF16)
            kaug_ref[h] = jnp.where(col < 3, dk_col, jnp.where(col == 3, -c[0], jnp.where(
                col == 4, -c[1], jnp.where(col == 5, -c[2], 0.0)))).astype(BF16)
            qaug_ref[2 * h, V_DIM:, :] = q_aug
            qaug_ref[2 * h + 1, V_DIM:, :] = q_aug

    row = lax.broadcasted_iota(jnp.int32, (V_DIM, tq), 0)
    for h in range(N_HEADS):
        qt = qt_ref[0, h * V_DIM:(h + 1) * V_DIM, :]
        zero = jnp.zeros_like(qt)
        qaug_ref[2 * h, 0:V_DIM, :] = jnp.where(row < HEAD_DIM, qt, zero)
        qaug_ref[2 * h + 1, 0:V_DIM, :] = jnp.where(row >= HEAD_DIM, qt, zero)

    ones_rows = jnp.ones((ONES_ROWS, tk), BF16)

    def issue_scores(j, h):
        off2 = pl.multiple_of(j * (tk // 2), tk // 2)
        kt = _bf16(k_ref[0, pl.ds(off2, tk // 2), h * V_DIM:(h + 1) * V_DIM])
        lhs = jnp.concatenate([kt, kaug_ref[h]], axis=1)
        for c in range(2):
            sc_ref[h % LA, c] = jnp.dot(lhs, qaug_ref[2 * h + c], preferred_element_type=F32)

    def softmax_pv(j, h, future):
        off = pl.multiple_of(j * tk, tk)
        vt = _bf16(vt_ref[0, h * V_DIM // 2:(h + 1) * V_DIM // 2, pl.ds(off, tk)])
        vta = jnp.concatenate([vt, ones_rows], axis=0)
        tile_bias = -log2_slopes[h] * ((qi - j) * tk).astype(F32)
        for c in range(2):
            idx = 2 * h + c
            sc = sc_ref[h % LA, c]
            if future is not None:
                sc = jnp.where(future, -jnp.inf, sc)
            m_old = m_ref[idx]
            m_new = jnp.maximum(m_old, jnp.max(sc, axis=0, keepdims=True) + tile_bias)
            alpha = jnp.exp2(m_old - m_new)
            p = jnp.exp2(sc - (m_new - tile_bias)).astype(BF16)
            acc_ref[idx] = alpha * acc_ref[idx] + jnp.dot(vta, p, preferred_element_type=F32)
            m_ref[idx] = m_new

    for h in range(LA):
        issue_scores(0, h)
    m_ref[...] = jnp.full(m_ref.shape, -jnp.inf, F32)
    acc_ref[...] = jnp.zeros(acc_ref.shape, F32)

    def unmasked_tile(j):
        for h in range(N_HEADS):
            softmax_pv(j, h, None)
            if h + LA < N_HEADS:
                issue_scores(j, h + LA)
            else:
                issue_scores(j + 1, h + LA - N_HEADS)

    def tile_quad(i, carry):
        for t in range(4):
            unmasked_tile(4 * i + t)
        return carry

    def tile_pair(i, carry):
        base = (qi // 4) * 4
        unmasked_tile(base)
        unmasked_tile(base + 1)
        return carry

    def odd_tile(i, carry):
        unmasked_tile(qi - 1)
        return carry

    lax.fori_loop(0, qi // 4, tile_quad, 0)
    lax.fori_loop(0, (qi % 4) // 2, tile_pair, 0)
    lax.fori_loop(0, qi % 2, odd_tile, 0)

    future = (lax.broadcasted_iota(jnp.int32, (tk, tq), 0)
              > lax.broadcasted_iota(jnp.int32, (tk, tq), 1))
    for h in range(N_HEADS):
        softmax_pv(qi, h, future)
        if h + LA < N_HEADS:
            issue_scores(qi, h + LA)

    lv = lam_ref[0]
    lam = (jnp.exp(jnp.sum(lv[0:1] * lv[1:2], keepdims=True))
           - jnp.exp(jnp.sum(lv[2:3] * lv[3:4], keepdims=True)) + lam_init)
    for h in range(N_HEADS):
        a1 = acc_ref[2 * h]
        a2 = acc_ref[2 * h + 1]
        o = (a1[0:V_DIM] / a1[V_DIM:V_DIM + 1]) - lam * (a2[0:V_DIM] / a2[V_DIM:V_DIM + 1])
        o = o * lax.rsqrt(jnp.mean(o * o, axis=0, keepdims=True) + SUBLN_EPS) * sg_ref[...]
        ot_ref[0, h * V_DIM:(h + 1) * V_DIM, :] = (o * (1.0 - lam_init)).astype(BF16)


def _diff_attention(qt, k, vt, lam_vecs, subln_g, slopes, lam_init, *, tq=256):
    B, D, S = qt.shape
    assert N_HEADS % SCORE_LOOKAHEAD == 0 and tq <= 256
    kern = functools.partial(_attn_kernel, tq=tq, lam_init=lam_init, slopes=slopes)
    return pl.pallas_call(
        kern,
        out_shape=jax.ShapeDtypeStruct((B, D, S), BF16),
        grid=(B, S // tq),
        in_specs=[
            pl.BlockSpec((1, D, tq), lambda b, i: (b, 0, i)),
            pl.BlockSpec((1, S // 2, D), lambda b, i: (b, 0, 0)),
            pl.BlockSpec((1, D // 2, S), lambda b, i: (b, 0, 0)),
            _const_spec((1, 4, HEAD_DIM)),
            _const_spec((V_DIM, 1)),
        ],
        out_specs=pl.BlockSpec((1, D, tq), lambda b, i: (b, 0, i)),
        scratch_shapes=[
            pltpu.VMEM((2 * N_HEADS, 2 * V_DIM, tq), BF16),
            pltpu.VMEM((N_HEADS, tq, V_DIM), BF16),
            pltpu.VMEM((SCORE_LOOKAHEAD, 2, tq, tq), F32),
            pltpu.VMEM((2 * N_HEADS, 1, tq), F32),
            pltpu.VMEM((2 * N_HEADS, V_DIM + ONES_ROWS, tq), F32),
        ],
        compiler_params=pltpu.CompilerParams(
            dimension_semantics=("arbitrary", "arbitrary"), vmem_limit_bytes=VMEM_LIMIT),
        name="diff_attention",
    )(qt, k, vt, lam_vecs.reshape(1, 4, HEAD_DIM), subln_g.reshape(V_DIM, 1))


def _attention_heads(x, k_v, g, w_q, lam_vecs, subln_g, slopes, lam_init):
    k, vt = k_v
    qt = _q_proj(x, g, w_q)
    return _diff_attention(qt, k, vt, lam_vecs, subln_g, slopes, lam_init)


def kernel(x, a_norm, a_w_in, a_b_in, a_conv_w, a_conv_b, a_gate_w, a_gate_b, a_lambda, a_w_out,
           kv_norm, w_kv, b_norm, b_w_q, b_lam, b_subln, b_w_o,
           mlp_norm, mlp_w1, mlp_w2, final_norm):
    depth = mlp_w1.shape[0]
    n_a = a_w_in.shape[0]
    slopes = tuple(2.0 ** (-8.0 * (h + 1) / N_HEADS) for h in range(N_HEADS))
    k_v = None
    for l in range(depth):
        if l < n_a:
            x = _recurrent_mlp_layer(x, a_norm[l], a_w_in[l], a_b_in[l], a_conv_w[l], a_conv_b[l],
                                     a_gate_w[l], a_gate_b[l], a_lambda[l], a_w_out[l],
                                     mlp_norm[l], mlp_w1[l], mlp_w2[l])
        else:
            j = l - n_a
            lam_init = 0.8 - 0.6 * math.exp(-0.3 * l)
            heads = _attention_heads(x, k_v, b_norm[j], b_w_q[j], b_lam[j], b_subln[j], slopes, lam_init)
            x = _mlp_layer(x, mlp_norm[l], mlp_w1[l], mlp_w2[l], final_norm,
                           final_norm=(l == depth - 1), attn=(heads, b_w_o[j]))
        if l == n_a - 1:
            k_v = _kv_proj(x, kv_norm, w_kv)
    return x
```

```python
import functools
import math

import jax
import jax.numpy as jnp
import numpy as np
from jax import lax
from jax.experimental import pallas as pl
from jax.experimental.pallas import tpu as pltpu

D_MODEL = 1024
N_HEADS = 8
HEAD_DIM = 64
V_DIM = 128
LRU_BLOCKS = 8
LRU_BLOCK_W = 128
CONV_W = 4
LRU_C = 8.0
NORM_EPS = 1e-6
SUBLN_EPS = 1e-5

SUBLANES = 8
ONES_ROWS = 16
SCORE_LOOKAHEAD = 4
KV_UNROLL = 4
LOG2E = math.log2(math.e)
REC_COLS = 256
REC_GATE_LOOKAHEAD = 2
PROJ_ROWS = 256
VMEM_LIMIT = 56 * 1024 * 1024

BF16 = jnp.bfloat16
F32 = jnp.float32


def _rmsnorm(x, g, eps):
    return x * lax.rsqrt(jnp.mean(x * x, axis=-1, keepdims=True) + eps) * g


def _bf16(packed):
    return pltpu.bitcast(packed, BF16)


def _const_spec(shape):
    nd = len(shape)
    return pl.BlockSpec(shape, lambda *_: (0,) * nd, pipeline_mode=pl.Buffered(1))


def _gelu_tanh(x):
    c1 = math.sqrt(2.0 / math.pi)
    return x * (0.5 + 0.5 * jnp.tanh(x * (c1 + (c1 * 0.044715) * (x * x))))


def _rec_mlp_kernel(x_ref, g_ref, wx_ref, wy_ref, bx_ref, by_ref, cw_ref, cb_ref, gw_ref, gb_ref,
                    lam_ref, wo_ref, perm_ref, unperm_ref, xres_ref, mg_ref, w1_ref, w2_ref, o_ref,
                    hn_s, xb_s, yb_s, xmid_s, xpad_ref, tail_ref, h_ref, *, ts, tiles_per_seq):
    f = pl.program_id(0)
    W = D_MODEL
    G = ts // SUBLANES
    HALO = (CONV_W - 1) * SUBLANES
    NB = W // REC_COLS
    starts_sequence = lax.rem(f - 2, tiles_per_seq) == 0

    @pl.when(f == 0)
    def _():
        hn_s[...] = jnp.zeros_like(hn_s)
        xb_s[...] = jnp.zeros_like(xb_s)
        yb_s[...] = jnp.zeros_like(yb_s)
        xmid_s[...] = jnp.zeros_like(xmid_s)
        tail_ref[...] = jnp.zeros_like(tail_ref)
        h_ref[...] = jnp.zeros_like(h_ref)

    row8 = lax.broadcasted_iota(jnp.int32, (SUBLANES, REC_COLS), 0)

    def norm_permute():
        hn = _rmsnorm(x_ref[0], g_ref[...], NORM_EPS).astype(BF16)
        hn_s[...] = jnp.dot(perm_ref[...], hn, preferred_element_type=F32).astype(BF16)

    def in_proj(cb):
        cs = slice(cb * REC_COLS, (cb + 1) * REC_COLS)
        hn = hn_s[...]
        xb_s[:, cs] = jnp.dot(hn, wx_ref[:, cs], preferred_element_type=F32) + bx_ref[:, cs]
        yb_s[:, cs] = jnp.dot(hn, wy_ref[:, cs], preferred_element_type=F32) + by_ref[:, cs]

    def load_proj(cb):
        cs = slice(cb * REC_COLS, (cb + 1) * REC_COLS)
        return xb_s[:, cs], yb_s[:, cs]

    def conv_gates(cb, xb, yb):
        cs = slice(cb * REC_COLS, (cb + 1) * REC_COLS)
        for k in range(CONV_W - 1):
            cur = xb[ts - HALO + k * SUBLANES:ts - HALO + (k + 1) * SUBLANES, :]
            prev = jnp.where(starts_sequence, 0.0, tail_ref[k * SUBLANES:(k + 1) * SUBLANES, cs])
            xpad_ref[k * SUBLANES:(k + 1) * SUBLANES, cs] = jnp.where(
                row8 == 0, pltpu.roll(prev, 1, axis=0), pltpu.roll(cur, 1, axis=0))
        tail_ref[:, cs] = xb[ts - HALO:, :]
        xpad_ref[HALO:, cs] = xb
        xc = xb * cw_ref[CONV_W - 1:CONV_W, cs] + cb_ref[:, cs]
        for j in range(CONV_W - 1):
            xc = xc + xpad_ref[j * SUBLANES:j * SUBLANES + ts, cs] * cw_ref[j:j + 1, cs]
        xcb = xc.astype(BF16)
        r_parts, i_parts = [], []
        for n in range(REC_COLS // LRU_BLOCK_W):
            gn = jnp.dot(xcb[:, n * LRU_BLOCK_W:(n + 1) * LRU_BLOCK_W],
                         gw_ref[cb * (REC_COLS // LRU_BLOCK_W) + n], preferred_element_type=F32)
            r_parts.append(gn[:, :LRU_BLOCK_W])
            i_parts.append(gn[:, LRU_BLOCK_W:])
        return xc, jnp.concatenate(r_parts, axis=1), jnp.concatenate(i_parts, axis=1), yb

    def recurrence(cb, xc, g_r, g_i, yb):
        cs = slice(cb * REC_COLS, (cb + 1) * REC_COLS)
        r = jax.nn.sigmoid(g_r + gb_ref[0:1, cs])
        ig = jax.nn.sigmoid(g_i + gb_ref[1:2, cs])
        z = -lam_ref[:, cs]
        softplus = jnp.maximum(z, 0.0) + jnp.log1p(jnp.exp(-jnp.abs(z)))
        log_a = (-LRU_C * softplus) * r
        a = jnp.exp(log_a)
        bx = jnp.sqrt(-jnp.tanh(log_a) * (a * a + 1.0)) * (ig * xc)

        h_loc = jnp.zeros((SUBLANES, REC_COLS), F32)
        prod = jnp.ones((SUBLANES, REC_COLS), F32)
        h_steps, p_steps = [], []
        for g in range(G):
            a_g = a[g * SUBLANES:(g + 1) * SUBLANES, :]
            h_loc = a_g * h_loc + bx[g * SUBLANES:(g + 1) * SUBLANES, :]
            prod = a_g * prod
            h_steps.append(h_loc)
            p_steps.append(prod)

        t_cum, f_cum = prod, h_loc
        for d in (1, 2, 4):
            keep = row8 >= d
            f_cum = jnp.where(keep, t_cum * pltpu.roll(f_cum, d, axis=0) + f_cum, f_cum)
            t_cum = jnp.where(keep, t_cum * pltpu.roll(t_cum, d, axis=0), t_cum)
        h_prev = jnp.where(starts_sequence, 0.0, h_ref[SUBLANES - 1:SUBLANES, cs])
        h_end = t_cum * h_prev + f_cum
        h_in = jnp.where(row8 == 0, h_prev, pltpu.roll(h_end, 1, axis=0))
        h_ref[:, cs] = h_end
        hs = jnp.concatenate([h_steps[g] + p_steps[g] * h_in for g in range(G)], axis=0)
        return (hs * _gelu_tanh(yb)).astype(BF16)

    def unpermute(y):
        return jnp.dot(unperm_ref[...], y, preferred_element_type=F32).astype(BF16)

    def out_proj(cb, y):
        return jnp.dot(y, wo_ref[cb * REC_COLS:(cb + 1) * REC_COLS, :], preferred_element_type=F32)

    F = w1_ref.shape[1]
    fchunk = F // NB
    xm = xmid_s[...]
    hm = _rmsnorm(xm, mg_ref[...], NORM_EPS).astype(BF16)

    def mlp_up(fc):
        fs = slice(fc * fchunk, (fc + 1) * fchunk)
        u = jnp.maximum(jnp.dot(hm, w1_ref[:, fs], preferred_element_type=F32), 0.0)
        return (u * u).astype(BF16)

    def mlp_down(fc, u):
        return jnp.dot(u, w2_ref[fc * fchunk:(fc + 1) * fchunk, :], preferred_element_type=F32)

    loaded = {cb: load_proj(cb) for cb in range(NB)}
    for cb in range(NB):
        in_proj(cb)
    norm_permute()
    up = {0: mlp_up(0)}
    gated = {cb: conv_gates(cb, *loaded.pop(cb)) for cb in range(REC_GATE_LOOKAHEAD)}
    out = xres_ref[0]
    mlp_out = xm
    y_nat = {}
    for cb in range(NB):
        if cb + 1 < NB:
            up[cb + 1] = mlp_up(cb + 1)
        mlp_out = mlp_out + mlp_down(cb, up.pop(cb))
        nxt = cb + REC_GATE_LOOKAHEAD
        if nxt < NB:
            gated[nxt] = conv_gates(nxt, *loaded.pop(nxt))
        y_nat[cb] = unpermute(recurrence(cb, *gated.pop(cb)))
        if cb > 0:
            out = out + out_proj(cb - 1, y_nat.pop(cb - 1))
    out = out + out_proj(NB - 1, y_nat.pop(NB - 1))
    o_ref[0] = mlp_out
    xmid_s[...] = out


def _chunk_permutation(ts):
    steps = ts // SUBLANES
    p = np.zeros((ts, ts), np.float32)
    for c in range(SUBLANES):
        for s in range(steps):
            p[s * SUBLANES + c, c * steps + s] = 1.0
    return p


def _recurrent_mlp_layer(x, g, w_in, b_in, conv_w, conv_b, gate_w, gate_b, lam, w_out,
                         mlp_g, w1, w2, *, ts=256):
    B, S, D = x.shape
    F = w1.shape[1]
    W = D_MODEL
    n_seq = S // ts
    n_tiles = B * n_seq
    wx = w_in[:, :W].astype(BF16)
    wy = w_in[:, W:].astype(BF16)
    bxv = b_in[:W].reshape(1, W)
    byv = b_in[W:].reshape(1, W)
    gw = jnp.concatenate([gate_w[0], gate_w[1]], axis=-1).astype(BF16)
    perm = _chunk_permutation(ts)
    kern = functools.partial(_rec_mlp_kernel, ts=ts, tiles_per_seq=n_seq)

    def in_tile(f):
        t = jnp.minimum(f, n_tiles - 1)
        return (t // n_seq, t % n_seq, 0)

    def res_tile(f):
        t = jnp.clip(f - 2, 0, n_tiles - 1)
        return (t // n_seq, t % n_seq, 0)

    def out_tile(f):
        t = jnp.maximum(f - 3, 0)
        return (t // n_seq, t % n_seq, 0)

    return pl.pallas_call(
        kern,
        out_shape=jax.ShapeDtypeStruct((B, S, D), F32),
        grid=(n_tiles + 3,),
        in_specs=[
            pl.BlockSpec((1, ts, D), in_tile),
            _const_spec((1, D)),
            _const_spec((D, W)), _const_spec((D, W)),
            _const_spec((1, W)), _const_spec((1, W)),
            _const_spec((CONV_W, W)), _const_spec((1, W)),
            _const_spec((LRU_BLOCKS, LRU_BLOCK_W, 2 * LRU_BLOCK_W)),
            _const_spec((2, W)),
            _const_spec((1, W)),
            _const_spec((W, D)),
            _const_spec((ts, ts)), _const_spec((ts, ts)),
            pl.BlockSpec((1, ts, D), res_tile),
            _const_spec((1, D)), _const_spec((D, F)), _const_spec((F, D)),
        ],
        out_specs=pl.BlockSpec((1, ts, D), out_tile),
        scratch_shapes=[
            pltpu.VMEM((ts, D), BF16),
            pltpu.VMEM((ts, W), F32),
            pltpu.VMEM((ts, W), F32),
            pltpu.VMEM((ts, D), F32),
            pltpu.VMEM((ts + (CONV_W - 1) * SUBLANES, W), F32),
            pltpu.VMEM(((CONV_W - 1) * SUBLANES, W), F32),
            pltpu.VMEM((SUBLANES, W), F32),
        ],
        compiler_params=pltpu.CompilerParams(
            dimension_semantics=("arbitrary",), vmem_limit_bytes=VMEM_LIMIT),
        name="rglru_mlp_layer",
    )(x, g.reshape(1, D), wx, wy, bxv, byv, conv_w, conv_b.reshape(1, W), gw, gate_b,
      lam.reshape(1, W), w_out.astype(BF16), jnp.asarray(perm, BF16), jnp.asarray(perm.T, BF16), x,
      mlp_g.reshape(1, D), w1.astype(BF16), w2.astype(BF16))


def _mlp_halves(xs, g_ref, w1_ref, w2_ref, fg_ref, o_ref, final_norm):
    hs = [_rmsnorm(x, g_ref[...], NORM_EPS).astype(BF16) for x in xs]
    us = [jnp.maximum(jnp.dot(h, w1_ref[...], preferred_element_type=F32), 0.0) for h in hs]
    us = [(u * u).astype(BF16) for u in us]
    ys = [x + jnp.dot(u, w2_ref[...], preferred_element_type=F32) for x, u in zip(xs, us)]
    half = xs[0].shape[0]
    for i, y in enumerate(ys):
        if final_norm:
            y = _rmsnorm(y, fg_ref[...], NORM_EPS)
        o_ref[0, i * half:(i + 1) * half, :] = y


def _row_halves(n):
    return [slice(0, n // 2), slice(n // 2, n)]


def _mlp_kernel(x_ref, g_ref, w1_ref, w2_ref, fg_ref, o_ref, *, final_norm):
    xs = [x_ref[0, rows, :] for rows in _row_halves(x_ref.shape[1])]
    _mlp_halves(xs, g_ref, w1_ref, w2_ref, fg_ref, o_ref, final_norm)


def _attn_out_mlp_kernel(at_ref, wo_ref, x_ref, g_ref, w1_ref, w2_ref, fg_ref, o_ref, *, final_norm):
    xs = [x_ref[0, rows, :] + lax.dot_general(at_ref[0, :, rows], wo_ref[...], (((0,), (0,)), ((), ())),
                                              preferred_element_type=F32)
          for rows in _row_halves(x_ref.shape[1])]
    _mlp_halves(xs, g_ref, w1_ref, w2_ref, fg_ref, o_ref, final_norm)


def _mlp_layer(x, g, w1, w2, final_g, *, final_norm, attn=None, tm=512):
    B, S, D = x.shape
    F = w1.shape[1]
    row_spec = pl.BlockSpec((1, tm, D), lambda b, s: (b, s, 0))
    specs = [row_spec, _const_spec((1, D)), _const_spec((D, F)), _const_spec((F, D)),
             _const_spec((1, D))]
    args = [x, g.reshape(1, D), w1.astype(BF16), w2.astype(BF16), final_g.reshape(1, D)]
    kern = _mlp_kernel
    if attn is not None:
        at, w_o = attn
        specs = [pl.BlockSpec((1, D, tm), lambda b, s: (b, 0, s)), _const_spec((D, D))] + specs
        args = [at, w_o.astype(BF16)] + args
        kern = _attn_out_mlp_kernel
    return pl.pallas_call(
        functools.partial(kern, final_norm=final_norm),
        out_shape=jax.ShapeDtypeStruct((B, S, D), F32),
        grid=(B, S // tm),
        in_specs=specs,
        out_specs=row_spec,
        compiler_params=pltpu.CompilerParams(
            dimension_semantics=("parallel", "parallel"), vmem_limit_bytes=VMEM_LIMIT),
        name="mlp_layer",
    )(*args)


def _kv_kernel(x_ref, g_ref, wk_ref, wv_ref, k_ref, vt_ref):
    for r0 in range(0, x_ref.shape[1], PROJ_ROWS):
        rows = slice(r0, r0 + PROJ_ROWS)
        hn = _rmsnorm(x_ref[0, rows, :], g_ref[...], NORM_EPS).astype(BF16)
        k = jnp.dot(hn, wk_ref[...], preferred_element_type=F32).astype(BF16)
        k_ref[0, r0 // 2:(r0 + PROJ_ROWS) // 2, :] = pltpu.bitcast(k, jnp.uint32)
        v = jnp.dot(hn, wv_ref[...], preferred_element_type=F32)
        vt_ref[0, :, rows] = pltpu.bitcast(v.T.astype(BF16), jnp.uint32)


def _kv_proj(x, g, w_kv, *, tm=1024):
    B, S, D = x.shape
    return pl.pallas_call(
        _kv_kernel,
        out_shape=(jax.ShapeDtypeStruct((B, S // 2, D), jnp.uint32),
                   jax.ShapeDtypeStruct((B, D // 2, S), jnp.uint32)),
        grid=(B, S // tm),
        in_specs=[
            pl.BlockSpec((1, tm, D), lambda b, s: (b, s, 0)),
            _const_spec((1, D)), _const_spec((D, D)), _const_spec((D, D)),
        ],
        out_specs=(pl.BlockSpec((1, tm // 2, D), lambda b, s: (b, s, 0)),
                   pl.BlockSpec((1, D // 2, tm), lambda b, s: (b, 0, s))),
        compiler_params=pltpu.CompilerParams(
            dimension_semantics=("parallel", "parallel"), vmem_limit_bytes=VMEM_LIMIT),
        name="kv_proj",
    )(x, g.reshape(1, D), w_kv[:, :D].astype(BF16), w_kv[:, D:].astype(BF16))


def _q_kernel(x_ref, g_ref, wq_ref, qt_ref, *, scale):
    for r0 in range(0, x_ref.shape[1], PROJ_ROWS):
        rows = slice(r0, r0 + PROJ_ROWS)
        hn = _rmsnorm(x_ref[0, rows, :], g_ref[...], NORM_EPS).astype(BF16)
        q = jnp.dot(hn, wq_ref[...], preferred_element_type=F32) * scale
        qt_ref[0, :, rows] = q.T.astype(BF16)


def _q_proj(x, g, w_q, *, tm=1024):
    B, S, D = x.shape
    kern = functools.partial(_q_kernel, scale=LOG2E * HEAD_DIM ** -0.5)
    return pl.pallas_call(
        kern,
        out_shape=jax.ShapeDtypeStruct((B, D, S), BF16),
        grid=(B, S // tm),
        in_specs=[
            pl.BlockSpec((1, tm, D), lambda b, s: (b, s, 0)),
            _const_spec((1, D)), _const_spec((D, D)),
        ],
        out_specs=pl.BlockSpec((1, D, tm), lambda b, s: (b, 0, s)),
        compiler_params=pltpu.CompilerParams(
            dimension_semantics=("parallel", "parallel"), vmem_limit_bytes=VMEM_LIMIT),
        name="q_proj",
    )(x, g.reshape(1, D), w_q.astype(BF16))


def _bf16_terms(c, n=3):
    terms, rest = [], np.float32(c)
    for _ in range(n):
        t = np.float32(np.asarray(rest, dtype=BF16))
        terms.append(float(t))
        rest = np.float32(rest - t)
    return terms


def _attn_kernel(qt_ref, k_ref, vt_ref, lam_ref, sg_ref, ot_ref,
                 qaug_ref, kaug_ref, sc_ref, m_ref, acc_ref, *, tq, lam_init, slopes):
    qi = pl.program_id(1)
    tk = tq
    LA = SCORE_LOOKAHEAD
    log2_slopes = [s * LOG2E for s in slopes]

    @pl.when((pl.program_id(0) == 0) & (qi == 0))
    def _():
        arow = lax.broadcasted_iota(jnp.int32, (V_DIM, tq), 0)
        dq_row = lax.broadcasted_iota(jnp.int32, (V_DIM, tq), 1).astype(F32)
        col = lax.broadcasted_iota(jnp.int32, (tk, V_DIM), 1)
        dk_col = lax.broadcasted_iota(jnp.int32, (tk, V_DIM), 0).astype(F32)
        for h in range(N_HEADS):
            c = _bf16_terms(log2_slopes[h])
            q_aug = jnp.where(arow == 0, c[0], jnp.where(arow == 1, c[1], jnp.where(
                arow == 2, c[2], jnp.where(arow < 6, dq_row, 0.0)))).astype(BF16)
            kaug_ref[h] = jnp.where(col < 3, dk_col, jnp.where(col == 3, -c[0], jnp.where(
                col == 4, -c[1], jnp.where(col == 5, -c[2], 0.0)))).astype(BF16)
            qaug_ref[2 * h, V_DIM:, :] = q_aug
            qaug_ref[2 * h + 1, V_DIM:, :] = q_aug

    row = lax.broadcasted_iota(jnp.int32, (V_DIM, tq), 0)
    for h in range(N_HEADS):
        qt = qt_ref[0, h * V_DIM:(h + 1) * V_DIM, :]
        zero = jnp.zeros_like(qt)
        qaug_ref[2 * h, 0:V_DIM, :] = jnp.where(row < HEAD_DIM, qt, zero)
        qaug_ref[2 * h + 1, 0:V_DIM, :] = jnp.where(row >= HEAD_DIM, qt, zero)

    ones_rows = jnp.ones((ONES_ROWS, tk), BF16)

    def issue_scores(j, h):
        off2 = pl.multiple_of(j * (tk // 2), tk // 2)
        kt = _bf16(k_ref[0, pl.ds(off2, tk // 2), h * V_DIM:(h + 1) * V_DIM])
        lhs = jnp.concatenate([kt, kaug_ref[h]], axis=1)
        for c in range(2):
            sc_ref[h % LA, c] = jnp.dot(lhs, qaug_ref[2 * h + c], preferred_element_type=F32)

    def softmax_pv(j, h, future):
        off = pl.multiple_of(j * tk, tk)
        vt = _bf16(vt_ref[0, h * V_DIM // 2:(h + 1) * V_DIM // 2, pl.ds(off, tk)])
        vta = jnp.concatenate([vt, ones_rows], axis=0)
        tile_bias = -log2_slopes[h] * ((qi - j) * tk).astype(F32)
        for c in range(2):
            idx = 2 * h + c
            sc = sc_ref[h % LA, c]
            if future is not None:
                sc = jnp.where(future, -jnp.inf, sc)
            m_old = m_ref[idx]
            m_new = jnp.maximum(m_old, jnp.max(sc, axis=0, keepdims=True) + tile_bias)
            alpha = jnp.exp2(m_old - m_new)
            p = jnp.exp2(sc - (m_new - tile_bias)).astype(BF16)
            acc_ref[idx] = alpha * acc_ref[idx] + jnp.dot(vta, p, preferred_element_type=F32)
            m_ref[idx] = m_new

    for h in range(LA):
        issue_scores(0, h)
    m_ref[...] = jnp.full(m_ref.shape, -jnp.inf, F32)
    acc_ref[...] = jnp.zeros(acc_ref.shape, F32)

    def unmasked_tile(j):
        for h in range(N_HEADS):
            softmax_pv(j, h, None)
            if h + LA < N_HEADS:
                issue_scores(j, h + LA)
            else:
                issue_scores(j + 1, h + LA - N_HEADS)

    def tile_group(i, carry):
        for t in range(KV_UNROLL):
            unmasked_tile(KV_UNROLL * i + t)
        return carry

    lax.fori_loop(0, qi // KV_UNROLL, tile_group, 0)

    def tail(n_unmasked):
        base = qi - n_unmasked
        for t in range(n_unmasked):
            unmasked_tile(base + t)
        future = (lax.broadcasted_iota(jnp.int32, (tk, tq), 0)
                  > lax.broadcasted_iota(jnp.int32, (tk, tq), 1))
        for h in range(N_HEADS):
            softmax_pv(qi, h, future)
            if h + LA < N_HEADS:
                issue_scores(qi, h + LA)

    for r in range(KV_UNROLL):
        pl.when(qi % KV_UNROLL == r)(functools.partial(tail, r))

    lv = lam_ref[0]
    lam = (jnp.exp(jnp.sum(lv[0:1] * lv[1:2], keepdims=True))
           - jnp.exp(jnp.sum(lv[2:3] * lv[3:4], keepdims=True)) + lam_init)
    for h in range(N_HEADS):
        a1 = acc_ref[2 * h]
        a2 = acc_ref[2 * h + 1]
        o = (a1[0:V_DIM] / a1[V_DIM:V_DIM + 1]) - lam * (a2[0:V_DIM] / a2[V_DIM:V_DIM + 1])
        o = o * lax.rsqrt(jnp.mean(o * o, axis=0, keepdims=True) + SUBLN_EPS) * sg_ref[...]
        ot_ref[0, h * V_DIM:(h + 1) * V_DIM, :] = (o * (1.0 - lam_init)).astype(BF16)


def _diff_attention(qt, k, vt, lam_vecs, subln_g, slopes, lam_init, *, tq=256):
    B, D, S = qt.shape
    assert N_HEADS % SCORE_LOOKAHEAD == 0 and tq <= 256
    kern = functools.partial(_attn_kernel, tq=tq, lam_init=lam_init, slopes=slopes)
    return pl.pallas_call(
        kern,
        out_shape=jax.ShapeDtypeStruct((B, D, S), BF16),
        grid=(B, S // tq),
        in_specs=[
            pl.BlockSpec((1, D, tq), lambda b, i: (b, 0, i)),
            pl.BlockSpec((1, S // 2, D), lambda b, i: (b, 0, 0)),
            pl.BlockSpec((1, D // 2, S), lambda b, i: (b, 0, 0)),
            _const_spec((1, 4, HEAD_DIM)),
            _const_spec((V_DIM, 1)),
        ],
        out_specs=pl.BlockSpec((1, D, tq), lambda b, i: (b, 0, i)),
        scratch_shapes=[
            pltpu.VMEM((2 * N_HEADS, 2 * V_DIM, tq), BF16),
            pltpu.VMEM((N_HEADS, tq, V_DIM), BF16),
            pltpu.VMEM((SCORE_LOOKAHEAD, 2, tq, tq), F32),
            pltpu.VMEM((2 * N_HEADS, 1, tq), F32),
            pltpu.VMEM((2 * N_HEADS, V_DIM + ONES_ROWS, tq), F32),
        ],
        compiler_params=pltpu.CompilerParams(
            dimension_semantics=("arbitrary", "arbitrary"), vmem_limit_bytes=VMEM_LIMIT),
        name="diff_attention",
    )(qt, k, vt, lam_vecs.reshape(1, 4, HEAD_DIM), subln_g.reshape(V_DIM, 1))


def _attention_heads(x, k_v, g, w_q, lam_vecs, subln_g, slopes, lam_init):
    k, vt = k_v
    qt = _q_proj(x, g, w_q)
    return _diff_attention(qt, k, vt, lam_vecs, subln_g, slopes, lam_init)


def kernel(x, a_norm, a_w_in, a_b_in, a_conv_w, a_conv_b, a_gate_w, a_gate_b, a_lambda, a_w_out,
           kv_norm, w_kv, b_norm, b_w_q, b_lam, b_subln, b_w_o,
           mlp_norm, mlp_w1, mlp_w2, final_norm):
    depth = mlp_w1.shape[0]
    n_a = a_w_in.shape[0]
    slopes = tuple(2.0 ** (-8.0 * (h + 1) / N_HEADS) for h in range(N_HEADS))
    k_v = None
    for l in range(depth):
        if l < n_a:
            x = _recurrent_mlp_layer(x, a_norm[l], a_w_in[l], a_b_in[l], a_conv_w[l], a_conv_b[l],
                                     a_gate_w[l], a_gate_b[l], a_lambda[l], a_w_out[l],
                                     mlp_norm[l], mlp_w1[l], mlp_w2[l])
        else:
            j = l - n_a
            lam_init = 0.8 - 0.6 * math.exp(-0.3 * l)
            heads = _attention_heads(x, k_v, b_norm[j], b_w_q[j], b_lam[j], b_subln[j], slopes, lam_init)
            x = _mlp_layer(x, mlp_norm[l], mlp_w1[l], mlp_w2[l], final_norm,
                           final_norm=(l == depth - 1), attn=(heads, b_w_o[j]))
        if l == n_a - 1:
            k_v = _kv_proj(x, kv_norm, w_kv)
    return x
```

```python
import functools
import math

import jax
import jax.numpy as jnp
import numpy as np
from jax import lax
from jax.experimental import pallas as pl
from jax.experimental.pallas import tpu as pltpu

D_MODEL = 1024
N_HEADS = 8
HEAD_DIM = 64
V_DIM = 128
LRU_BLOCKS = 8
LRU_BLOCK_W = 128
CONV_W = 4
LRU_C = 8.0
NORM_EPS = 1e-6
SUBLN_EPS = 1e-5

SUBLANES = 8
ONES_ROWS = 16
SCORE_LOOKAHEAD = 4
KV_UNROLL = 4
LOG2E = math.log2(math.e)
REC_COLS = 256
REC_GATE_LOOKAHEAD = 2
PROJ_ROWS = 256
VMEM_LIMIT = 56 * 1024 * 1024

BF16 = jnp.bfloat16
F32 = jnp.float32


def _rmsnorm(x, g, eps):
    return x * lax.rsqrt(jnp.mean(x * x, axis=-1, keepdims=True) + eps) * g


def _bf16(packed):
    return pltpu.bitcast(packed, BF16)


def _const_spec(shape):
    nd = len(shape)
    return pl.BlockSpec(shape, lambda *_: (0,) * nd, pipeline_mode=pl.Buffered(1))


def _gelu_tanh(x):
    c1 = math.sqrt(2.0 / math.pi)
    return x * (0.5 + 0.5 * jnp.tanh(x * (c1 + (c1 * 0.044715) * (x * x))))


def _rec_mlp_kernel(x_ref, g_ref, wx_ref, wy_ref, bx_ref, by_ref, cw_ref, cb_ref, gw_ref, gb_ref,
                    lam_ref, wo_ref, perm_ref, unperm_ref, xres_ref, mg_ref, w1_ref, w2_ref, o_ref,
                    hn_s, xb_s, yb_s, xmid_s, xpad_ref, tail_ref, h_ref, *, ts, tiles_per_seq):
    f = pl.program_id(0)
    W = D_MODEL
    G = ts // SUBLANES
    HALO = (CONV_W - 1) * SUBLANES
    NB = W // REC_COLS
    starts_sequence = lax.rem(f - 2, tiles_per_seq) == 0

    @pl.when(f == 0)
    def _():
        hn_s[...] = jnp.zeros_like(hn_s)
        xb_s[...] = jnp.zeros_like(xb_s)
        yb_s[...] = jnp.zeros_like(yb_s)
        xmid_s[...] = jnp.zeros_like(xmid_s)
        tail_ref[...] = jnp.zeros_like(tail_ref)
        h_ref[...] = jnp.zeros_like(h_ref)

    row8 = lax.broadcasted_iota(jnp.int32, (SUBLANES, REC_COLS), 0)

    def norm_permute():
        hn = _rmsnorm(x_ref[0], g_ref[...], NORM_EPS).astype(BF16)
        hn_s[...] = jnp.dot(perm_ref[...], hn, preferred_element_type=F32).astype(BF16)

    def in_proj(cb):
        cs = slice(cb * REC_COLS, (cb + 1) * REC_COLS)
        hn = hn_s[...]
        xb_s[:, cs] = jnp.dot(hn, wx_ref[:, cs], preferred_element_type=F32) + bx_ref[:, cs]
        yb_s[:, cs] = jnp.dot(hn, wy_ref[:, cs], preferred_element_type=F32) + by_ref[:, cs]

    def load_proj(cb):
        cs = slice(cb * REC_COLS, (cb + 1) * REC_COLS)
        return xb_s[:, cs], yb_s[:, cs]

    def conv_gates(cb, xb, yb):
        cs = slice(cb * REC_COLS, (cb + 1) * REC_COLS)
        for k in range(CONV_W - 1):
            cur = xb[ts - HALO + k * SUBLANES:ts - HALO + (k + 1) * SUBLANES, :]
            prev = jnp.where(starts_sequence, 0.0, tail_ref[k * SUBLANES:(k + 1) * SUBLANES, cs])
            xpad_ref[k * SUBLANES:(k + 1) * SUBLANES, cs] = jnp.where(
                row8 == 0, pltpu.roll(prev, 1, axis=0), pltpu.roll(cur, 1, axis=0))
        tail_ref[:, cs] = xb[ts - HALO:, :]
        xpad_ref[HALO:, cs] = xb
        xc = xb * cw_ref[CONV_W - 1:CONV_W, cs] + cb_ref[:, cs]
        for j in range(CONV_W - 1):
            xc = xc + xpad_ref[j * SUBLANES:j * SUBLANES + ts, cs] * cw_ref[j:j + 1, cs]
        xcb = xc.astype(BF16)
        r_parts, i_parts = [], []
        for n in range(REC_COLS // LRU_BLOCK_W):
            gn = jnp.dot(xcb[:, n * LRU_BLOCK_W:(n + 1) * LRU_BLOCK_W],
                         gw_ref[cb * (REC_COLS // LRU_BLOCK_W) + n], preferred_element_type=F32)
            r_parts.append(gn[:, :LRU_BLOCK_W])
            i_parts.append(gn[:, LRU_BLOCK_W:])
        return xc, jnp.concatenate(r_parts, axis=1), jnp.concatenate(i_parts, axis=1), yb

    def recurrence(cb, xc, g_r, g_i, yb):
        cs = slice(cb * REC_COLS, (cb + 1) * REC_COLS)
        r = jax.nn.sigmoid(g_r + gb_ref[0:1, cs])
        ig = jax.nn.sigmoid(g_i + gb_ref[1:2, cs])
        z = -lam_ref[:, cs]
        softplus = jnp.maximum(z, 0.0) + jnp.log1p(jnp.exp(-jnp.abs(z)))
        log_a = (-LRU_C * softplus) * r
        a = jnp.exp(log_a)
        bx = jnp.sqrt(-jnp.tanh(log_a) * (a * a + 1.0)) * (ig * xc)

        h_loc = jnp.zeros((SUBLANES, REC_COLS), F32)
        prod = jnp.ones((SUBLANES, REC_COLS), F32)
        h_steps, p_steps = [], []
        for g in range(G):
            a_g = a[g * SUBLANES:(g + 1) * SUBLANES, :]
            h_loc = a_g * h_loc + bx[g * SUBLANES:(g + 1) * SUBLANES, :]
            prod = a_g * prod
            h_steps.append(h_loc)
            p_steps.append(prod)

        t_cum, f_cum = prod, h_loc
        for d in (1, 2, 4):
            keep = row8 >= d
            f_cum = jnp.where(keep, t_cum * pltpu.roll(f_cum, d, axis=0) + f_cum, f_cum)
            t_cum = jnp.where(keep, t_cum * pltpu.roll(t_cum, d, axis=0), t_cum)
        h_prev = jnp.where(starts_sequence, 0.0, h_ref[SUBLANES - 1:SUBLANES, cs])
        h_end = t_cum * h_prev + f_cum
        h_in = jnp.where(row8 == 0, h_prev, pltpu.roll(h_end, 1, axis=0))
        h_ref[:, cs] = h_end
        hs = jnp.concatenate([h_steps[g] + p_steps[g] * h_in for g in range(G)], axis=0)
        return (hs * _gelu_tanh(yb)).astype(BF16)

    def unpermute(y):
        return jnp.dot(unperm_ref[...], y, preferred_element_type=F32).astype(BF16)

    def out_proj(cb, y):
        return jnp.dot(y, wo_ref[cb * REC_COLS:(cb + 1) * REC_COLS, :], preferred_element_type=F32)

    F = w1_ref.shape[1]
    fchunk = F // NB
    xm = xmid_s[...]
    hm = _rmsnorm(xm, mg_ref[...], NORM_EPS).astype(BF16)

    def mlp_up(fc):
        fs = slice(fc * fchunk, (fc + 1) * fchunk)
        u = jnp.maximum(jnp.dot(hm, w1_ref[:, fs], preferred_element_type=F32), 0.0)
        return (u * u).astype(BF16)

    def mlp_down(fc, u):
        return jnp.dot(u, w2_ref[fc * fchunk:(fc + 1) * fchunk, :], preferred_element_type=F32)

    loaded = {cb: load_proj(cb) for cb in range(NB)}
    for cb in range(NB):
        in_proj(cb)
    norm_permute()
    up = {0: mlp_up(0)}
    gated = {cb: conv_gates(cb, *loaded.pop(cb)) for cb in range(REC_GATE_LOOKAHEAD)}
    out = xres_ref[0]
    mlp_out = xm
    y_nat = {}
    for cb in range(NB):
        if cb + 1 < NB:
            up[cb + 1] = mlp_up(cb + 1)
        mlp_out = mlp_out + mlp_down(cb, up.pop(cb))
        nxt = cb + REC_GATE_LOOKAHEAD
        if nxt < NB:
            gated[nxt] = conv_gates(nxt, *loaded.pop(nxt))
        y_nat[cb] = unpermute(recurrence(cb, *gated.pop(cb)))
        if cb > 0:
            out = out + out_proj(cb - 1, y_nat.pop(cb - 1))
    out = out + out_proj(NB - 1, y_nat.pop(NB - 1))
    o_ref[0] = mlp_out
    xmid_s[...] = out


def _chunk_permutation(ts):
    steps = ts // SUBLANES
    p = np.zeros((ts, ts), np.float32)
    for c in range(SUBLANES):
        for s in range(steps):
            p[s * SUBLANES + c, c * steps + s] = 1.0
    return p


def _recurrent_mlp_layer(x, g, w_in, b_in, conv_w, conv_b, gate_w, gate_b, lam, w_out,
                         mlp_g, w1, w2, *, ts=256):
    B, S, D = x.shape
    F = w1.shape[1]
    W = D_MODEL
    n_seq = S // ts
    n_tiles = B * n_seq
    wx = w_in[:, :W].astype(BF16)
    wy = w_in[:, W:].astype(BF16)
    bxv = b_in[:W].reshape(1, W)
    byv = b_in[W:].reshape(1, W)
    gw = jnp.concatenate([gate_w[0], gate_w[1]], axis=-1).astype(BF16)
    perm = _chunk_permutation(ts)
    kern = functools.partial(_rec_mlp_kernel, ts=ts, tiles_per_seq=n_seq)

    def in_tile(f):
        t = jnp.minimum(f, n_tiles - 1)
        return (t // n_seq, t % n_seq, 0)

    def res_tile(f):
        t = jnp.clip(f - 2, 0, n_tiles - 1)
        return (t // n_seq, t % n_seq, 0)

    def out_tile(f):
        t = jnp.maximum(f - 3, 0)
        return (t // n_seq, t % n_seq, 0)

    return pl.pallas_call(
        kern,
        out_shape=jax.ShapeDtypeStruct((B, S, D), F32),
        grid=(n_tiles + 3,),
        in_specs=[
            pl.BlockSpec((1, ts, D), in_tile),
            _const_spec((1, D)),
            _const_spec((D, W)), _const_spec((D, W)),
            _const_spec((1, W)), _const_spec((1, W)),
            _const_spec((CONV_W, W)), _const_spec((1, W)),
            _const_spec((LRU_BLOCKS, LRU_BLOCK_W, 2 * LRU_BLOCK_W)),
            _const_spec((2, W)),
            _const_spec((1, W)),
            _const_spec((W, D)),
            _const_spec((ts, ts)), _const_spec((ts, ts)),
            pl.BlockSpec((1, ts, D), res_tile),
            _const_spec((1, D)), _const_spec((D, F)), _const_spec((F, D)),
        ],
        out_specs=pl.BlockSpec((1, ts, D), out_tile),
        scratch_shapes=[
            pltpu.VMEM((ts, D), BF16),
            pltpu.VMEM((ts, W), F32),
            pltpu.VMEM((ts, W), F32),
            pltpu.VMEM((ts, D), F32),
            pltpu.VMEM((ts + (CONV_W - 1) * SUBLANES, W), F32),
            pltpu.VMEM(((CONV_W - 1) * SUBLANES, W), F32),
            pltpu.VMEM((SUBLANES, W), F32),
        ],
        compiler_params=pltpu.CompilerParams(
            dimension_semantics=("arbitrary",), vmem_limit_bytes=VMEM_LIMIT),
        name="rglru_mlp_layer",
    )(x, g.reshape(1, D), wx, wy, bxv, byv, conv_w, conv_b.reshape(1, W), gw, gate_b,
      lam.reshape(1, W), w_out.astype(BF16), jnp.asarray(perm, BF16), jnp.asarray(perm.T, BF16), x,
      mlp_g.reshape(1, D), w1.astype(BF16), w2.astype(BF16))


def _mlp_halves(xs, g_ref, w1_ref, w2_ref, fg_ref, o_ref, final_norm):
    hs = [_rmsnorm(x, g_ref[...], NORM_EPS).astype(BF16) for x in xs]
    us = [jnp.maximum(jnp.dot(h, w1_ref[...], preferred_element_type=F32), 0.0) for h in hs]
    us = [(u * u).astype(BF16) for u in us]
    ys = [x + jnp.dot(u, w2_ref[...], preferred_element_type=F32) for x, u in zip(xs, us)]
    half = xs[0].shape[0]
    for i, y in enumerate(ys):
        if final_norm:
            y = _rmsnorm(y, fg_ref[...], NORM_EPS)
        o_ref[0, i * half:(i + 1) * half, :] = y


def _row_halves(n):
    return [slice(0, n // 2), slice(n // 2, n)]


def _mlp_kernel(x_ref, g_ref, w1_ref, w2_ref, fg_ref, o_ref, *, final_norm):
    xs = [x_ref[0, rows, :] for rows in _row_halves(x_ref.shape[1])]
    _mlp_halves(xs, g_ref, w1_ref, w2_ref, fg_ref, o_ref, final_norm)


def _attn_out_mlp_kernel(at_ref, wo_ref, x_ref, g_ref, w1_ref, w2_ref, fg_ref, o_ref, *, final_norm):
    xs = [x_ref[0, rows, :] + lax.dot_general(at_ref[0, :, rows], wo_ref[...], (((0,), (0,)), ((), ())),
                                              preferred_element_type=F32)
          for rows in _row_halves(x_ref.shape[1])]
    _mlp_halves(xs, g_ref, w1_ref, w2_ref, fg_ref, o_ref, final_norm)


def _mlp_layer(x, g, w1, w2, final_g, *, final_norm, attn=None, tm=512):
    B, S, D = x.shape
    F = w1.shape[1]
    row_spec = pl.BlockSpec((1, tm, D), lambda b, s: (b, s, 0))
    specs = [row_spec, _const_spec((1, D)), _const_spec((D, F)), _const_spec((F, D)),
             _const_spec((1, D))]
    args = [x, g.reshape(1, D), w1.astype(BF16), w2.astype(BF16), final_g.reshape(1, D)]
    kern = _mlp_kernel
    if attn is not None:
        at, w_o = attn
        specs = [pl.BlockSpec((1, D, tm), lambda b, s: (b, 0, s)), _const_spec((D, D))] + specs
        args = [at, w_o.astype(BF16)] + args
        kern = _attn_out_mlp_kernel
    return pl.pallas_call(
        functools.partial(kern, final_norm=final_norm),
        out_shape=jax.ShapeDtypeStruct((B, S, D), F32),
        grid=(B, S // tm),
        in_specs=specs,
        out_specs=row_spec,
        compiler_params=pltpu.CompilerParams(
            dimension_semantics=("parallel", "parallel"), vmem_limit_bytes=VMEM_LIMIT),
        name="mlp_layer",
    )(*args)


def _kv_kernel(x_ref, g_ref, wk_ref, wv_ref, k_ref, vt_ref):
    for r0 in range(0, x_ref.shape[1], PROJ_ROWS):
        rows = slice(r0, r0 + PROJ_ROWS)
        hn = _rmsnorm(x_ref[0, rows, :], g_ref[...], NORM_EPS).astype(BF16)
        k = jnp.dot(hn, wk_ref[...], preferred_element_type=F32).astype(BF16)
        k_ref[0, r0 // 2:(r0 + PROJ_ROWS) // 2, :] = pltpu.bitcast(k, jnp.uint32)
        v = jnp.dot(hn, wv_ref[...], preferred_element_type=F32)
        vt_ref[0, :, rows] = pltpu.bitcast(v.T.astype(BF16), jnp.uint32)


def _kv_proj(x, g, w_kv, *, tm=1024):
    B, S, D = x.shape
    return pl.pallas_call(
        _kv_kernel,
        out_shape=(jax.ShapeDtypeStruct((B, S // 2, D), jnp.uint32),
                   jax.ShapeDtypeStruct((B, D // 2, S), jnp.uint32)),
        grid=(B, S // tm),
        in_specs=[
            pl.BlockSpec((1, tm, D), lambda b, s: (b, s, 0)),
            _const_spec((1, D)), _const_spec((D, D)), _const_spec((D, D)),
        ],
        out_specs=(pl.BlockSpec((1, tm // 2, D), lambda b, s: (b, s, 0)),
                   pl.BlockSpec((1, D // 2, tm), lambda b, s: (b, 0, s))),
        compiler_params=pltpu.CompilerParams(
            dimension_semantics=("parallel", "parallel"), vmem_limit_bytes=VMEM_LIMIT),
        name="kv_proj",
    )(x, g.reshape(1, D), w_kv[:, :D].astype(BF16), w_kv[:, D:].astype(BF16))


def _q_kernel(x_ref, g_ref, wq_ref, qt_ref, *, scale):
    for r0 in range(0, x_ref.shape[1], PROJ_ROWS):
        rows = slice(r0, r0 + PROJ_ROWS)
        hn = _rmsnorm(x_ref[0, rows, :], g_ref[...], NORM_EPS).astype(BF16)
        q = jnp.dot(hn, wq_ref[...], preferred_element_type=F32) * scale
        qt_ref[0, :, rows] = q.T.astype(BF16)


def _q_proj(x, g, w_q, *, tm=1024):
    B, S, D = x.shape
    kern = functools.partial(_q_kernel, scale=LOG2E * HEAD_DIM ** -0.5)
    return pl.pallas_call(
        kern,
        out_shape=jax.ShapeDtypeStruct((B, D, S), BF16),
        grid=(B, S // tm),
        in_specs=[
            pl.BlockSpec((1, tm, D), lambda b, s: (b, s, 0)),
            _const_spec((1, D)), _const_spec((D, D)),
        ],
        out_specs=pl.BlockSpec((1, D, tm), lambda b, s: (b, 0, s)),
        compiler_params=pltpu.CompilerParams(
            dimension_semantics=("parallel", "parallel"), vmem_limit_bytes=VMEM_LIMIT),
        name="q_proj",
    )(x, g.reshape(1, D), w_q.astype(BF16))


def _bf16_terms(c, n=3):
    terms, rest = [], np.float32(c)
    for _ in range(n):
        t = np.float32(np.asarray(rest, dtype=BF16))
        terms.append(float(t))
        rest = np.float32(rest - t)
    return terms


def _attn_kernel(qt_ref, k_ref, vt_ref, lam_ref, sg_ref, ot_ref,
                 qaug_ref, kaug_ref, sc_ref, m_ref, acc_ref, *, tq, lam_init, slopes):
    qi = pl.program_id(1)
    tk = tq
    LA = SCORE_LOOKAHEAD
    log2_slopes = [s * LOG2E for s in slopes]

    @pl.when((pl.program_id(0) == 0) & (qi == 0))
    def _():
        arow = lax.broadcasted_iota(jnp.int32, (V_DIM, tq), 0)
        dq_row = lax.broadcasted_iota(jnp.int32, (V_DIM, tq), 1).astype(F32)
        col = lax.broadcasted_iota(jnp.int32, (tk, V_DIM), 1)
        dk_col = lax.broadcasted_iota(jnp.int32, (tk, V_DIM), 0).astype(F32)
        for h in range(N_HEADS):
            c = _bf16_terms(log2_slopes[h])
            q_aug = jnp.where(arow == 0, c[0], jnp.where(arow == 1, c[1], jnp.where(
                arow == 2, c[2], jnp.where(arow < 6, dq_row, 0.0)))).astype(BF16)
            kaug_ref[h] = jnp.where(col < 3, dk_col, jnp.where(col == 3, -c[0], jnp.where(
                col == 4, -c[1], jnp.where(col == 5, -c[2], 0.0)))).astype(BF16)
            qaug_ref[2 * h, V_DIM:, :] = q_aug
            qaug_ref[2 * h + 1, V_DIM:, :] = q_aug

    row = lax.broadcasted_iota(jnp.int32, (V_DIM, tq), 0)
    for h in range(N_HEADS):
        qt = qt_ref[0, h * V_DIM:(h + 1) * V_DIM, :]
        zero = jnp.zeros_like(qt)
        qaug_ref[2 * h, 0:V_DIM, :] = jnp.where(row < HEAD_DIM, qt, zero)
        qaug_ref[2 * h + 1, 0:V_DIM, :] = jnp.where(row >= HEAD_DIM, qt, zero)

    ones_rows = jnp.ones((ONES_ROWS, tk), BF16)

    def issue_scores(j, h):
        off2 = pl.multiple_of(j * (tk // 2), tk // 2)
        kt = _bf16(k_ref[0, pl.ds(off2, tk // 2), h * V_DIM:(h + 1) * V_DIM])
        lhs = jnp.concatenate([kt, kaug_ref[h]], axis=1)
        for c in range(2):
            sc_ref[h % LA, c] = jnp.dot(lhs, qaug_ref[2 * h + c], preferred_element_type=F32)

    def softmax_pv(j, h, future):
        off = pl.multiple_of(j * tk, tk)
        vt = _bf16(vt_ref[0, h * V_DIM // 2:(h + 1) * V_DIM // 2, pl.ds(off, tk)])
        vta = jnp.concatenate([vt, ones_rows], axis=0)
        tile_bias = -log2_slopes[h] * ((qi - j) * tk).astype(F32)
        for c in range(2):
            idx = 2 * h + c
            sc = sc_ref[h % LA, c]
            if future is not None:
                sc = jnp.where(future, -jnp.inf, sc)
            m_old = m_ref[idx]
            m_new = jnp.maximum(m_old, jnp.max(sc, axis=0, keepdims=True) + tile_bias)
            alpha = jnp.exp2(m_old - m_new)
            p = jnp.exp2(sc - (m_new - tile_bias)).astype(BF16)
            acc_ref[idx] = alpha * acc_ref[idx] + jnp.dot(vta, p, preferred_element_type=F32)
            m_ref[idx] = m_new

    for h in range(LA):
        issue_scores(0, h)
    m_ref[...] = jnp.full(m_ref.shape, -jnp.inf, F32)
    acc_ref[...] = jnp.zeros(acc_ref.shape, F32)

    def unmasked_tile(j):
        for h in range(N_HEADS):
            softmax_pv(j, h, None)
            if h + LA < N_HEADS:
                issue_scores(j, h + LA)
            else:
                issue_scores(j + 1, h + LA - N_HEADS)

    def tile_group(i, carry):
        for t in range(KV_UNROLL):
            unmasked_tile(KV_UNROLL * i + t)
        return carry

    lax.fori_loop(0, qi // KV_UNROLL, tile_group, 0)

    lv = lam_ref[0]
    lam = (jnp.exp(jnp.sum(lv[0:1] * lv[1:2], keepdims=True))
           - jnp.exp(jnp.sum(lv[2:3] * lv[3:4], keepdims=True)) + lam_init)

    def finish_head(h):
        a1 = acc_ref[2 * h]
        a2 = acc_ref[2 * h + 1]
        o = (a1[0:V_DIM] / a1[V_DIM:V_DIM + 1]) - lam * (a2[0:V_DIM] / a2[V_DIM:V_DIM + 1])
        o = o * lax.rsqrt(jnp.mean(o * o, axis=0, keepdims=True) + SUBLN_EPS) * sg_ref[...]
        ot_ref[0, h * V_DIM:(h + 1) * V_DIM, :] = (o * (1.0 - lam_init)).astype(BF16)

    def tail(n_unmasked):
        base = qi - n_unmasked
        for t in range(n_unmasked):
            unmasked_tile(base + t)
        future = (lax.broadcasted_iota(jnp.int32, (tk, tq), 0)
                  > lax.broadcasted_iota(jnp.int32, (tk, tq), 1))
        for h in range(N_HEADS):
            softmax_pv(qi, h, future)
            if h + LA < N_HEADS:
                issue_scores(qi, h + LA)
            finish_head(h)

    for r in range(KV_UNROLL):
        pl.when(qi % KV_UNROLL == r)(functools.partial(tail, r))


def _diff_attention(qt, k, vt, lam_vecs, subln_g, slopes, lam_init, *, tq=256):
    B, D, S = qt.shape
    assert N_HEADS % SCORE_LOOKAHEAD == 0 and tq <= 256
    kern = functools.partial(_attn_kernel, tq=tq, lam_init=lam_init, slopes=slopes)
    return pl.pallas_call(
        kern,
        out_shape=jax.ShapeDtypeStruct((B, D, S), BF16),
        grid=(B, S // tq),
        in_specs=[
            pl.BlockSpec((1, D, tq), lambda b, i: (b, 0, i)),
            pl.BlockSpec((1, S // 2, D), lambda b, i: (b, 0, 0)),
            pl.BlockSpec((1, D // 2, S), lambda b, i: (b, 0, 0)),
            _const_spec((1, 4, HEAD_DIM)),
            _const_spec((V_DIM, 1)),
        ],
        out_specs=pl.BlockSpec((1, D, tq), lambda b, i: (b, 0, i)),
        scratch_shapes=[
            pltpu.VMEM((2 * N_HEADS, 2 * V_DIM, tq), BF16),
            pltpu.VMEM((N_HEADS, tq, V_DIM), BF16),
            pltpu.VMEM((SCORE_LOOKAHEAD, 2, tq, tq), F32),
            pltpu.VMEM((2 * N_HEADS, 1, tq), F32),
            pltpu.VMEM((2 * N_HEADS, V_DIM + ONES_ROWS, tq), F32),
        ],
        compiler_params=pltpu.CompilerParams(
            dimension_semantics=("arbitrary", "arbitrary"), vmem_limit_bytes=VMEM_LIMIT),
        name="diff_attention",
    )(qt, k, vt, lam_vecs.reshape(1, 4, HEAD_DIM), subln_g.reshape(V_DIM, 1))


def _attention_heads(x, k_v, g, w_q, lam_vecs, subln_g, slopes, lam_init):
    k, vt = k_v
    qt = _q_proj(x, g, w_q)
    return _diff_attention(qt, k, vt, lam_vecs, subln_g, slopes, lam_init)


def kernel(x, a_norm, a_w_in, a_b_in, a_conv_w, a_conv_b, a_gate_w, a_gate_b, a_lambda, a_w_out,
           kv_norm, w_kv, b_norm, b_w_q, b_lam, b_subln, b_w_o,
           mlp_norm, mlp_w1, mlp_w2, final_norm):
    depth = mlp_w1.shape[0]
    n_a = a_w_in.shape[0]
    slopes = tuple(2.0 ** (-8.0 * (h + 1) / N_HEADS) for h in range(N_HEADS))
    k_v = None
    for l in range(depth):
        if l < n_a:
            x = _recurrent_mlp_layer(x, a_norm[l], a_w_in[l], a_b_in[l], a_conv_w[l], a_conv_b[l],
                                     a_gate_w[l], a_gate_b[l], a_lambda[l], a_w_out[l],
                                     mlp_norm[l], mlp_w1[l], mlp_w2[l])
        else:
            j = l - n_a
            lam_init = 0.8 - 0.6 * math.exp(-0.3 * l)
            heads = _attention_heads(x, k_v, b_norm[j], b_w_q[j], b_lam[j], b_subln[j], slopes, lam_init)
            x = _mlp_layer(x, mlp_norm[l], mlp_w1[l], mlp_w2[l], final_norm,
                           final_norm=(l == depth - 1), attn=(heads, b_w_o[j]))
        if l == n_a - 1:
            k_v = _kv_proj(x, kv_norm, w_kv)
    return x
```

```python
import functools
import math

import jax
import jax.numpy as jnp
import numpy as np
from jax import lax
from jax.experimental import pallas as pl
from jax.experimental.pallas import tpu as pltpu

D_MODEL = 1024
N_HEADS = 8
HEAD_DIM = 64
V_DIM = 128
LRU_BLOCKS = 8
LRU_BLOCK_W = 128
CONV_W = 4
LRU_C = 8.0
NORM_EPS = 1e-6
SUBLN_EPS = 1e-5

SUBLANES = 8
ONES_ROWS = 16
SCORE_LOOKAHEAD = 4
KV_UNROLL = 4
LOG2E = math.log2(math.e)
REC_COLS = 256
REC_GATE_LOOKAHEAD = 2
PROJ_ROWS = 256
VMEM_LIMIT = 56 * 1024 * 1024

BF16 = jnp.bfloat16
F32 = jnp.float32


def _rmsnorm(x, g, eps):
    return x * lax.rsqrt(jnp.mean(x * x, axis=-1, keepdims=True) + eps) * g


def _bf16(packed):
    return pltpu.bitcast(packed, BF16)


def _const_spec(shape):
    nd = len(shape)
    return pl.BlockSpec(shape, lambda *_: (0,) * nd, pipeline_mode=pl.Buffered(1))


def _gelu_tanh(x):
    c1 = math.sqrt(2.0 / math.pi)
    return x * (0.5 + 0.5 * jnp.tanh(x * (c1 + (c1 * 0.044715) * (x * x))))


def _rec_mlp_kernel(x_ref, g_ref, wx_ref, wy_ref, bx_ref, by_ref, cw_ref, cb_ref, gw_ref, gb_ref,
                    lam_ref, wo_ref, perm_ref, unperm_ref, xres_ref, mg_ref, w1_ref, w2_ref, o_ref,
                    hn_s, xb_s, yb_s, xmid_s, xpad_ref, tail_ref, h_ref, *, ts, tiles_per_seq):
    f = pl.program_id(0)
    W = D_MODEL
    G = ts // SUBLANES
    HALO = (CONV_W - 1) * SUBLANES
    NB = W // REC_COLS
    starts_sequence = lax.rem(f - 2, tiles_per_seq) == 0

    @pl.when(f == 0)
    def _():
        hn_s[...] = jnp.zeros_like(hn_s)
        xb_s[...] = jnp.zeros_like(xb_s)
        yb_s[...] = jnp.zeros_like(yb_s)
        xmid_s[...] = jnp.zeros_like(xmid_s)
        tail_ref[...] = jnp.zeros_like(tail_ref)
        h_ref[...] = jnp.zeros_like(h_ref)

    row8 = lax.broadcasted_iota(jnp.int32, (SUBLANES, REC_COLS), 0)

    def norm_permute():
        hn = _rmsnorm(x_ref[0], g_ref[...], NORM_EPS).astype(BF16)
        hn_s[...] = jnp.dot(perm_ref[...], hn, preferred_element_type=F32).astype(BF16)

    def in_proj(cb):
        cs = slice(cb * REC_COLS, (cb + 1) * REC_COLS)
        hn = hn_s[...]
        xb_s[:, cs] = jnp.dot(hn, wx_ref[:, cs], preferred_element_type=F32) + bx_ref[:, cs]
        yb_s[:, cs] = jnp.dot(hn, wy_ref[:, cs], preferred_element_type=F32) + by_ref[:, cs]

    def load_proj(cb):
        cs = slice(cb * REC_COLS, (cb + 1) * REC_COLS)
        return xb_s[:, cs], yb_s[:, cs]

    def conv_gates(cb, xb, yb):
        cs = slice(cb * REC_COLS, (cb + 1) * REC_COLS)
        for k in range(CONV_W - 1):
            cur = xb[ts - HALO + k * SUBLANES:ts - HALO + (k + 1) * SUBLANES, :]
            prev = jnp.where(starts_sequence, 0.0, tail_ref[k * SUBLANES:(k + 1) * SUBLANES, cs])
            xpad_ref[k * SUBLANES:(k + 1) * SUBLANES, cs] = jnp.where(
                row8 == 0, pltpu.roll(prev, 1, axis=0), pltpu.roll(cur, 1, axis=0))
        tail_ref[:, cs] = xb[ts - HALO:, :]
        xpad_ref[HALO:, cs] = xb
        xc = xb * cw_ref[CONV_W - 1:CONV_W, cs] + cb_ref[:, cs]
        for j in range(CONV_W - 1):
            xc = xc + xpad_ref[j * SUBLANES:j * SUBLANES + ts, cs] * cw_ref[j:j + 1, cs]
        xcb = xc.astype(BF16)
        r_parts, i_parts = [], []
        for n in range(REC_COLS // LRU_BLOCK_W):
            gn = jnp.dot(xcb[:, n * LRU_BLOCK_W:(n + 1) * LRU_BLOCK_W],
                         gw_ref[cb * (REC_COLS // LRU_BLOCK_W) + n], preferred_element_type=F32)
            r_parts.append(gn[:, :LRU_BLOCK_W])
            i_parts.append(gn[:, LRU_BLOCK_W:])
        return xc, jnp.concatenate(r_parts, axis=1), jnp.concatenate(i_parts, axis=1), yb

    def recurrence(cb, xc, g_r, g_i, yb):
        cs = slice(cb * REC_COLS, (cb + 1) * REC_COLS)
        r = jax.nn.sigmoid(g_r + gb_ref[0:1, cs])
        ig = jax.nn.sigmoid(g_i + gb_ref[1:2, cs])
        z = -lam_ref[:, cs]
        softplus = jnp.maximum(z, 0.0) + jnp.log1p(jnp.exp(-jnp.abs(z)))
        log_a = (-LRU_C * softplus) * r
        a = jnp.exp(log_a)
        bx = jnp.sqrt(-jnp.tanh(log_a) * (a * a + 1.0)) * (ig * xc)

        h_loc = jnp.zeros((SUBLANES, REC_COLS), F32)
        prod = jnp.ones((SUBLANES, REC_COLS), F32)
        h_steps, p_steps = [], []
        for g in range(G):
            a_g = a[g * SUBLANES:(g + 1) * SUBLANES, :]
            h_loc = a_g * h_loc + bx[g * SUBLANES:(g + 1) * SUBLANES, :]
            prod = a_g * prod
            h_steps.append(h_loc)
            p_steps.append(prod)

        t_cum, f_cum = prod, h_loc
        for d in (1, 2, 4):
            keep = row8 >= d
            f_cum = jnp.where(keep, t_cum * pltpu.roll(f_cum, d, axis=0) + f_cum, f_cum)
            t_cum = jnp.where(keep, t_cum * pltpu.roll(t_cum, d, axis=0), t_cum)
        h_prev = jnp.where(starts_sequence, 0.0, h_ref[SUBLANES - 1:SUBLANES, cs])
        h_end = t_cum * h_prev + f_cum
        h_in = jnp.where(row8 == 0, h_prev, pltpu.roll(h_end, 1, axis=0))
        h_ref[:, cs] = h_end
        hs = jnp.concatenate([h_steps[g] + p_steps[g] * h_in for g in range(G)], axis=0)
        return (hs * _gelu_tanh(yb)).astype(BF16)

    def unpermute(y):
        return jnp.dot(unperm_ref[...], y, preferred_element_type=F32).astype(BF16)

    def out_proj(cb, y):
        return jnp.dot(y, wo_ref[cb * REC_COLS:(cb + 1) * REC_COLS, :], preferred_element_type=F32)

    F = w1_ref.shape[1]
    fchunk = F // NB
    xm = xmid_s[...]
    hm = _rmsnorm(xm, mg_ref[...], NORM_EPS).astype(BF16)

    def mlp_up(fc):
        fs = slice(fc * fchunk, (fc + 1) * fchunk)
        u = jnp.maximum(jnp.dot(hm, w1_ref[:, fs], preferred_element_type=F32), 0.0)
        return (u * u).astype(BF16)

    def mlp_down(fc, u):
        return jnp.dot(u, w2_ref[fc * fchunk:(fc + 1) * fchunk, :], preferred_element_type=F32)

    loaded = {cb: load_proj(cb) for cb in range(NB)}
    for cb in range(NB):
        in_proj(cb)
    norm_permute()
    up = {0: mlp_up(0)}
    gated = {cb: conv_gates(cb, *loaded.pop(cb)) for cb in range(REC_GATE_LOOKAHEAD)}
    out = xres_ref[0]
    mlp_out = xm
    y_nat = {}
    for cb in range(NB):
        if cb + 1 < NB:
            up[cb + 1] = mlp_up(cb + 1)
        mlp_out = mlp_out + mlp_down(cb, up.pop(cb))
        nxt = cb + REC_GATE_LOOKAHEAD
        if nxt < NB:
            gated[nxt] = conv_gates(nxt, *loaded.pop(nxt))
        y_nat[cb] = unpermute(recurrence(cb, *gated.pop(cb)))
        if cb > 0:
            out = out + out_proj(cb - 1, y_nat.pop(cb - 1))
    out = out + out_proj(NB - 1, y_nat.pop(NB - 1))
    o_ref[0] = mlp_out
    xmid_s[...] = out


def _chunk_permutation(ts):
    steps = ts // SUBLANES
    p = np.zeros((ts, ts), np.float32)
    for c in range(SUBLANES):
        for s in range(steps):
            p[s * SUBLANES + c, c * steps + s] = 1.0
    return p


def _recurrent_mlp_layer(x, g, w_in, b_in, conv_w, conv_b, gate_w, gate_b, lam, w_out,
                         mlp_g, w1, w2, *, ts=256):
    B, S, D = x.shape
    F = w1.shape[1]
    W = D_MODEL
    n_seq = S // ts
    n_tiles = B * n_seq
    wx = w_in[:, :W].astype(BF16)
    wy = w_in[:, W:].astype(BF16)
    bxv = b_in[:W].reshape(1, W)
    byv = b_in[W:].reshape(1, W)
    gw = jnp.concatenate([gate_w[0], gate_w[1]], axis=-1).astype(BF16)
    perm = _chunk_permutation(ts)
    kern = functools.partial(_rec_mlp_kernel, ts=ts, tiles_per_seq=n_seq)

    def in_tile(f):
        t = jnp.minimum(f, n_tiles - 1)
        return (t // n_seq, t % n_seq, 0)

    def res_tile(f):
        t = jnp.clip(f - 2, 0, n_tiles - 1)
        return (t // n_seq, t % n_seq, 0)

    def out_tile(f):
        t = jnp.maximum(f - 3, 0)
        return (t // n_seq, t % n_seq, 0)

    return pl.pallas_call(
        kern,
        out_shape=jax.ShapeDtypeStruct((B, S, D), F32),
        grid=(n_tiles + 3,),
        in_specs=[
            pl.BlockSpec((1, ts, D), in_tile),
            _const_spec((1, D)),
            _const_spec((D, W)), _const_spec((D, W)),
            _const_spec((1, W)), _const_spec((1, W)),
            _const_spec((CONV_W, W)), _const_spec((1, W)),
            _const_spec((LRU_BLOCKS, LRU_BLOCK_W, 2 * LRU_BLOCK_W)),
            _const_spec((2, W)),
            _const_spec((1, W)),
            _const_spec((W, D)),
            _const_spec((ts, ts)), _const_spec((ts, ts)),
            pl.BlockSpec((1, ts, D), res_tile),
            _const_spec((1, D)), _const_spec((D, F)), _const_spec((F, D)),
        ],
        out_specs=pl.BlockSpec((1, ts, D), out_tile),
        scratch_shapes=[
            pltpu.VMEM((ts, D), BF16),
            pltpu.VMEM((ts, W), F32),
            pltpu.VMEM((ts, W), F32),
            pltpu.VMEM((ts, D), F32),
            pltpu.VMEM((ts + (CONV_W - 1) * SUBLANES, W), F32),
            pltpu.VMEM(((CONV_W - 1) * SUBLANES, W), F32),
            pltpu.VMEM((SUBLANES, W), F32),
        ],
        compiler_params=pltpu.CompilerParams(
            dimension_semantics=("arbitrary",), vmem_limit_bytes=VMEM_LIMIT),
        name="rglru_mlp_layer",
    )(x, g.reshape(1, D), wx, wy, bxv, byv, conv_w, conv_b.reshape(1, W), gw, gate_b,
      lam.reshape(1, W), w_out.astype(BF16), jnp.asarray(perm, BF16), jnp.asarray(perm.T, BF16), x,
      mlp_g.reshape(1, D), w1.astype(BF16), w2.astype(BF16))


def _mlp_halves(xs, g_ref, w1_ref, w2_ref, fg_ref, o_ref, final_norm):
    hs = [_rmsnorm(x, g_ref[...], NORM_EPS).astype(BF16) for x in xs]
    us = [jnp.maximum(jnp.dot(h, w1_ref[...], preferred_element_type=F32), 0.0) for h in hs]
    us = [(u * u).astype(BF16) for u in us]
    ys = [x + jnp.dot(u, w2_ref[...], preferred_element_type=F32) for x, u in zip(xs, us)]
    half = xs[0].shape[0]
    for i, y in enumerate(ys):
        if final_norm:
            y = _rmsnorm(y, fg_ref[...], NORM_EPS)
        o_ref[0, i * half:(i + 1) * half, :] = y


def _row_halves(n):
    return [slice(0, n // 2), slice(n // 2, n)]


def _mlp_kernel(x_ref, g_ref, w1_ref, w2_ref, fg_ref, o_ref, *, final_norm):
    xs = [x_ref[0, rows, :] for rows in _row_halves(x_ref.shape[1])]
    _mlp_halves(xs, g_ref, w1_ref, w2_ref, fg_ref, o_ref, final_norm)


def _attn_out_mlp_kernel(at_ref, wo_ref, x_ref, g_ref, w1_ref, w2_ref, fg_ref, o_ref, *, final_norm):
    xs = [x_ref[0, rows, :] + lax.dot_general(at_ref[0, :, rows], wo_ref[...], (((0,), (0,)), ((), ())),
                                              preferred_element_type=F32)
          for rows in _row_halves(x_ref.shape[1])]
    _mlp_halves(xs, g_ref, w1_ref, w2_ref, fg_ref, o_ref, final_norm)


def _mlp_layer(x, g, w1, w2, final_g, *, final_norm, attn=None, tm=512):
    B, S, D = x.shape
    F = w1.shape[1]
    row_spec = pl.BlockSpec((1, tm, D), lambda b, s: (b, s, 0))
    specs = [row_spec, _const_spec((1, D)), _const_spec((D, F)), _const_spec((F, D)),
             _const_spec((1, D))]
    args = [x, g.reshape(1, D), w1.astype(BF16), w2.astype(BF16), final_g.reshape(1, D)]
    kern = _mlp_kernel
    if attn is not None:
        at, w_o = attn
        specs = [pl.BlockSpec((1, D, tm), lambda b, s: (b, 0, s)), _const_spec((D, D))] + specs
        args = [at, w_o.astype(BF16)] + args
        kern = _attn_out_mlp_kernel
    return pl.pallas_call(
        functools.partial(kern, final_norm=final_norm),
        out_shape=jax.ShapeDtypeStruct((B, S, D), F32),
        grid=(B, S // tm),
        in_specs=specs,
        out_specs=row_spec,
        compiler_params=pltpu.CompilerParams(
            dimension_semantics=("parallel", "parallel"), vmem_limit_bytes=VMEM_LIMIT),
        name="mlp_layer",
    )(*args)


def _kv_kernel(x_ref, g_ref, wk_ref, wv_ref, k_ref, vt_ref):
    for r0 in range(0, x_ref.shape[1], PROJ_ROWS):
        rows = slice(r0, r0 + PROJ_ROWS)
        hn = _rmsnorm(x_ref[0, rows, :], g_ref[...], NORM_EPS).astype(BF16)
        k = jnp.dot(hn, wk_ref[...], preferred_element_type=F32).astype(BF16)
        k_ref[0, r0 // 2:(r0 + PROJ_ROWS) // 2, :] = pltpu.bitcast(k, jnp.uint32)
        v = jnp.dot(hn, wv_ref[...], preferred_element_type=F32)
        vt_ref[0, :, rows] = pltpu.bitcast(v.T.astype(BF16), jnp.uint32)


def _kv_proj(x, g, w_kv, *, tm=1024):
    B, S, D = x.shape
    return pl.pallas_call(
        _kv_kernel,
        out_shape=(jax.ShapeDtypeStruct((B, S // 2, D), jnp.uint32),
                   jax.ShapeDtypeStruct((B, D // 2, S), jnp.uint32)),
        grid=(B, S // tm),
        in_specs=[
            pl.BlockSpec((1, tm, D), lambda b, s: (b, s, 0)),
            _const_spec((1, D)), _const_spec((D, D)), _const_spec((D, D)),
        ],
        out_specs=(pl.BlockSpec((1, tm // 2, D), lambda b, s: (b, s, 0)),
                   pl.BlockSpec((1, D // 2, tm), lambda b, s: (b, 0, s))),
        compiler_params=pltpu.CompilerParams(
            dimension_semantics=("parallel", "parallel"), vmem_limit_bytes=VMEM_LIMIT),
        name="kv_proj",
    )(x, g.reshape(1, D), w_kv[:, :D].astype(BF16), w_kv[:, D:].astype(BF16))


def _q_kernel(x_ref, g_ref, wq_ref, qt_ref, *, scale):
    for r0 in range(0, x_ref.shape[1], PROJ_ROWS):
        rows = slice(r0, r0 + PROJ_ROWS)
        hn = _rmsnorm(x_ref[0, rows, :], g_ref[...], NORM_EPS).astype(BF16)
        q = jnp.dot(hn, wq_ref[...], preferred_element_type=F32) * scale
        qt_ref[0, :, rows] = q.T.astype(BF16)


def _q_proj(x, g, w_q, *, tm=1024):
    B, S, D = x.shape
    kern = functools.partial(_q_kernel, scale=LOG2E * HEAD_DIM ** -0.5)
    return pl.pallas_call(
        kern,
        out_shape=jax.ShapeDtypeStruct((B, D, S), BF16),
        grid=(B, S // tm),
        in_specs=[
            pl.BlockSpec((1, tm, D), lambda b, s: (b, s, 0)),
            _const_spec((1, D)), _const_spec((D, D)),
        ],
        out_specs=pl.BlockSpec((1, D, tm), lambda b, s: (b, 0, s)),
        compiler_params=pltpu.CompilerParams(
            dimension_semantics=("parallel", "parallel"), vmem_limit_bytes=VMEM_LIMIT),
        name="q_proj",
    )(x, g.reshape(1, D), w_q.astype(BF16))


def _bf16_terms(c, n=3):
    terms, rest = [], np.float32(c)
    for _ in range(n):
        t = np.float32(np.asarray(rest, dtype=BF16))
        terms.append(float(t))
        rest = np.float32(rest - t)
    return terms


def _attn_kernel(qt_ref, k_ref, vt_ref, lam_ref, sg_ref, ot_ref,
                 qaug_ref, kaug_ref, sc_ref, m_ref, acc_ref, *, tq, lam_init, slopes):
    qi = pl.program_id(1)
    tk = tq
    LA = SCORE_LOOKAHEAD
    log2_slopes = [s * LOG2E for s in slopes]

    @pl.when((pl.program_id(0) == 0) & (qi == 0))
    def _():
        arow = lax.broadcasted_iota(jnp.int32, (V_DIM, tq), 0)
        dq_row = lax.broadcasted_iota(jnp.int32, (V_DIM, tq), 1).astype(F32)
        col = lax.broadcasted_iota(jnp.int32, (tk, V_DIM), 1)
        dk_col = lax.broadcasted_iota(jnp.int32, (tk, V_DIM), 0).astype(F32)
        for h in range(N_HEADS):
            c = _bf16_terms(log2_slopes[h])
            q_aug = jnp.where(arow == 0, c[0], jnp.where(arow == 1, c[1], jnp.where(
                arow == 2, c[2], jnp.where(arow < 6, dq_row, 0.0)))).astype(BF16)
            kaug_ref[h] = jnp.where(col < 3, dk_col, jnp.where(col == 3, -c[0], jnp.where(
                col == 4, -c[1], jnp.where(col == 5, -c[2], 0.0)))).astype(BF16)
            qaug_ref[2 * h, V_DIM:, :] = q_aug
            qaug_ref[2 * h + 1, V_DIM:, :] = q_aug

    row = lax.broadcasted_iota(jnp.int32, (V_DIM, tq), 0)
    for h in range(N_HEADS):
        qt = qt_ref[0, h * V_DIM:(h + 1) * V_DIM, :]
        zero = jnp.zeros_like(qt)
        qaug_ref[2 * h, 0:V_DIM, :] = jnp.where(row < HEAD_DIM, qt, zero)
        qaug_ref[2 * h + 1, 0:V_DIM, :] = jnp.where(row >= HEAD_DIM, qt, zero)

    ones_rows = jnp.ones((ONES_ROWS, tk), BF16)

    def issue_scores(j, h):
        off2 = pl.multiple_of(j * (tk // 2), tk // 2)
        kt = _bf16(k_ref[0, pl.ds(off2, tk // 2), h * V_DIM:(h + 1) * V_DIM])
        lhs = jnp.concatenate([kt, kaug_ref[h]], axis=1)
        for c in range(2):
            sc_ref[h % LA, c] = jnp.dot(lhs, qaug_ref[2 * h + c], preferred_element_type=F32)

    def softmax_pv(j, h, future):
        off = pl.multiple_of(j * tk, tk)
        vt = _bf16(vt_ref[0, h * V_DIM // 2:(h + 1) * V_DIM // 2, pl.ds(off, tk)])
        vta = jnp.concatenate([vt, ones_rows], axis=0)
        tile_bias = -log2_slopes[h] * ((qi - j) * tk).astype(F32)
        for c in range(2):
            idx = 2 * h + c
            sc = sc_ref[h % LA, c]
            if future is not None:
                sc = jnp.where(future, -jnp.inf, sc)
            m_old = m_ref[idx]
            m_new = jnp.maximum(m_old, jnp.max(sc, axis=0, keepdims=True) + tile_bias)
            alpha = jnp.exp2(m_old - m_new)
            p = jnp.exp2(sc - (m_new - tile_bias)).astype(BF16)
            acc_ref[idx] = alpha * acc_ref[idx] + jnp.dot(vta, p, preferred_element_type=F32)
            m_ref[idx] = m_new

    for h in range(LA):
        issue_scores(0, h)
    m_ref[...] = jnp.full(m_ref.shape, -jnp.inf, F32)
    acc_ref[...] = jnp.zeros(acc_ref.shape, F32)

    def unmasked_tile(j):
        for h in range(N_HEADS):
            softmax_pv(j, h, None)
            if h + LA < N_HEADS:
                issue_scores(j, h + LA)
            else:
                issue_scores(j + 1, h + LA - N_HEADS)

    def tile_group2(i, carry):
        for t in range(2 * KV_UNROLL):
            unmasked_tile(2 * KV_UNROLL * i + t)
        return carry

    def tile_group(i, carry):
        base = (qi // (2 * KV_UNROLL)) * (2 * KV_UNROLL)
        for t in range(KV_UNROLL):
            unmasked_tile(base + t)
        return carry

    lax.fori_loop(0, qi // (2 * KV_UNROLL), tile_group2, 0)
    lax.fori_loop(0, (qi % (2 * KV_UNROLL)) // KV_UNROLL, tile_group, 0)

    lv = lam_ref[0]
    lam = (jnp.exp(jnp.sum(lv[0:1] * lv[1:2], keepdims=True))
           - jnp.exp(jnp.sum(lv[2:3] * lv[3:4], keepdims=True)) + lam_init)

    def finish_head(h):
        a1 = acc_ref[2 * h]
        a2 = acc_ref[2 * h + 1]
        o = (a1[0:V_DIM] / a1[V_DIM:V_DIM + 1]) - lam * (a2[0:V_DIM] / a2[V_DIM:V_DIM + 1])
        o = o * lax.rsqrt(jnp.mean(o * o, axis=0, keepdims=True) + SUBLN_EPS) * sg_ref[...]
        ot_ref[0, h * V_DIM:(h + 1) * V_DIM, :] = (o * (1.0 - lam_init)).astype(BF16)

    def tail(n_unmasked):
        base = qi - n_unmasked
        for t in range(n_unmasked):
            unmasked_tile(base + t)
        future = (lax.broadcasted_iota(jnp.int32, (tk, tq), 0)
                  > lax.broadcasted_iota(jnp.int32, (tk, tq), 1))
        for h in range(N_HEADS):
            softmax_pv(qi, h, future)
            if h + LA < N_HEADS:
                issue_scores(qi, h + LA)
            finish_head(h)

    for r in range(KV_UNROLL):
        pl.when(qi % KV_UNROLL == r)(functools.partial(tail, r))


def _diff_attention(qt, k, vt, lam_vecs, subln_g, slopes, lam_init, *, tq=256):
    B, D, S = qt.shape
    assert N_HEADS % SCORE_LOOKAHEAD == 0 and tq <= 256
    kern = functools.partial(_attn_kernel, tq=tq, lam_init=lam_init, slopes=slopes)
    return pl.pallas_call(
        kern,
        out_shape=jax.ShapeDtypeStruct((B, D, S), BF16),
        grid=(B, S // tq),
        in_specs=[
            pl.BlockSpec((1, D, tq), lambda b, i: (b, 0, i)),
            pl.BlockSpec((1, S // 2, D), lambda b, i: (b, 0, 0)),
            pl.BlockSpec((1, D // 2, S), lambda b, i: (b, 0, 0)),
            _const_spec((1, 4, HEAD_DIM)),
            _const_spec((V_DIM, 1)),
        ],
        out_specs=pl.BlockSpec((1, D, tq), lambda b, i: (b, 0, i)),
        scratch_shapes=[
            pltpu.VMEM((2 * N_HEADS, 2 * V_DIM, tq), BF16),
            pltpu.VMEM((N_HEADS, tq, V_DIM), BF16),
            pltpu.VMEM((SCORE_LOOKAHEAD, 2, tq, tq), F32),
            pltpu.VMEM((2 * N_HEADS, 1, tq), F32),
            pltpu.VMEM((2 * N_HEADS, V_DIM + ONES_ROWS, tq), F32),
        ],
        compiler_params=pltpu.CompilerParams(
            dimension_semantics=("arbitrary", "arbitrary"), vmem_limit_bytes=VMEM_LIMIT),
        name="diff_attention",
    )(qt, k, vt, lam_vecs.reshape(1, 4, HEAD_DIM), subln_g.reshape(V_DIM, 1))


def _attention_heads(x, k_v, g, w_q, lam_vecs, subln_g, slopes, lam_init):
    k, vt = k_v
    qt = _q_proj(x, g, w_q)
    return _diff_attention(qt, k, vt, lam_vecs, subln_g, slopes, lam_init)


def kernel(x, a_norm, a_w_in, a_b_in, a_conv_w, a_conv_b, a_gate_w, a_gate_b, a_lambda, a_w_out,
           kv_norm, w_kv, b_norm, b_w_q, b_lam, b_subln, b_w_o,
           mlp_norm, mlp_w1, mlp_w2, final_norm):
    depth = mlp_w1.shape[0]
    n_a = a_w_in.shape[0]
    slopes = tuple(2.0 ** (-8.0 * (h + 1) / N_HEADS) for h in range(N_HEADS))
    k_v = None
    for l in range(depth):
        if l < n_a:
            x = _recurrent_mlp_layer(x, a_norm[l], a_w_in[l], a_b_in[l], a_conv_w[l], a_conv_b[l],
                                     a_gate_w[l], a_gate_b[l], a_lambda[l], a_w_out[l],
                                     mlp_norm[l], mlp_w1[l], mlp_w2[l])
        else:
            j = l - n_a
            lam_init = 0.8 - 0.6 * math.exp(-0.3 * l)
            heads = _attention_heads(x, k_v, b_norm[j], b_w_q[j], b_lam[j], b_subln[j], slopes, lam_init)
            x = _mlp_layer(x, mlp_norm[l], mlp_w1[l], mlp_w2[l], final_norm,
                           final_norm=(l == depth - 1), attn=(heads, b_w_o[j]))
        if l == n_a - 1:
            k_v = _kv_proj(x, kv_norm, w_kv)
    return x
```

```python
import functools
import math

import jax
import jax.numpy as jnp
import numpy as np
from jax import lax
from jax.experimental import pallas as pl
from jax.experimental.pallas import tpu as pltpu

D_MODEL = 1024
N_HEADS = 8
HEAD_DIM = 64
V_DIM = 128
LRU_BLOCKS = 8
LRU_BLOCK_W = 128
CONV_W = 4
LRU_C = 8.0
NORM_EPS = 1e-6
SUBLN_EPS = 1e-5

SUBLANES = 8
ONES_ROWS = 16
SCORE_LOOKAHEAD = 4
SOFTMAX_LANES = 128
KV_UNROLL = 4
LOG2E = math.log2(math.e)
REC_COLS = 256
REC_GATE_LOOKAHEAD = 2
TILES_PER_STEP = 2
PROJ_ROWS = 256
VMEM_LIMIT = 56 * 1024 * 1024

BF16 = jnp.bfloat16
F32 = jnp.float32


def _rmsnorm(x, g, eps):
    return x * lax.rsqrt(jnp.mean(x * x, axis=-1, keepdims=True) + eps) * g


def _bf16(packed):
    return pltpu.bitcast(packed, BF16)


def _const_spec(shape):
    nd = len(shape)
    return pl.BlockSpec(shape, lambda *_: (0,) * nd, pipeline_mode=pl.Buffered(1))


def _gelu_tanh(x):
    c1 = math.sqrt(2.0 / math.pi)
    return x * (0.5 + 0.5 * jnp.tanh(x * (c1 + (c1 * 0.044715) * (x * x))))


def _rec_mlp_kernel(*refs, ts, tiles_per_seq):
    for sub in range(TILES_PER_STEP):
        _rec_mlp_step(*refs, ts=ts, tiles_per_seq=tiles_per_seq, sub=sub)


def _rec_mlp_step(x_ref, g_ref, wx_ref, wy_ref, bx_ref, by_ref, cw_ref, cb_ref, gw_ref, gb_ref,
                  lam_ref, wo_ref, perm_ref, unperm_ref, xres_ref, mg_ref, w1_ref, w2_ref, o_ref,
                  hn_s, xb_s, yb_s, xmid_s, xpad_ref, tail_ref, h_ref, *, ts, tiles_per_seq, sub):
    f = pl.program_id(0)
    W = D_MODEL
    G = ts // SUBLANES
    HALO = (CONV_W - 1) * SUBLANES
    NB = W // REC_COLS
    rows = slice(sub * ts, (sub + 1) * ts)
    starts_sequence = lax.rem(TILES_PER_STEP * f + sub - 2, tiles_per_seq) == 0

    if sub == 0:
        @pl.when(f == 0)
        def _():
            hn_s[...] = jnp.zeros_like(hn_s)
            xb_s[...] = jnp.zeros_like(xb_s)
            yb_s[...] = jnp.zeros_like(yb_s)
            xmid_s[...] = jnp.zeros_like(xmid_s)
            tail_ref[...] = jnp.zeros_like(tail_ref)
            h_ref[...] = jnp.zeros_like(h_ref)

    row8 = lax.broadcasted_iota(jnp.int32, (SUBLANES, REC_COLS), 0)

    def norm_permute():
        hn = _rmsnorm(x_ref[0, rows, :], g_ref[...], NORM_EPS).astype(BF16)
        hn_s[...] = jnp.dot(perm_ref[...], hn, preferred_element_type=F32).astype(BF16)

    def in_proj(cb):
        cs = slice(cb * REC_COLS, (cb + 1) * REC_COLS)
        hn = hn_s[...]
        xb_s[:, cs] = jnp.dot(hn, wx_ref[:, cs], preferred_element_type=F32) + bx_ref[:, cs]
        yb_s[:, cs] = jnp.dot(hn, wy_ref[:, cs], preferred_element_type=F32) + by_ref[:, cs]

    def load_proj(cb):
        cs = slice(cb * REC_COLS, (cb + 1) * REC_COLS)
        return xb_s[:, cs], yb_s[:, cs]

    def conv_gates(cb, xb, yb):
        cs = slice(cb * REC_COLS, (cb + 1) * REC_COLS)
        for k in range(CONV_W - 1):
            cur = xb[ts - HALO + k * SUBLANES:ts - HALO + (k + 1) * SUBLANES, :]
            prev = jnp.where(starts_sequence, 0.0, tail_ref[k * SUBLANES:(k + 1) * SUBLANES, cs])
            xpad_ref[k * SUBLANES:(k + 1) * SUBLANES, cs] = jnp.where(
                row8 == 0, pltpu.roll(prev, 1, axis=0), pltpu.roll(cur, 1, axis=0))
        tail_ref[:, cs] = xb[ts - HALO:, :]
        xpad_ref[HALO:, cs] = xb
        xc = xb * cw_ref[CONV_W - 1:CONV_W, cs] + cb_ref[:, cs]
        for j in range(CONV_W - 1):
            xc = xc + xpad_ref[j * SUBLANES:j * SUBLANES + ts, cs] * cw_ref[j:j + 1, cs]
        xcb = xc.astype(BF16)
        r_parts, i_parts = [], []
        for n in range(REC_COLS // LRU_BLOCK_W):
            gn = jnp.dot(xcb[:, n * LRU_BLOCK_W:(n + 1) * LRU_BLOCK_W],
                         gw_ref[cb * (REC_COLS // LRU_BLOCK_W) + n], preferred_element_type=F32)
            r_parts.append(gn[:, :LRU_BLOCK_W])
            i_parts.append(gn[:, LRU_BLOCK_W:])
        return xc, jnp.concatenate(r_parts, axis=1), jnp.concatenate(i_parts, axis=1), yb

    def recurrence(cb, xc, g_r, g_i, yb):
        cs = slice(cb * REC_COLS, (cb + 1) * REC_COLS)
        r = jax.nn.sigmoid(g_r + gb_ref[0:1, cs])
        ig = jax.nn.sigmoid(g_i + gb_ref[1:2, cs])
        z = -lam_ref[:, cs]
        softplus = jnp.maximum(z, 0.0) + jnp.log1p(jnp.exp(-jnp.abs(z)))
        log_a = (-LRU_C * softplus) * r
        a = jnp.exp(log_a)
        bx = jnp.sqrt(-jnp.tanh(log_a) * (a * a + 1.0)) * (ig * xc)

        h_loc = jnp.zeros((SUBLANES, REC_COLS), F32)
        prod = jnp.ones((SUBLANES, REC_COLS), F32)
        h_steps, p_steps = [], []
        for g in range(G):
            a_g = a[g * SUBLANES:(g + 1) * SUBLANES, :]
            h_loc = a_g * h_loc + bx[g * SUBLANES:(g + 1) * SUBLANES, :]
            prod = a_g * prod
            h_steps.append(h_loc)
            p_steps.append(prod)

        t_cum, f_cum = prod, h_loc
        for d in (1, 2, 4):
            keep = row8 >= d
            f_cum = jnp.where(keep, t_cum * pltpu.roll(f_cum, d, axis=0) + f_cum, f_cum)
            t_cum = jnp.where(keep, t_cum * pltpu.roll(t_cum, d, axis=0), t_cum)
        h_prev = jnp.where(starts_sequence, 0.0, h_ref[SUBLANES - 1:SUBLANES, cs])
        h_end = t_cum * h_prev + f_cum
        h_in = jnp.where(row8 == 0, h_prev, pltpu.roll(h_end, 1, axis=0))
        h_ref[:, cs] = h_end
        hs = jnp.concatenate([h_steps[g] + p_steps[g] * h_in for g in range(G)], axis=0)
        return (hs * _gelu_tanh(yb)).astype(BF16)

    def unpermute(y):
        return jnp.dot(unperm_ref[...], y, preferred_element_type=F32).astype(BF16)

    def out_proj(cb, y):
        return jnp.dot(y, wo_ref[cb * REC_COLS:(cb + 1) * REC_COLS, :], preferred_element_type=F32)

    F = w1_ref.shape[1]
    fchunk = F // NB
    xm = xmid_s[sub]
    hm = _rmsnorm(xm, mg_ref[...], NORM_EPS).astype(BF16)

    def mlp_up(fc):
        fs = slice(fc * fchunk, (fc + 1) * fchunk)
        u = jnp.maximum(jnp.dot(hm, w1_ref[:, fs], preferred_element_type=F32), 0.0)
        return (u * u).astype(BF16)

    def mlp_down(fc, u):
        return jnp.dot(u, w2_ref[fc * fchunk:(fc + 1) * fchunk, :], preferred_element_type=F32)

    loaded = {cb: load_proj(cb) for cb in range(NB)}
    for cb in range(NB):
        in_proj(cb)
    norm_permute()
    up = {0: mlp_up(0)}
    gated = {cb: conv_gates(cb, *loaded.pop(cb)) for cb in range(REC_GATE_LOOKAHEAD)}
    out = xres_ref[0, rows, :]
    mlp_out = xm
    y_nat = {}
    for cb in range(NB):
        if cb + 1 < NB:
            up[cb + 1] = mlp_up(cb + 1)
        mlp_out = mlp_out + mlp_down(cb, up.pop(cb))
        nxt = cb + REC_GATE_LOOKAHEAD
        if nxt < NB:
            gated[nxt] = conv_gates(nxt, *loaded.pop(nxt))
        y_nat[cb] = unpermute(recurrence(cb, *gated.pop(cb)))
        if cb > 0:
            out = out + out_proj(cb - 1, y_nat.pop(cb - 1))
    out = out + out_proj(NB - 1, y_nat.pop(NB - 1))
    o_ref[0, rows, :] = mlp_out
    xmid_s[sub] = out


def _chunk_permutation(ts):
    steps = ts // SUBLANES
    p = np.zeros((ts, ts), np.float32)
    for c in range(SUBLANES):
        for s in range(steps):
            p[s * SUBLANES + c, c * steps + s] = 1.0
    return p


def _recurrent_mlp_layer(x, g, w_in, b_in, conv_w, conv_b, gate_w, gate_b, lam, w_out,
                         mlp_g, w1, w2, *, ts=256):
    B, S, D = x.shape
    F = w1.shape[1]
    W = D_MODEL
    assert TILES_PER_STEP == 2 and S % (TILES_PER_STEP * ts) == 0
    n_seq = S // (TILES_PER_STEP * ts)
    n_blocks = B * n_seq
    wx = w_in[:, :W].astype(BF16)
    wy = w_in[:, W:].astype(BF16)
    bxv = b_in[:W].reshape(1, W)
    byv = b_in[W:].reshape(1, W)
    gw = jnp.concatenate([gate_w[0], gate_w[1]], axis=-1).astype(BF16)
    perm = _chunk_permutation(ts)
    kern = functools.partial(_rec_mlp_kernel, ts=ts, tiles_per_seq=TILES_PER_STEP * n_seq)

    def in_tile(f):
        t = jnp.minimum(f, n_blocks - 1)
        return (t // n_seq, t % n_seq, 0)

    def res_tile(f):
        t = jnp.clip(f - 1, 0, n_blocks - 1)
        return (t // n_seq, t % n_seq, 0)

    def out_tile(f):
        t = jnp.maximum(f - 2, 0)
        return (t // n_seq, t % n_seq, 0)

    return pl.pallas_call(
        kern,
        out_shape=jax.ShapeDtypeStruct((B, S, D), F32),
        grid=(n_blocks + 2,),
        in_specs=[
            pl.BlockSpec((1, TILES_PER_STEP * ts, D), in_tile),
            _const_spec((1, D)),
            _const_spec((D, W)), _const_spec((D, W)),
            _const_spec((1, W)), _const_spec((1, W)),
            _const_spec((CONV_W, W)), _const_spec((1, W)),
            _const_spec((LRU_BLOCKS, LRU_BLOCK_W, 2 * LRU_BLOCK_W)),
            _const_spec((2, W)),
            _const_spec((1, W)),
            _const_spec((W, D)),
            _const_spec((ts, ts)), _const_spec((ts, ts)),
            pl.BlockSpec((1, TILES_PER_STEP * ts, D), res_tile),
            _const_spec((1, D)), _const_spec((D, F)), _const_spec((F, D)),
        ],
        out_specs=pl.BlockSpec((1, TILES_PER_STEP * ts, D), out_tile),
        scratch_shapes=[
            pltpu.VMEM((ts, D), BF16),
            pltpu.VMEM((ts, W), F32),
            pltpu.VMEM((ts, W), F32),
            pltpu.VMEM((TILES_PER_STEP, ts, D), F32),
            pltpu.VMEM((ts + (CONV_W - 1) * SUBLANES, W), F32),
            pltpu.VMEM(((CONV_W - 1) * SUBLANES, W), F32),
            pltpu.VMEM((SUBLANES, W), F32),
        ],
        compiler_params=pltpu.CompilerParams(
            dimension_semantics=("arbitrary",), vmem_limit_bytes=VMEM_LIMIT),
        name="rglru_mlp_layer",
    )(x, g.reshape(1, D), wx, wy, bxv, byv, conv_w, conv_b.reshape(1, W), gw, gate_b,
      lam.reshape(1, W), w_out.astype(BF16), jnp.asarray(perm, BF16), jnp.asarray(perm.T, BF16), x,
      mlp_g.reshape(1, D), w1.astype(BF16), w2.astype(BF16))


def _mlp_halves(xs, g_ref, w1_ref, w2_ref, fg_ref, o_ref, final_norm):
    hs = [_rmsnorm(x, g_ref[...], NORM_EPS).astype(BF16) for x in xs]
    us = [jnp.maximum(jnp.dot(h, w1_ref[...], preferred_element_type=F32), 0.0) for h in hs]
    us = [(u * u).astype(BF16) for u in us]
    ys = [x + jnp.dot(u, w2_ref[...], preferred_element_type=F32) for x, u in zip(xs, us)]
    half = xs[0].shape[0]
    for i, y in enumerate(ys):
        if final_norm:
            y = _rmsnorm(y, fg_ref[...], NORM_EPS)
        o_ref[0, i * half:(i + 1) * half, :] = y


def _row_halves(n):
    return [slice(0, n // 2), slice(n // 2, n)]


def _mlp_kernel(x_ref, g_ref, w1_ref, w2_ref, fg_ref, o_ref, *, final_norm):
    xs = [x_ref[0, rows, :] for rows in _row_halves(x_ref.shape[1])]
    _mlp_halves(xs, g_ref, w1_ref, w2_ref, fg_ref, o_ref, final_norm)


def _attn_out_mlp_kernel(at_ref, wo_ref, x_ref, g_ref, w1_ref, w2_ref, fg_ref, o_ref, *, final_norm):
    xs = [x_ref[0, rows, :] + lax.dot_general(at_ref[0, :, rows], wo_ref[...], (((0,), (0,)), ((), ())),
                                              preferred_element_type=F32)
          for rows in _row_halves(x_ref.shape[1])]
    _mlp_halves(xs, g_ref, w1_ref, w2_ref, fg_ref, o_ref, final_norm)


def _mlp_layer(x, g, w1, w2, final_g, *, final_norm, attn=None, tm=512):
    B, S, D = x.shape
    F = w1.shape[1]
    row_spec = pl.BlockSpec((1, tm, D), lambda b, s: (b, s, 0))
    specs = [row_spec, _const_spec((1, D)), _const_spec((D, F)), _const_spec((F, D)),
             _const_spec((1, D))]
    args = [x, g.reshape(1, D), w1.astype(BF16), w2.astype(BF16), final_g.reshape(1, D)]
    kern = _mlp_kernel
    if attn is not None:
        at, w_o = attn
        specs = [pl.BlockSpec((1, D, tm), lambda b, s: (b, 0, s)), _const_spec((D, D))] + specs
        args = [at, w_o.astype(BF16)] + args
        kern = _attn_out_mlp_kernel
    return pl.pallas_call(
        functools.partial(kern, final_norm=final_norm),
        out_shape=jax.ShapeDtypeStruct((B, S, D), F32),
        grid=(B, S // tm),
        in_specs=specs,
        out_specs=row_spec,
        compiler_params=pltpu.CompilerParams(
            dimension_semantics=("parallel", "parallel"), vmem_limit_bytes=VMEM_LIMIT),
        name="mlp_layer",
    )(*args)


def _kv_kernel(x_ref, g_ref, wk_ref, wv_ref, k_ref, vt_ref):
    for r0 in range(0, x_ref.shape[1], PROJ_ROWS):
        rows = slice(r0, r0 + PROJ_ROWS)
        hn = _rmsnorm(x_ref[0, rows, :], g_ref[...], NORM_EPS).astype(BF16)
        k = jnp.dot(hn, wk_ref[...], preferred_element_type=F32).astype(BF16)
        k_ref[0, r0 // 2:(r0 + PROJ_ROWS) // 2, :] = pltpu.bitcast(k, jnp.uint32)
        v = jnp.dot(hn, wv_ref[...], preferred_element_type=F32)
        vt_ref[0, :, rows] = pltpu.bitcast(v.T.astype(BF16), jnp.uint32)


def _kv_proj(x, g, w_kv, *, tm=1024):
    B, S, D = x.shape
    return pl.pallas_call(
        _kv_kernel,
        out_shape=(jax.ShapeDtypeStruct((B, S // 2, D), jnp.uint32),
                   jax.ShapeDtypeStruct((B, D // 2, S), jnp.uint32)),
        grid=(B, S // tm),
        in_specs=[
            pl.BlockSpec((1, tm, D), lambda b, s: (b, s, 0)),
            _const_spec((1, D)), _const_spec((D, D)), _const_spec((D, D)),
        ],
        out_specs=(pl.BlockSpec((1, tm // 2, D), lambda b, s: (b, s, 0)),
                   pl.BlockSpec((1, D // 2, tm), lambda b, s: (b, 0, s))),
        compiler_params=pltpu.CompilerParams(
            dimension_semantics=("parallel", "parallel"), vmem_limit_bytes=VMEM_LIMIT),
        name="kv_proj",
    )(x, g.reshape(1, D), w_kv[:, :D].astype(BF16), w_kv[:, D:].astype(BF16))


def _q_kernel(x_ref, g_ref, wq_ref, qt_ref, *, scale):
    for r0 in range(0, x_ref.shape[1], PROJ_ROWS):
        rows = slice(r0, r0 + PROJ_ROWS)
        hn = _rmsnorm(x_ref[0, rows, :], g_ref[...], NORM_EPS).astype(BF16)
        q = jnp.dot(hn, wq_ref[...], preferred_element_type=F32) * scale
        qt_ref[0, :, rows] = q.T.astype(BF16)


def _q_proj(x, g, w_q, *, tm=1024):
    B, S, D = x.shape
    kern = functools.partial(_q_kernel, scale=LOG2E * HEAD_DIM ** -0.5)
    return pl.pallas_call(
        kern,
        out_shape=jax.ShapeDtypeStruct((B, D, S), BF16),
        grid=(B, S // tm),
        in_specs=[
            pl.BlockSpec((1, tm, D), lambda b, s: (b, s, 0)),
            _const_spec((1, D)), _const_spec((D, D)),
        ],
        out_specs=pl.BlockSpec((1, D, tm), lambda b, s: (b, 0, s)),
        compiler_params=pltpu.CompilerParams(
            dimension_semantics=("parallel", "parallel"), vmem_limit_bytes=VMEM_LIMIT),
        name="q_proj",
    )(x, g.reshape(1, D), w_q.astype(BF16))


def _bf16_terms(c, n=3):
    terms, rest = [], np.float32(c)
    for _ in range(n):
        t = np.float32(np.asarray(rest, dtype=BF16))
        terms.append(float(t))
        rest = np.float32(rest - t)
    return terms


def _attn_kernel(qt_ref, k_ref, vt_ref, lam_ref, sg_ref, ot_ref,
                 qaug_ref, kaug_ref, sc_ref, m_ref, acc_ref, *, tq, lam_init, slopes):
    qi = pl.program_id(1)
    tk = tq
    LA = SCORE_LOOKAHEAD
    log2_slopes = [s * LOG2E for s in slopes]

    @pl.when((pl.program_id(0) == 0) & (qi == 0))
    def _():
        arow = lax.broadcasted_iota(jnp.int32, (V_DIM, tq), 0)
        dq_row = lax.broadcasted_iota(jnp.int32, (V_DIM, tq), 1).astype(F32)
        col = lax.broadcasted_iota(jnp.int32, (tk, V_DIM), 1)
        dk_col = lax.broadcasted_iota(jnp.int32, (tk, V_DIM), 0).astype(F32)
        for h in range(N_HEADS):
            c = _bf16_terms(log2_slopes[h])
            q_aug = jnp.where(arow == 0, c[0], jnp.where(arow == 1, c[1], jnp.where(
                arow == 2, c[2], jnp.where(arow < 6, dq_row, 0.0)))).astype(BF16)
            kaug_ref[h] = jnp.where(col < 3, dk_col, jnp.where(col == 3, -c[0], jnp.where(
                col == 4, -c[1], jnp.where(col == 5, -c[2], 0.0)))).astype(BF16)
            qaug_ref[2 * h, V_DIM:, :] = q_aug
            qaug_ref[2 * h + 1, V_DIM:, :] = q_aug

    row = lax.broadcasted_iota(jnp.int32, (V_DIM, tq), 0)
    for h in range(N_HEADS):
        qt = qt_ref[0, h * V_DIM:(h + 1) * V_DIM, :]
        zero = jnp.zeros_like(qt)
        qaug_ref[2 * h, 0:V_DIM, :] = jnp.where(row < HEAD_DIM, qt, zero)
        qaug_ref[2 * h + 1, 0:V_DIM, :] = jnp.where(row >= HEAD_DIM, qt, zero)

    ones_rows = jnp.ones((ONES_ROWS, tk), BF16)

    def issue_scores(j, h):
        off2 = pl.multiple_of(j * (tk // 2), tk // 2)
        kt = _bf16(k_ref[0, pl.ds(off2, tk // 2), h * V_DIM:(h + 1) * V_DIM])
        lhs = jnp.concatenate([kt, kaug_ref[h]], axis=1)
        for c in range(2):
            sc_ref[h % LA, c] = jnp.dot(lhs, qaug_ref[2 * h + c], preferred_element_type=F32)

    def softmax_pv(j, h, future):
        off = pl.multiple_of(j * tk, tk)
        vt = _bf16(vt_ref[0, h * V_DIM // 2:(h + 1) * V_DIM // 2, pl.ds(off, tk)])
        vta = jnp.concatenate([vt, ones_rows], axis=0)
        tile_bias = -log2_slopes[h] * ((qi - j) * tk).astype(F32)
        for c in range(2):
            idx = 2 * h + c
            ps, alphas = [], []
            for q0 in range(0, tq, SOFTMAX_LANES):
                qs = slice(q0, q0 + SOFTMAX_LANES)
                sc = sc_ref[h % LA, c, :, qs]
                if future is not None:
                    sc = jnp.where(future[:, qs], -jnp.inf, sc)
                m_old = m_ref[idx, :, qs]
                m_new = jnp.maximum(m_old, jnp.max(sc, axis=0, keepdims=True) + tile_bias)
                alphas.append(jnp.exp2(m_old - m_new))
                ps.append(jnp.exp2(sc - (m_new - tile_bias)).astype(BF16))
                m_ref[idx, :, qs] = m_new
            p = jnp.concatenate(ps, axis=1)
            alpha = jnp.concatenate(alphas, axis=1)
            acc_ref[idx] = alpha * acc_ref[idx] + jnp.dot(vta, p, preferred_element_type=F32)

    for h in range(LA):
        issue_scores(0, h)
    m_ref[...] = jnp.full(m_ref.shape, -jnp.inf, F32)
    acc_ref[...] = jnp.zeros(acc_ref.shape, F32)

    def unmasked_tile(j):
        for h in range(N_HEADS):
            softmax_pv(j, h, None)
            if h + LA < N_HEADS:
                issue_scores(j, h + LA)
            else:
                issue_scores(j + 1, h + LA - N_HEADS)

    def tile_group2(i, carry):
        for t in range(2 * KV_UNROLL):
            unmasked_tile(2 * KV_UNROLL * i + t)
        return carry

    def tile_group(i, carry):
        base = (qi // (2 * KV_UNROLL)) * (2 * KV_UNROLL)
        for t in range(KV_UNROLL):
            unmasked_tile(base + t)
        return carry

    lax.fori_loop(0, qi // (2 * KV_UNROLL), tile_group2, 0)
    lax.fori_loop(0, (qi % (2 * KV_UNROLL)) // KV_UNROLL, tile_group, 0)

    lv = lam_ref[0]
    lam = (jnp.exp(jnp.sum(lv[0:1] * lv[1:2], keepdims=True))
           - jnp.exp(jnp.sum(lv[2:3] * lv[3:4], keepdims=True)) + lam_init)

    def finish_head(h):
        a1 = acc_ref[2 * h]
        a2 = acc_ref[2 * h + 1]
        o = (a1[0:V_DIM] / a1[V_DIM:V_DIM + 1]) - lam * (a2[0:V_DIM] / a2[V_DIM:V_DIM + 1])
        o = o * lax.rsqrt(jnp.mean(o * o, axis=0, keepdims=True) + SUBLN_EPS) * sg_ref[...]
        ot_ref[0, h * V_DIM:(h + 1) * V_DIM, :] = (o * (1.0 - lam_init)).astype(BF16)

    def tail(n_unmasked):
        base = qi - n_unmasked
        for t in range(n_unmasked):
            unmasked_tile(base + t)
        future = (lax.broadcasted_iota(jnp.int32, (tk, tq), 0)
                  > lax.broadcasted_iota(jnp.int32, (tk, tq), 1))
        for h in range(N_HEADS):
            softmax_pv(qi, h, future)
            if h + LA < N_HEADS:
                issue_scores(qi, h + LA)
            finish_head(h)

    for r in range(KV_UNROLL):
        pl.when(qi % KV_UNROLL == r)(functools.partial(tail, r))


def _diff_attention(qt, k, vt, lam_vecs, subln_g, slopes, lam_init, *, tq=256):
    B, D, S = qt.shape
    assert N_HEADS % SCORE_LOOKAHEAD == 0 and tq <= 256
    kern = functools.partial(_attn_kernel, tq=tq, lam_init=lam_init, slopes=slopes)
    return pl.pallas_call(
        kern,
        out_shape=jax.ShapeDtypeStruct((B, D, S), BF16),
        grid=(B, S // tq),
        in_specs=[
            pl.BlockSpec((1, D, tq), lambda b, i: (b, 0, i)),
            pl.BlockSpec((1, S // 2, D), lambda b, i: (b, 0, 0)),
            pl.BlockSpec((1, D // 2, S), lambda b, i: (b, 0, 0)),
            _const_spec((1, 4, HEAD_DIM)),
            _const_spec((V_DIM, 1)),
        ],
        out_specs=pl.BlockSpec((1, D, tq), lambda b, i: (b, 0, i)),
        scratch_shapes=[
            pltpu.VMEM((2 * N_HEADS, 2 * V_DIM, tq), BF16),
            pltpu.VMEM((N_HEADS, tq, V_DIM), BF16),
            pltpu.VMEM((SCORE_LOOKAHEAD, 2, tq, tq), F32),
            pltpu.VMEM((2 * N_HEADS, 1, tq), F32),
            pltpu.VMEM((2 * N_HEADS, V_DIM + ONES_ROWS, tq), F32),
        ],
        compiler_params=pltpu.CompilerParams(
            dimension_semantics=("arbitrary", "arbitrary"), vmem_limit_bytes=VMEM_LIMIT),
        name="diff_attention",
    )(qt, k, vt, lam_vecs.reshape(1, 4, HEAD_DIM), subln_g.reshape(V_DIM, 1))


def _attention_heads(x, k_v, g, w_q, lam_vecs, subln_g, slopes, lam_init):
    k, vt = k_v
    qt = _q_proj(x, g, w_q)
    return _diff_attention(qt, k, vt, lam_vecs, subln_g, slopes, lam_init)


def kernel(x, a_norm, a_w_in, a_b_in, a_conv_w, a_conv_b, a_gate_w, a_gate_b, a_lambda, a_w_out,
           kv_norm, w_kv, b_norm, b_w_q, b_lam, b_subln, b_w_o,
           mlp_norm, mlp_w1, mlp_w2, final_norm):
    depth = mlp_w1.shape[0]
    n_a = a_w_in.shape[0]
    slopes = tuple(2.0 ** (-8.0 * (h + 1) / N_HEADS) for h in range(N_HEADS))
    k_v = None
    for l in range(depth):
        if l < n_a:
            x = _recurrent_mlp_layer(x, a_norm[l], a_w_in[l], a_b_in[l], a_conv_w[l], a_conv_b[l],
                                     a_gate_w[l], a_gate_b[l], a_lambda[l], a_w_out[l],
                                     mlp_norm[l], mlp_w1[l], mlp_w2[l])
        else:
            j = l - n_a
            lam_init = 0.8 - 0.6 * math.exp(-0.3 * l)
            heads = _attention_heads(x, k_v, b_norm[j], b_w_q[j], b_lam[j], b_subln[j], slopes, lam_init)
            x = _mlp_layer(x, mlp_norm[l], mlp_w1[l], mlp_w2[l], final_norm,
                           final_norm=(l == depth - 1), attn=(heads, b_w_o[j]))
        if l == n_a - 1:
            k_v = _kv_proj(x, kv_norm, w_kv)
    return x
```

```python
import functools
import math

import jax
import jax.numpy as jnp
import numpy as np
from jax import lax
from jax.experimental import pallas as pl
from jax.experimental.pallas import tpu as pltpu

D_MODEL = 1024
N_HEADS = 8
HEAD_DIM = 64
V_DIM = 128
LRU_BLOCKS = 8
LRU_BLOCK_W = 128
CONV_W = 4
LRU_C = 8.0
NORM_EPS = 1e-6
SUBLN_EPS = 1e-5

SUBLANES = 8
ONES_ROWS = 16
SCORE_LOOKAHEAD = 4
KV_UNROLL = 4
LOG2E = math.log2(math.e)
REC_COLS = 256
REC_GATE_LOOKAHEAD = 2
TILES_PER_STEP = 2
PROJ_ROWS = 256
VMEM_LIMIT = 56 * 1024 * 1024

BF16 = jnp.bfloat16
F32 = jnp.float32


def _rmsnorm(x, g, eps):
    return x * lax.rsqrt(jnp.mean(x * x, axis=-1, keepdims=True) + eps) * g


def _bf16(packed):
    return pltpu.bitcast(packed, BF16)


def _const_spec(shape):
    nd = len(shape)
    return pl.BlockSpec(shape, lambda *_: (0,) * nd, pipeline_mode=pl.Buffered(1))


def _gelu_tanh(x):
    c1 = math.sqrt(2.0 / math.pi)
    return x * (0.5 + 0.5 * jnp.tanh(x * (c1 + (c1 * 0.044715) * (x * x))))


def _rec_mlp_kernel(*refs, ts, tiles_per_seq):
    for sub in range(TILES_PER_STEP):
        _rec_mlp_step(*refs, ts=ts, tiles_per_seq=tiles_per_seq, sub=sub)


def _rec_mlp_step(x_ref, g_ref, wx_ref, wy_ref, bx_ref, by_ref, cw_ref, cb_ref, gw_ref, gb_ref,
                  lam_ref, wo_ref, perm_ref, unperm_ref, xres_ref, mg_ref, w1_ref, w2_ref, o_ref,
                  hn_s, xb_s, yb_s, xmid_s, xpad_ref, tail_ref, h_ref, *, ts, tiles_per_seq, sub):
    f = pl.program_id(0)
    W = D_MODEL
    G = ts // SUBLANES
    HALO = (CONV_W - 1) * SUBLANES
    NB = W // REC_COLS
    rows = slice(sub * ts, (sub + 1) * ts)
    starts_sequence = lax.rem(TILES_PER_STEP * f + sub - 2, tiles_per_seq) == 0

    if sub == 0:
        @pl.when(f == 0)
        def _():
            hn_s[...] = jnp.zeros_like(hn_s)
            xb_s[...] = jnp.zeros_like(xb_s)
            yb_s[...] = jnp.zeros_like(yb_s)
            xmid_s[...] = jnp.zeros_like(xmid_s)
            tail_ref[...] = jnp.zeros_like(tail_ref)
            h_ref[...] = jnp.zeros_like(h_ref)

    row8 = lax.broadcasted_iota(jnp.int32, (SUBLANES, REC_COLS), 0)

    def norm_permute():
        hn = _rmsnorm(x_ref[0, rows, :], g_ref[...], NORM_EPS).astype(BF16)
        hn_s[...] = jnp.dot(perm_ref[...], hn, preferred_element_type=F32).astype(BF16)

    def in_proj(cb):
        cs = slice(cb * REC_COLS, (cb + 1) * REC_COLS)
        hn = hn_s[...]
        xb_s[:, cs] = jnp.dot(hn, wx_ref[:, cs], preferred_element_type=F32) + bx_ref[:, cs]
        yb_s[:, cs] = jnp.dot(hn, wy_ref[:, cs], preferred_element_type=F32) + by_ref[:, cs]

    def load_proj(cb):
        cs = slice(cb * REC_COLS, (cb + 1) * REC_COLS)
        return xb_s[:, cs], yb_s[:, cs]

    def conv_gates(cb, xb, yb):
        cs = slice(cb * REC_COLS, (cb + 1) * REC_COLS)
        for k in range(CONV_W - 1):
            cur = xb[ts - HALO + k * SUBLANES:ts - HALO + (k + 1) * SUBLANES, :]
            prev = jnp.where(starts_sequence, 0.0, tail_ref[k * SUBLANES:(k + 1) * SUBLANES, cs])
            xpad_ref[k * SUBLANES:(k + 1) * SUBLANES, cs] = jnp.where(
                row8 == 0, pltpu.roll(prev, 1, axis=0), pltpu.roll(cur, 1, axis=0))
        tail_ref[:, cs] = xb[ts - HALO:, :]
        xpad_ref[HALO:, cs] = xb
        xc = xb * cw_ref[CONV_W - 1:CONV_W, cs] + cb_ref[:, cs]
        for j in range(CONV_W - 1):
            xc = xc + xpad_ref[j * SUBLANES:j * SUBLANES + ts, cs] * cw_ref[j:j + 1, cs]
        xcb = xc.astype(BF16)
        r_parts, i_parts = [], []
        for n in range(REC_COLS // LRU_BLOCK_W):
            gn = jnp.dot(xcb[:, n * LRU_BLOCK_W:(n + 1) * LRU_BLOCK_W],
                         gw_ref[cb * (REC_COLS // LRU_BLOCK_W) + n], preferred_element_type=F32)
            r_parts.append(gn[:, :LRU_BLOCK_W])
            i_parts.append(gn[:, LRU_BLOCK_W:])
        return xc, jnp.concatenate(r_parts, axis=1), jnp.concatenate(i_parts, axis=1), yb

    def recurrence(cb, xc, g_r, g_i, yb):
        cs = slice(cb * REC_COLS, (cb + 1) * REC_COLS)
        r = jax.nn.sigmoid(g_r + gb_ref[0:1, cs])
        ig = jax.nn.sigmoid(g_i + gb_ref[1:2, cs])
        z = -lam_ref[:, cs]
        softplus = jnp.maximum(z, 0.0) + jnp.log1p(jnp.exp(-jnp.abs(z)))
        log_a = (-LRU_C * softplus) * r
        a = jnp.exp(log_a)
        bx = jnp.sqrt(-jnp.tanh(log_a) * (a * a + 1.0)) * (ig * xc)

        h_loc = jnp.zeros((SUBLANES, REC_COLS), F32)
        prod = jnp.ones((SUBLANES, REC_COLS), F32)
        h_steps, p_steps = [], []
        for g in range(G):
            a_g = a[g * SUBLANES:(g + 1) * SUBLANES, :]
            h_loc = a_g * h_loc + bx[g * SUBLANES:(g + 1) * SUBLANES, :]
            prod = a_g * prod
            h_steps.append(h_loc)
            p_steps.append(prod)

        t_cum, f_cum = prod, h_loc
        for d in (1, 2, 4):
            keep = row8 >= d
            f_cum = jnp.where(keep, t_cum * pltpu.roll(f_cum, d, axis=0) + f_cum, f_cum)
            t_cum = jnp.where(keep, t_cum * pltpu.roll(t_cum, d, axis=0), t_cum)
        h_prev = jnp.where(starts_sequence, 0.0, h_ref[SUBLANES - 1:SUBLANES, cs])
        h_end = t_cum * h_prev + f_cum
        h_in = jnp.where(row8 == 0, h_prev, pltpu.roll(h_end, 1, axis=0))
        h_ref[:, cs] = h_end
        hs = jnp.concatenate([h_steps[g] + p_steps[g] * h_in for g in range(G)], axis=0)
        return (hs * _gelu_tanh(yb)).astype(BF16)

    def unpermute(y):
        return jnp.dot(unperm_ref[...], y, preferred_element_type=F32).astype(BF16)

    def out_proj(cb, y):
        return jnp.dot(y, wo_ref[cb * REC_COLS:(cb + 1) * REC_COLS, :], preferred_element_type=F32)

    F = w1_ref.shape[1]
    fchunk = F // NB
    xm = xmid_s[sub]
    hm = _rmsnorm(xm, mg_ref[...], NORM_EPS).astype(BF16)

    def mlp_up(fc):
        fs = slice(fc * fchunk, (fc + 1) * fchunk)
        u = jnp.maximum(jnp.dot(hm, w1_ref[:, fs], preferred_element_type=F32), 0.0)
        return (u * u).astype(BF16)

    def mlp_down(fc, u):
        return jnp.dot(u, w2_ref[fc * fchunk:(fc + 1) * fchunk, :], preferred_element_type=F32)

    loaded = {cb: load_proj(cb) for cb in range(NB)}
    for cb in range(NB):
        in_proj(cb)
    norm_permute()
    up = {0: mlp_up(0)}
    gated = {cb: conv_gates(cb, *loaded.pop(cb)) for cb in range(REC_GATE_LOOKAHEAD)}
    out = xres_ref[0, rows, :]
    mlp_out = xm
    y_nat = {}
    for cb in range(NB):
        if cb + 1 < NB:
            up[cb + 1] = mlp_up(cb + 1)
        mlp_out = mlp_out + mlp_down(cb, up.pop(cb))
        nxt = cb + REC_GATE_LOOKAHEAD
        if nxt < NB:
            gated[nxt] = conv_gates(nxt, *loaded.pop(nxt))
        y_nat[cb] = unpermute(recurrence(cb, *gated.pop(cb)))
        if cb > 0:
            out = out + out_proj(cb - 1, y_nat.pop(cb - 1))
    out = out + out_proj(NB - 1, y_nat.pop(NB - 1))
    o_ref[0, rows, :] = mlp_out
    xmid_s[sub] = out


def _chunk_permutation(ts):
    steps = ts // SUBLANES
    p = np.zeros((ts, ts), np.float32)
    for c in range(SUBLANES):
        for s in range(steps):
            p[s * SUBLANES + c, c * steps + s] = 1.0
    return p


def _recurrent_mlp_layer(x, g, w_in, b_in, conv_w, conv_b, gate_w, gate_b, lam, w_out,
                         mlp_g, w1, w2, *, ts=256):
    B, S, D = x.shape
    F = w1.shape[1]
    W = D_MODEL
    assert TILES_PER_STEP == 2 and S % (TILES_PER_STEP * ts) == 0
    n_seq = S // (TILES_PER_STEP * ts)
    n_blocks = B * n_seq
    wx = w_in[:, :W].astype(BF16)
    wy = w_in[:, W:].astype(BF16)
    bxv = b_in[:W].reshape(1, W)
    byv = b_in[W:].reshape(1, W)
    gw = jnp.concatenate([gate_w[0], gate_w[1]], axis=-1).astype(BF16)
    perm = _chunk_permutation(ts)
    kern = functools.partial(_rec_mlp_kernel, ts=ts, tiles_per_seq=TILES_PER_STEP * n_seq)

    def in_tile(f):
        t = jnp.minimum(f, n_blocks - 1)
        return (t // n_seq, t % n_seq, 0)

    def res_tile(f):
        t = jnp.clip(f - 1, 0, n_blocks - 1)
        return (t // n_seq, t % n_seq, 0)

    def out_tile(f):
        t = jnp.maximum(f - 2, 0)
        return (t // n_seq, t % n_seq, 0)

    return pl.pallas_call(
        kern,
        out_shape=jax.ShapeDtypeStruct((B, S, D), F32),
        grid=(n_blocks + 2,),
        in_specs=[
            pl.BlockSpec((1, TILES_PER_STEP * ts, D), in_tile),
            _const_spec((1, D)),
            _const_spec((D, W)), _const_spec((D, W)),
            _const_spec((1, W)), _const_spec((1, W)),
            _const_spec((CONV_W, W)), _const_spec((1, W)),
            _const_spec((LRU_BLOCKS, LRU_BLOCK_W, 2 * LRU_BLOCK_W)),
            _const_spec((2, W)),
            _const_spec((1, W)),
            _const_spec((W, D)),
            _const_spec((ts, ts)), _const_spec((ts, ts)),
            pl.BlockSpec((1, TILES_PER_STEP * ts, D), res_tile),
            _const_spec((1, D)), _const_spec((D, F)), _const_spec((F, D)),
        ],
        out_specs=pl.BlockSpec((1, TILES_PER_STEP * ts, D), out_tile),
        scratch_shapes=[
            pltpu.VMEM((ts, D), BF16),
            pltpu.VMEM((ts, W), F32),
            pltpu.VMEM((ts, W), F32),
            pltpu.VMEM((TILES_PER_STEP, ts, D), F32),
            pltpu.VMEM((ts + (CONV_W - 1) * SUBLANES, W), F32),
            pltpu.VMEM(((CONV_W - 1) * SUBLANES, W), F32),
            pltpu.VMEM((SUBLANES, W), F32),
        ],
        compiler_params=pltpu.CompilerParams(
            dimension_semantics=("arbitrary",), vmem_limit_bytes=VMEM_LIMIT),
        name="rglru_mlp_layer",
    )(x, g.reshape(1, D), wx, wy, bxv, byv, conv_w, conv_b.reshape(1, W), gw, gate_b,
      lam.reshape(1, W), w_out.astype(BF16), jnp.asarray(perm, BF16), jnp.asarray(perm.T, BF16), x,
      mlp_g.reshape(1, D), w1.astype(BF16), w2.astype(BF16))


def _mlp_halves(xs, g_ref, w1_ref, w2_ref, fg_ref, o_ref, final_norm):
    hs = [_rmsnorm(x, g_ref[...], NORM_EPS).astype(BF16) for x in xs]
    us = [jnp.maximum(jnp.dot(h, w1_ref[...], preferred_element_type=F32), 0.0) for h in hs]
    us = [(u * u).astype(BF16) for u in us]
    ys = [x + jnp.dot(u, w2_ref[...], preferred_element_type=F32) for x, u in zip(xs, us)]
    half = xs[0].shape[0]
    for i, y in enumerate(ys):
        if final_norm:
            y = _rmsnorm(y, fg_ref[...], NORM_EPS)
        o_ref[0, i * half:(i + 1) * half, :] = y


def _row_halves(n):
    return [slice(0, n // 2), slice(n // 2, n)]


def _mlp_kernel(x_ref, g_ref, w1_ref, w2_ref, fg_ref, o_ref, *, final_norm):
    xs = [x_ref[0, rows, :] for rows in _row_halves(x_ref.shape[1])]
    _mlp_halves(xs, g_ref, w1_ref, w2_ref, fg_ref, o_ref, final_norm)


def _attn_out_mlp_kernel(at_ref, wo_ref, x_ref, g_ref, w1_ref, w2_ref, fg_ref, o_ref, *, final_norm):
    xs = [x_ref[0, rows, :] + lax.dot_general(at_ref[0, :, rows], wo_ref[...], (((0,), (0,)), ((), ())),
                                              preferred_element_type=F32)
          for rows in _row_halves(x_ref.shape[1])]
    _mlp_halves(xs, g_ref, w1_ref, w2_ref, fg_ref, o_ref, final_norm)


def _mlp_layer(x, g, w1, w2, final_g, *, final_norm, attn=None, tm=512):
    B, S, D = x.shape
    F = w1.shape[1]
    row_spec = pl.BlockSpec((1, tm, D), lambda b, s: (b, s, 0))
    specs = [row_spec, _const_spec((1, D)), _const_spec((D, F)), _const_spec((F, D)),
             _const_spec((1, D))]
    args = [x, g.reshape(1, D), w1.astype(BF16), w2.astype(BF16), final_g.reshape(1, D)]
    kern = _mlp_kernel
    if attn is not None:
        at, w_o = attn
        specs = [pl.BlockSpec((1, D, tm), lambda b, s: (b, 0, s)), _const_spec((D, D))] + specs
        args = [at, w_o.astype(BF16)] + args
        kern = _attn_out_mlp_kernel
    return pl.pallas_call(
        functools.partial(kern, final_norm=final_norm),
        out_shape=jax.ShapeDtypeStruct((B, S, D), F32),
        grid=(B, S // tm),
        in_specs=specs,
        out_specs=row_spec,
        compiler_params=pltpu.CompilerParams(
            dimension_semantics=("parallel", "parallel"), vmem_limit_bytes=VMEM_LIMIT),
        name="mlp_layer",
    )(*args)


def _kvq_kernel(x_ref, gkv_ref, gq_ref, wk_ref, wv_ref, wq_ref, k_ref, vt_ref, qt_ref, *, scale):
    for r0 in range(0, x_ref.shape[1], PROJ_ROWS):
        rows = slice(r0, r0 + PROJ_ROWS)
        x = x_ref[0, rows, :]
        xn = x * lax.rsqrt(jnp.mean(x * x, axis=-1, keepdims=True) + NORM_EPS)
        hn = (xn * gkv_ref[...]).astype(BF16)
        k = jnp.dot(hn, wk_ref[...], preferred_element_type=F32).astype(BF16)
        k_ref[0, r0 // 2:(r0 + PROJ_ROWS) // 2, :] = pltpu.bitcast(k, jnp.uint32)
        v = jnp.dot(hn, wv_ref[...], preferred_element_type=F32)
        vt_ref[0, :, rows] = pltpu.bitcast(v.T.astype(BF16), jnp.uint32)
        hq = (xn * gq_ref[...]).astype(BF16)
        q = jnp.dot(hq, wq_ref[...], preferred_element_type=F32) * scale
        qt_ref[0, :, rows] = q.T.astype(BF16)


def _kvq_proj(x, g_kv, w_kv, g_q, w_q, *, tm=1024):
    B, S, D = x.shape
    kern = functools.partial(_kvq_kernel, scale=LOG2E * HEAD_DIM ** -0.5)
    k, vt, qt = pl.pallas_call(
        kern,
        out_shape=(jax.ShapeDtypeStruct((B, S // 2, D), jnp.uint32),
                   jax.ShapeDtypeStruct((B, D // 2, S), jnp.uint32),
                   jax.ShapeDtypeStruct((B, D, S), BF16)),
        grid=(B, S // tm),
        in_specs=[
            pl.BlockSpec((1, tm, D), lambda b, s: (b, s, 0)),
            _const_spec((1, D)), _const_spec((1, D)),
            _const_spec((D, D)), _const_spec((D, D)), _const_spec((D, D)),
        ],
        out_specs=(pl.BlockSpec((1, tm // 2, D), lambda b, s: (b, s, 0)),
                   pl.BlockSpec((1, D // 2, tm), lambda b, s: (b, 0, s)),
                   pl.BlockSpec((1, D, tm), lambda b, s: (b, 0, s))),
        compiler_params=pltpu.CompilerParams(
            dimension_semantics=("parallel", "parallel"), vmem_limit_bytes=VMEM_LIMIT),
        name="kvq_proj",
    )(x, g_kv.reshape(1, D), g_q.reshape(1, D), w_kv[:, :D].astype(BF16), w_kv[:, D:].astype(BF16),
      w_q.astype(BF16))
    return (k, vt), qt


def _q_kernel(x_ref, g_ref, wq_ref, qt_ref, *, scale):
    for r0 in range(0, x_ref.shape[1], PROJ_ROWS):
        rows = slice(r0, r0 + PROJ_ROWS)
        hn = _rmsnorm(x_ref[0, rows, :], g_ref[...], NORM_EPS).astype(BF16)
        q = jnp.dot(hn, wq_ref[...], preferred_element_type=F32) * scale
        qt_ref[0, :, rows] = q.T.astype(BF16)


def _q_proj(x, g, w_q, *, tm=1024):
    B, S, D = x.shape
    kern = functools.partial(_q_kernel, scale=LOG2E * HEAD_DIM ** -0.5)
    return pl.pallas_call(
        kern,
        out_shape=jax.ShapeDtypeStruct((B, D, S), BF16),
        grid=(B, S // tm),
        in_specs=[
            pl.BlockSpec((1, tm, D), lambda b, s: (b, s, 0)),
            _const_spec((1, D)), _const_spec((D, D)),
        ],
        out_specs=pl.BlockSpec((1, D, tm), lambda b, s: (b, 0, s)),
        compiler_params=pltpu.CompilerParams(
            dimension_semantics=("parallel", "parallel"), vmem_limit_bytes=VMEM_LIMIT),
        name="q_proj",
    )(x, g.reshape(1, D), w_q.astype(BF16))


def _bf16_terms(c, n=3):
    terms, rest = [], np.float32(c)
    for _ in range(n):
        t = np.float32(np.asarray(rest, dtype=BF16))
        terms.append(float(t))
        rest = np.float32(rest - t)
    return terms


def _attn_kernel(qt_ref, k_ref, vt_ref, lam_ref, sg_ref, ot_ref,
                 qaug_ref, kaug_ref, sc_ref, m_ref, acc_ref, *, tq, lam_init, slopes):
    qi = pl.program_id(1)
    tk = tq
    LA = SCORE_LOOKAHEAD
    log2_slopes = [s * LOG2E for s in slopes]

    @pl.when((pl.program_id(0) == 0) & (qi == 0))
    def _():
        arow = lax.broadcasted_iota(jnp.int32, (V_DIM, tq), 0)
        dq_row = lax.broadcasted_iota(jnp.int32, (V_DIM, tq), 1).astype(F32)
        col = lax.broadcasted_iota(jnp.int32, (tk, V_DIM), 1)
        dk_col = lax.broadcasted_iota(jnp.int32, (tk, V_DIM), 0).astype(F32)
        for h in range(N_HEADS):
            c = _bf16_terms(log2_slopes[h])
            q_aug = jnp.where(arow == 0, c[0], jnp.where(arow == 1, c[1], jnp.where(
                arow == 2, c[2], jnp.where(arow < 6, dq_row, 0.0)))).astype(BF16)
            kaug_ref[h] = jnp.where(col < 3, dk_col, jnp.where(col == 3, -c[0], jnp.where(
                col == 4, -c[1], jnp.where(col == 5, -c[2], 0.0)))).astype(BF16)
            qaug_ref[2 * h, V_DIM:, :] = q_aug
            qaug_ref[2 * h + 1, V_DIM:, :] = q_aug

    row = lax.broadcasted_iota(jnp.int32, (V_DIM, tq), 0)
    for h in range(N_HEADS):
        qt = qt_ref[0, h * V_DIM:(h + 1) * V_DIM, :]
        zero = jnp.zeros_like(qt)
        qaug_ref[2 * h, 0:V_DIM, :] = jnp.where(row < HEAD_DIM, qt, zero)
        qaug_ref[2 * h + 1, 0:V_DIM, :] = jnp.where(row >= HEAD_DIM, qt, zero)

    ones_rows = jnp.ones((ONES_ROWS, tk), BF16)

    def issue_scores(j, h):
        off2 = pl.multiple_of(j * (tk // 2), tk // 2)
        kt = _bf16(k_ref[0, pl.ds(off2, tk // 2), h * V_DIM:(h + 1) * V_DIM])
        lhs = jnp.concatenate([kt, kaug_ref[h]], axis=1)
        for c in range(2):
            sc_ref[h % LA, c] = jnp.dot(lhs, qaug_ref[2 * h + c], preferred_element_type=F32)

    def softmax_pv(j, h, future):
        off = pl.multiple_of(j * tk, tk)
        vt = _bf16(vt_ref[0, h * V_DIM // 2:(h + 1) * V_DIM // 2, pl.ds(off, tk)])
        vta = jnp.concatenate([vt, ones_rows], axis=0)
        tile_bias = -log2_slopes[h] * ((qi - j) * tk).astype(F32)
        for c in range(2):
            idx = 2 * h + c
            sc = sc_ref[h % LA, c]
            if future is not None:
                sc = jnp.where(future, -jnp.inf, sc)
            m_old = m_ref[idx]
            m_new = jnp.maximum(m_old, jnp.max(sc, axis=0, keepdims=True) + tile_bias)
            alpha = jnp.exp2(m_old - m_new)
            p = jnp.exp2(sc - (m_new - tile_bias)).astype(BF16)
            acc_ref[idx] = alpha * acc_ref[idx] + jnp.dot(vta, p, preferred_element_type=F32)
            m_ref[idx] = m_new

    for h in range(LA):
        issue_scores(0, h)
    m_ref[...] = jnp.full(m_ref.shape, -jnp.inf, F32)
    acc_ref[...] = jnp.zeros(acc_ref.shape, F32)

    def unmasked_tile(j):
        for h in range(N_HEADS):
            softmax_pv(j, h, None)
            if h + LA < N_HEADS:
                issue_scores(j, h + LA)
            else:
                issue_scores(j + 1, h + LA - N_HEADS)

    def tile_group2(i, carry):
        for t in range(2 * KV_UNROLL):
            unmasked_tile(2 * KV_UNROLL * i + t)
        return carry

    def tile_group(i, carry):
        base = (qi // (2 * KV_UNROLL)) * (2 * KV_UNROLL)
        for t in range(KV_UNROLL):
            unmasked_tile(base + t)
        return carry

    lax.fori_loop(0, qi // (2 * KV_UNROLL), tile_group2, 0)
    lax.fori_loop(0, (qi % (2 * KV_UNROLL)) // KV_UNROLL, tile_group, 0)

    lv = lam_ref[0]
    lam = (jnp.exp(jnp.sum(lv[0:1] * lv[1:2], keepdims=True))
           - jnp.exp(jnp.sum(lv[2:3] * lv[3:4], keepdims=True)) + lam_init)

    def finish_head(h):
        a1 = acc_ref[2 * h]
        a2 = acc_ref[2 * h + 1]
        o = (a1[0:V_DIM] / a1[V_DIM:V_DIM + 1]) - lam * (a2[0:V_DIM] / a2[V_DIM:V_DIM + 1])
        o = o * lax.rsqrt(jnp.mean(o * o, axis=0, keepdims=True) + SUBLN_EPS) * sg_ref[...]
        ot_ref[0, h * V_DIM:(h + 1) * V_DIM, :] = (o * (1.0 - lam_init)).astype(BF16)

    def tail(n_unmasked):
        base = qi - n_unmasked
        for t in range(n_unmasked):
            unmasked_tile(base + t)
        future = (lax.broadcasted_iota(jnp.int32, (tk, tq), 0)
                  > lax.broadcasted_iota(jnp.int32, (tk, tq), 1))
        for h in range(N_HEADS):
            softmax_pv(qi, h, future)
            if h + LA < N_HEADS:
                issue_scores(qi, h + LA)
            finish_head(h)

    for r in range(KV_UNROLL):
        pl.when(qi % KV_UNROLL == r)(functools.partial(tail, r))


def _diff_attention(qt, k, vt, lam_vecs, subln_g, slopes, lam_init, *, tq=256):
    B, D, S = qt.shape
    assert N_HEADS % SCORE_LOOKAHEAD == 0 and tq <= 256
    kern = functools.partial(_attn_kernel, tq=tq, lam_init=lam_init, slopes=slopes)
    return pl.pallas_call(
        kern,
        out_shape=jax.ShapeDtypeStruct((B, D, S), BF16),
        grid=(B, S // tq),
        in_specs=[
            pl.BlockSpec((1, D, tq), lambda b, i: (b, 0, i)),
            pl.BlockSpec((1, S // 2, D), lambda b, i: (b, 0, 0)),
            pl.BlockSpec((1, D // 2, S), lambda b, i: (b, 0, 0)),
            _const_spec((1, 4, HEAD_DIM)),
            _const_spec((V_DIM, 1)),
        ],
        out_specs=pl.BlockSpec((1, D, tq), lambda b, i: (b, 0, i)),
        scratch_shapes=[
            pltpu.VMEM((2 * N_HEADS, 2 * V_DIM, tq), BF16),
            pltpu.VMEM((N_HEADS, tq, V_DIM), BF16),
            pltpu.VMEM((SCORE_LOOKAHEAD, 2, tq, tq), F32),
            pltpu.VMEM((2 * N_HEADS, 1, tq), F32),
            pltpu.VMEM((2 * N_HEADS, V_DIM + ONES_ROWS, tq), F32),
        ],
        compiler_params=pltpu.CompilerParams(
            dimension_semantics=("arbitrary", "arbitrary"), vmem_limit_bytes=VMEM_LIMIT),
        name="diff_attention",
    )(qt, k, vt, lam_vecs.reshape(1, 4, HEAD_DIM), subln_g.reshape(V_DIM, 1))


def _attention_heads(x, k_v, g, w_q, lam_vecs, subln_g, slopes, lam_init, qt=None):
    k, vt = k_v
    if qt is None:
        qt = _q_proj(x, g, w_q)
    return _diff_attention(qt, k, vt, lam_vecs, subln_g, slopes, lam_init)


def kernel(x, a_norm, a_w_in, a_b_in, a_conv_w, a_conv_b, a_gate_w, a_gate_b, a_lambda, a_w_out,
           kv_norm, w_kv, b_norm, b_w_q, b_lam, b_subln, b_w_o,
           mlp_norm, mlp_w1, mlp_w2, final_norm):
    depth = mlp_w1.shape[0]
    n_a = a_w_in.shape[0]
    slopes = tuple(2.0 ** (-8.0 * (h + 1) / N_HEADS) for h in range(N_HEADS))
    k_v = first_qt = None
    for l in range(depth):
        if l < n_a:
            x = _recurrent_mlp_layer(x, a_norm[l], a_w_in[l], a_b_in[l], a_conv_w[l], a_conv_b[l],
                                     a_gate_w[l], a_gate_b[l], a_lambda[l], a_w_out[l],
                                     mlp_norm[l], mlp_w1[l], mlp_w2[l])
        else:
            j = l - n_a
            lam_init = 0.8 - 0.6 * math.exp(-0.3 * l)
            heads = _attention_heads(x, k_v, b_norm[j], b_w_q[j], b_lam[j], b_subln[j], slopes, lam_init,
                                     qt=first_qt if j == 0 else None)
            x = _mlp_layer(x, mlp_norm[l], mlp_w1[l], mlp_w2[l], final_norm,
                           final_norm=(l == depth - 1), attn=(heads, b_w_o[j]))
        if l == n_a - 1:
            k_v, first_qt = _kvq_proj(x, kv_norm, w_kv, b_norm[0], b_w_q[0])
    return x
```
